```python
import jax, jax.numpy as jnp
from jax import lax
import numpy as np

D_MODEL = 2048
BATCH = 4
SEQ = 4096
DEPTH = 1

MLA_HEADS = 16
Q_LORA_RANK = 512
KV_LORA_RANK = 512
QK_NOPE_DIM = 128
QK_ROPE_DIM = 64
V_HEAD_DIM = 128
ROPE_THETA = 10000.0
Q_BLOCK = 128
SSD_EXPAND = 2
SSD_D_INNER = SSD_EXPAND * D_MODEL
SSD_HEAD_DIM = 64
SSD_HEADS = SSD_D_INNER // SSD_HEAD_DIM
SSD_GROUPS = 8
SSD_STATE = 128
SSD_CONV = 5
SSD_CHUNK = 128
SSD_CONV_DIM = SSD_D_INNER + 2 * SSD_GROUPS * SSD_STATE
FFN_DIM = 5632
FFN_CONV = 3
EPS = 1e-6

IN_WIDTHS = (Q_LORA_RANK, KV_LORA_RANK, QK_ROPE_DIM,
             SSD_D_INNER, SSD_CONV_DIM, SSD_HEADS, SSD_HEADS,
             D_MODEL, D_MODEL)
IN_DIM = sum(IN_WIDTHS)
IN_SPLITS = [int(v) for v in np.cumsum(IN_WIDTHS)[:-1]]

kernel_name = "bidir_hybrid_mla_ssd_convffn"


def rms_norm(x, w):
    xf = x.astype(jnp.float32)
    y = xf * lax.rsqrt(jnp.mean(xf * xf, axis=-1, keepdims=True) + EPS)
    return (y * w.astype(jnp.float32)).astype(x.dtype)


def centred_dwconv(x, w, b):
    k, c = w.shape
    y = lax.conv_general_dilated(
        x, w[:, None, :].astype(x.dtype), window_strides=(1,),
        padding=[((k - 1) // 2, k // 2)],
        dimension_numbers=("NWC", "WIO", "NWC"), feature_group_count=c)
    return y + b.astype(x.dtype)


def rope_tables(positions):
    half = QK_ROPE_DIM // 2
    inv_freq = ROPE_THETA ** (-jnp.arange(half, dtype=jnp.float32) / half)
    ang = positions.astype(jnp.float32)[..., None] * inv_freq
    return jnp.cos(ang), jnp.sin(ang)


def apply_rope(x, cos, sin):
    x1, x2 = jnp.split(x.astype(jnp.float32), 2, axis=-1)
    return jnp.concatenate([x1 * cos - x2 * sin, x1 * sin + x2 * cos], axis=-1).astype(x.dtype)


def mla_branch(q_lat, kv_lat, k_rope, cos, sin, q_norm_w, w_uq, kv_norm_w, w_ukv, w_o_attn):
    bsz, s_len, _ = q_lat.shape
    q = (rms_norm(q_lat, q_norm_w) @ w_uq).reshape(bsz, s_len, MLA_HEADS, QK_NOPE_DIM + QK_ROPE_DIM)
    q_nope, q_rope = q[..., :QK_NOPE_DIM], q[..., QK_NOPE_DIM:]
    q_rope = apply_rope(q_rope, cos[:, :, None, :], sin[:, :, None, :])
    kv = (rms_norm(kv_lat, kv_norm_w) @ w_ukv).reshape(bsz, s_len, MLA_HEADS, QK_NOPE_DIM + V_HEAD_DIM)
    k_nope, v = kv[..., :QK_NOPE_DIM], kv[..., QK_NOPE_DIM:]
    k_rope = apply_rope(k_rope, cos, sin)
    scale = (QK_NOPE_DIM + QK_ROPE_DIM) ** -0.5
    nb = s_len // Q_BLOCK
    qn_blocks = q_nope.reshape(bsz, nb, Q_BLOCK, MLA_HEADS, QK_NOPE_DIM).transpose(1, 0, 2, 3, 4)
    qr_blocks = q_rope.reshape(bsz, nb, Q_BLOCK, MLA_HEADS, QK_ROPE_DIM).transpose(1, 0, 2, 3, 4)

    def attend(blk):
        qn, qr = blk
        sc = (jnp.einsum("bqhd,bkhd->bhqk", qn, k_nope)
              + jnp.einsum("bqhr,bkr->bhqk", qr, k_rope))
        p = jax.nn.softmax(sc.astype(jnp.float32) * scale, axis=-1).astype(v.dtype)
        return jnp.einsum("bhqk,bkhd->bqhd", p, v)

    o = lax.map(attend, (qn_blocks, qr_blocks))
    o = o.transpose(1, 0, 2, 3, 4).reshape(bsz, s_len, MLA_HEADS * V_HEAD_DIM)
    return o @ w_o_attn


def ssd_scan(x, dt_raw, a_log, dt_bias, b_mat, c_mat):
    bsz, s_len = x.shape[:2]
    nc, q_len = s_len // SSD_CHUNK, SSD_CHUNK
    g, r = SSD_GROUPS, SSD_HEADS // SSD_GROUPS
    p, n = SSD_HEAD_DIM, SSD_STATE
    dt = jax.nn.softplus(dt_raw.astype(jnp.float32) + dt_bias.astype(jnp.float32))
    a = (-jnp.exp(a_log.astype(jnp.float32)) * dt).reshape(bsz, nc, q_len, g, r)
    xd = (x.astype(jnp.float32) * dt[..., None]).reshape(bsz, nc, q_len, g, r, p)
    bm = b_mat.astype(jnp.float32).reshape(bsz, nc, q_len, g, n)
    cm = c_mat.astype(jnp.float32).reshape(bsz, nc, q_len, g, n)
    a_cs = jnp.cumsum(a, axis=2)
    mask = jnp.tril(jnp.ones((q_len, q_len), dtype=bool))[:, :, None, None]
    seg = a_cs[:, :, :, None] - a_cs[:, :, None]
    decay = jnp.exp(jnp.where(mask, seg, -jnp.inf))
    cb = jnp.einsum("bclgn,bcsgn->bclsg", cm, bm)
    y_diag = jnp.einsum("bclsgr,bcsgrp->bclgrp", cb[..., None] * decay, xd)
    decay_to_end = jnp.exp(a_cs[:, :, -1:] - a_cs)
    states = jnp.einsum("bclgn,bclgrp->bcgrpn", bm, xd * decay_to_end[..., None])
    chunk_decay = jnp.exp(a_cs[:, :, -1])

    def step(h, inp):
        st, dec = inp
        return h * dec[..., None, None] + st, h

    h0 = jnp.zeros((bsz, g, r, p, n), jnp.float32)
    _, prev = lax.scan(step, h0, (states.transpose(1, 0, 2, 3, 4, 5), chunk_decay.transpose(1, 0, 2, 3)))
    prev = prev.transpose(1, 0, 2, 3, 4, 5)
    y_off = jnp.einsum("bclgn,bcgrpn->bclgrp", cm, prev) * jnp.exp(a_cs)[..., None]
    return (y_diag + y_off).reshape(bsz, s_len, SSD_HEADS, p)


def ssd_branch(z, xbc, dt_f, dt_b, conv_w, conv_b, a_log_f, a_log_b, dt_bias_f, dt_bias_b,
               d_skip, norm_w, w_o_ssd):
    bsz, s_len, _ = xbc.shape
    xbc = jax.nn.silu(centred_dwconv(xbc, conv_w, conv_b))
    xs, bm, cm = jnp.split(xbc, [SSD_D_INNER, SSD_D_INNER + SSD_GROUPS * SSD_STATE], axis=-1)
    xs = xs.reshape(bsz, s_len, SSD_HEADS, SSD_HEAD_DIM)
    bm = bm.reshape(bsz, s_len, SSD_GROUPS, SSD_STATE)
    cm = cm.reshape(bsz, s_len, SSD_GROUPS, SSD_STATE)
    flip = lambda t: jnp.flip(t, axis=1)
    y_f = ssd_scan(xs, dt_f, a_log_f, dt_bias_f, bm, cm)
    y_b = flip(ssd_scan(flip(xs), flip(dt_b), a_log_b, dt_bias_b, flip(bm), flip(cm)))
    y = (y_f + y_b + d_skip.astype(jnp.float32)[:, None] * xs.astype(jnp.float32)).astype(xs.dtype)
    y = y.reshape(bsz, s_len, SSD_D_INNER)
    y = rms_norm(y * jax.nn.silu(z), norm_w)
    return y @ w_o_ssd


def conv_ffn(h, w_up, conv_w, conv_b, w_down):
    u = centred_dwconv(h @ w_up, conv_w, conv_b)
    gate, val = jnp.split(u, 2, axis=-1)
    return (jax.nn.silu(gate) * val) @ w_down


def setup_inputs(seed: int = 0) -> dict:
    key = jax.random.key(seed)
    ks = jax.random.split(key, 32)
    L, D = DEPTH, D_MODEL

    def nrm(k, shape, scale):
        return jax.random.normal(k, shape, jnp.float32) * scale

    def gain(k, shape):
        return 1.0 + 0.02 * jax.random.normal(k, shape, jnp.float32)

    def dt_bias(k):
        dt = jnp.exp(jax.random.uniform(k, (L, SSD_HEADS), jnp.float32, np.log(1e-3), np.log(1e-1)))
        return dt + jnp.log(-jnp.expm1(-dt))

    def a_log(k):
        return jnp.log(jax.random.uniform(k, (L, SSD_HEADS), jnp.float32, 1.0, 16.0))

    offsets = jax.random.randint(ks[1], (BATCH, 1), 0, 1024, dtype=jnp.int32)
    positions = jnp.arange(SEQ, dtype=jnp.int32)[None, :] + offsets
    return {
        "x": nrm(ks[0], (BATCH, SEQ, D), 1.0),
        "positions": positions,
        "norm_mix_w": gain(ks[2], (L, D)),
        "w_in": nrm(ks[3], (L, D, IN_DIM), D ** -0.5),
        "q_norm_w": gain(ks[4], (L, Q_LORA_RANK)),
        "w_uq": nrm(ks[5], (L, Q_LORA_RANK, MLA_HEADS * (QK_NOPE_DIM + QK_ROPE_DIM)), Q_LORA_RANK ** -0.5),
        "kv_norm_w": gain(ks[6], (L, KV_LORA_RANK)),
        "w_ukv": nrm(ks[7], (L, KV_LORA_RANK, MLA_HEADS * (QK_NOPE_DIM + V_HEAD_DIM)), KV_LORA_RANK ** -0.5),
        "w_o_attn": nrm(ks[8], (L, MLA_HEADS * V_HEAD_DIM, D), (MLA_HEADS * V_HEAD_DIM) ** -0.5),
        "ssd_conv_w": nrm(ks[9], (L, SSD_CONV, SSD_CONV_DIM), SSD_CONV ** -0.5),
        "ssd_conv_b": nrm(ks[10], (L, SSD_CONV_DIM), 0.02),
        "a_log_fwd": a_log(ks[11]),
        "a_log_bwd": a_log(ks[12]),
        "dt_bias_fwd": dt_bias(ks[13]),
        "dt_bias_bwd": dt_bias(ks[14]),
        "ssd_d": gain(ks[15], (L, SSD_HEADS)),
        "ssd_norm_w": gain(ks[16], (L, SSD_D_INNER)),
        "w_o_ssd": nrm(ks[17], (L, SSD_D_INNER, D), SSD_D_INNER ** -0.5),
        "w_out": nrm(ks[18], (L, D, D), D ** -0.5),
        "norm_ffn_w": gain(ks[19], (L, D)),
        "ffn_w_up": nrm(ks[20], (L, D, 2 * FFN_DIM), D ** -0.5),
        "ffn_conv_w": nrm(ks[21], (L, FFN_CONV, 2 * FFN_DIM), FFN_CONV ** -0.5),
        "ffn_conv_b": nrm(ks[22], (L, 2 * FFN_DIM), 0.02),
        "ffn_w_down": nrm(ks[23], (L, FFN_DIM, D), FFN_DIM ** -0.5),
        "norm_final_w": gain(ks[24], (D,)),
    }


def reference(x, positions, norm_mix_w, w_in, q_norm_w, w_uq, kv_norm_w, w_ukv, w_o_attn,
              ssd_conv_w, ssd_conv_b, a_log_fwd, a_log_bwd, dt_bias_fwd, dt_bias_bwd, ssd_d,
              ssd_norm_w, w_o_ssd, w_out, norm_ffn_w, ffn_w_up, ffn_conv_w, ffn_conv_b,
              ffn_w_down, norm_final_w):
    cos, sin = rope_tables(positions)
    h = x
    for l in range(DEPTH):
        n = rms_norm(h, norm_mix_w[l])
        u = n @ w_in[l]
        q_lat, kv_lat, k_rope, z, xbc, dt_f, dt_b, g_attn, g_ssd = jnp.split(u, IN_SPLITS, axis=-1)
        attn_out = mla_branch(q_lat, kv_lat, k_rope, cos, sin, q_norm_w[l], w_uq[l],
                              kv_norm_w[l], w_ukv[l], w_o_attn[l])
        ssd_out = ssd_branch(z, xbc, dt_f, dt_b, ssd_conv_w[l], ssd_conv_b[l], a_log_fwd[l],
                             a_log_bwd[l], dt_bias_fwd[l], dt_bias_bwd[l], ssd_d[l],
                             ssd_norm_w[l], w_o_ssd[l])
        mixed = jax.nn.sigmoid(g_attn) * attn_out + jax.nn.sigmoid(g_ssd) * ssd_out
        h = h + mixed @ w_out[l]
        h = h + conv_ffn(rms_norm(h, norm_ffn_w[l]), ffn_w_up[l], ffn_conv_w[l], ffn_conv_b[l],
                         ffn_w_down[l])
    return rms_norm(h, norm_final_w)
```

```python
import functools
import math

import jax
import jax.numpy as jnp
import numpy as np
from jax import lax
from jax.experimental import pallas as pl
from jax.experimental.pallas import tpu as pltpu

D_MODEL = 2048
MLA_HEADS = 16
Q_LORA_RANK = 512
KV_LORA_RANK = 512
QK_NOPE_DIM = 128
QK_ROPE_DIM = 64
V_HEAD_DIM = 128
ROPE_THETA = 10000.0
SSD_D_INNER = 2 * D_MODEL
SSD_HEAD_DIM = 64
SSD_HEADS = SSD_D_INNER // SSD_HEAD_DIM
SSD_GROUPS = 8
SSD_HEADS_PER_GROUP = SSD_HEADS // SSD_GROUPS
SSD_STATE = 128
SSD_CONV = 5
SSD_CHUNK = 128
SSD_CONV_DIM = SSD_D_INNER + 2 * SSD_GROUPS * SSD_STATE
SSD_GROUP_WIDTH = SSD_HEADS_PER_GROUP * SSD_HEAD_DIM
FFN_DIM = 5632
FFN_CONV = 3
EPS = 1e-6

V7X_LANES = 128
V7X_BF16_SUBLANES = 16
V7X_VMEM_BYTES = 64 * 1024 * 1024
VMEM_LIMIT_CAP = V7X_VMEM_BYTES - 8 * 1024 * 1024

F32 = jnp.float32
BF16 = jnp.bfloat16

U_Z = 0
U_XBC = U_Z + SSD_D_INNER
U_GA = U_XBC + SSD_CONV_DIM
U_GS = U_GA + D_MODEL
U_QL = U_GS + D_MODEL
U_KVL = U_QL + Q_LORA_RANK
U_WIDTH = U_KVL + KV_LORA_RANK


def _cparams(semantics, vmem_estimate_bytes):
    limit = int(min(max(vmem_estimate_bytes * 5 // 4, 16 * 1024 * 1024), VMEM_LIMIT_CAP))
    return pltpu.CompilerParams(dimension_semantics=semantics, vmem_limit_bytes=limit)


def _rms_scale(x):
    return lax.rsqrt(jnp.mean(x * x, axis=-1, keepdims=True) + EPS)


def _silu(x):
    return x * (1.0 / (1.0 + jnp.exp(-x)))


def _sigmoid(x):
    return 1.0 / (1.0 + jnp.exp(-x))


def _norm_matmul_kernel(x_ref, g_ref, w_ref, o_ref, xn_ref):
    @pl.when(pl.program_id(1) == 0)
    def _():
        x = x_ref[...].astype(F32)
        xn_ref[...] = (x * _rms_scale(x) * g_ref[...]).astype(xn_ref.dtype)

    o_ref[...] = jnp.dot(xn_ref[...], w_ref[...], preferred_element_type=F32).astype(o_ref.dtype)


def _norm_matmul(x, gain, w, out_dtype, tm, tn):
    t, k = x.shape
    n = w.shape[1]
    tm, tn = min(tm, t), min(tn, n)
    est = 2 * tm * k * x.dtype.itemsize + tm * k * 2 + 2 * k * tn * 2 + 2 * tm * tn * 4
    return pl.pallas_call(
        _norm_matmul_kernel,
        grid=(t // tm, n // tn),
        in_specs=[
            pl.BlockSpec((tm, k), lambda i, j: (i, 0)),
            pl.BlockSpec((1, k), lambda i, j: (0, 0)),
            pl.BlockSpec((k, tn), lambda i, j: (0, j)),
        ],
        out_specs=pl.BlockSpec((tm, tn), lambda i, j: (i, j)),
        out_shape=jax.ShapeDtypeStruct((t, n), out_dtype),
        scratch_shapes=[pltpu.VMEM((tm, k), BF16)],
        compiler_params=_cparams(("parallel", "arbitrary"), est),
        name="norm_matmul",
    )(x, gain.reshape(1, k).astype(F32), w)


def _rope_half(y2, cos_t, sin_t):
    return y2 * cos_t + pltpu.roll(y2, QK_ROPE_DIM, 1) * sin_t


def _q_proj_kernel(ql_ref, g_ref, w_ref, cos_ref, sin_ref, o_ref, *, scale):
    x = ql_ref[...].astype(F32)
    xn = (x * _rms_scale(x) * g_ref[...]).astype(BF16)
    cos_t = cos_ref[...] * scale
    sin_t = sin_ref[...] * scale
    for h in range(MLA_HEADS):
        y = jnp.dot(xn, w_ref[h], preferred_element_type=F32)
        o_ref[h, :, :QK_NOPE_DIM] = (y[:, :QK_NOPE_DIM] * scale).astype(o_ref.dtype)
        o_ref[h, :, QK_NOPE_DIM:] = _rope_half(y[:, QK_NOPE_DIM:], cos_t, sin_t).astype(o_ref.dtype)


def _kv_proj_kernel(kvl_ref, g_ref, w_ref, kr_ref, cos_ref, sin_ref, k_ref, vt_ref):
    x = kvl_ref[...].astype(F32)
    xn = (x * _rms_scale(x) * g_ref[...]).astype(BF16)
    roped = _rope_half(kr_ref[...], cos_ref[...], sin_ref[...]).astype(k_ref.dtype)
    for h in range(MLA_HEADS):
        y = jnp.dot(xn, w_ref[h], preferred_element_type=F32)
        k_ref[h, :, :QK_NOPE_DIM] = y[:, :QK_NOPE_DIM].astype(k_ref.dtype)
        k_ref[h, :, QK_NOPE_DIM:] = roped
        vt_ref[h] = y[:, QK_NOPE_DIM:].T.astype(vt_ref.dtype)


def _mla_projections(u3, misc3, cos_t, sin_t, q_gain, wq, kv_gain, wkv, tm):
    b, s, _ = u3.shape
    tm = min(tm, s)
    grid = (b, s // tm)
    head_w = 2 * V7X_LANES
    scale = (QK_NOPE_DIM + QK_ROPE_DIM) ** -0.5 * math.log2(math.e)
    table = pl.BlockSpec((None, tm, V7X_LANES), lambda bi, i: (bi, i, 0))
    gain = pl.BlockSpec((1, Q_LORA_RANK), lambda bi, i: (0, 0))
    wspec = pl.BlockSpec((MLA_HEADS, Q_LORA_RANK, head_w), lambda bi, i: (0, 0, 0))
    est = (2 * tm * Q_LORA_RANK * 2 + 2 * MLA_HEADS * Q_LORA_RANK * head_w * 2
           + 2 * MLA_HEADS * tm * (head_w + V_HEAD_DIM) * 2 + 8 * tm * V7X_LANES * 4)
    params = _cparams(("parallel", "parallel"), est)
    q = pl.pallas_call(
        functools.partial(_q_proj_kernel, scale=scale),
        grid=grid,
        in_specs=[
            pl.BlockSpec((None, tm, Q_LORA_RANK), lambda bi, i: (bi, i, U_QL // Q_LORA_RANK)),
            gain, wspec, table, table,
        ],
        out_specs=pl.BlockSpec((None, MLA_HEADS, tm, head_w), lambda bi, i: (bi, 0, i, 0)),
        out_shape=jax.ShapeDtypeStruct((b, MLA_HEADS, s, head_w), BF16),
        compiler_params=params,
        name="mla_q_proj",
    )(u3, q_gain.reshape(1, -1).astype(F32), wq, cos_t, sin_t)
    k, vt = pl.pallas_call(
        _kv_proj_kernel,
        grid=grid,
        in_specs=[
            pl.BlockSpec((None, tm, KV_LORA_RANK), lambda bi, i: (bi, i, U_KVL // KV_LORA_RANK)),
            gain, wspec,
            pl.BlockSpec((None, tm, V7X_LANES), lambda bi, i: (bi, i, 0)),
            table, table,
        ],
        out_specs=[
            pl.BlockSpec((None, MLA_HEADS, tm, head_w), lambda bi, i: (bi, 0, i, 0)),
            pl.BlockSpec((None, MLA_HEADS, V_HEAD_DIM, tm), lambda bi, i: (bi, 0, 0, i)),
        ],
        out_shape=[
            jax.ShapeDtypeStruct((b, MLA_HEADS, s, head_w), BF16),
            jax.ShapeDtypeStruct((b, MLA_HEADS, V_HEAD_DIM, s), BF16),
        ],
        compiler_params=params,
        name="mla_kv_proj",
    )(u3, kv_gain.reshape(1, -1).astype(F32), wkv, misc3, cos_t, sin_t)
    return q, k, vt


def _attn_kernel(q_ref, k_ref, vt_ref, o_ref, *, tk):
    q = q_ref[...]
    tq = q.shape[0]
    n_k = k_ref.shape[0] // tk

    def body(j, carry):
        m, l, acc = carry
        k0 = pl.multiple_of(j * tk, tk)
        st = lax.dot_general(k_ref[pl.ds(k0, tk), :], q, (((1,), (1,)), ((), ())),
                             preferred_element_type=F32)
        m_new = jnp.maximum(m, jnp.max(st, axis=0, keepdims=True))
        alpha = jnp.exp2(m - m_new)
        p = jnp.exp2(st - m_new)
        l = alpha * l + jnp.sum(p, axis=0, keepdims=True)
        acc = alpha * acc + jnp.dot(vt_ref[:, pl.ds(k0, tk)], p.astype(BF16),
                                    preferred_element_type=F32)
        return m_new, l, acc

    m0 = jnp.full((1, tq), -jnp.inf, F32)
    l0 = jnp.zeros((1, tq), F32)
    acc0 = jnp.zeros((V_HEAD_DIM, tq), F32)
    _, l, acc = lax.fori_loop(0, n_k, body, (m0, l0, acc0))
    o_ref[...] = (acc * (1.0 / l)).T.astype(o_ref.dtype)


def _attention(q, k, vt, tq, tk):
    b, h, s, dk = q.shape
    tq, tk = min(tq, s), min(tk, s)
    est = 2 * tq * dk * 2 + 2 * s * dk * 2 + 2 * V_HEAD_DIM * s * 2 + 2 * tq * V_HEAD_DIM * 2 + 6 * tk * tq * 4
    return pl.pallas_call(
        functools.partial(_attn_kernel, tk=tk),
        grid=(b, h, s // tq),
        in_specs=[
            pl.BlockSpec((None, None, tq, dk), lambda bi, hi, i: (bi, hi, i, 0)),
            pl.BlockSpec((None, None, s, dk), lambda bi, hi, i: (bi, hi, 0, 0)),
            pl.BlockSpec((None, None, V_HEAD_DIM, s), lambda bi, hi, i: (bi, hi, 0, 0)),
        ],
        out_specs=pl.BlockSpec((None, tq, V_HEAD_DIM), lambda bi, hi, i: (bi, i, hi)),
        out_shape=jax.ShapeDtypeStruct((b, s, h * V_HEAD_DIM), BF16),
        compiler_params=_cparams(("parallel", "parallel", "arbitrary"), est),
        name="mla_attention",
    )(q, k, vt)


HALO = V7X_BF16_SUBLANES


def _with_halo(prev, main, nxt):
    i, n = pl.program_id(1), pl.num_programs(1)
    prev = jnp.where(i > 0, prev, jnp.zeros_like(prev))
    nxt = jnp.where(i < n - 1, nxt, jnp.zeros_like(nxt))
    return jnp.concatenate([prev, main, nxt], axis=0)


def _conv_rows(ext, w, bias, rows):
    taps = w.shape[0]
    acc = bias
    for kk in range(taps):
        off = HALO + kk - (taps - 1) // 2
        acc = acc + ext[off:off + rows] * w[kk:kk + 1]
    return acc


def _halo_specs(s, tm, width, col_block):
    per = tm // HALO
    last = s // HALO - 1

    def prev_map(bi, i, *rest):
        return (bi, jnp.maximum(i * per - 1, 0), col_block(*rest))

    def main_map(bi, i, *rest):
        return (bi, i, col_block(*rest))

    def next_map(bi, i, *rest):
        return (bi, jnp.minimum((i + 1) * per, last), col_block(*rest))

    return [
        pl.BlockSpec((None, HALO, width), prev_map),
        pl.BlockSpec((None, tm, width), main_map),
        pl.BlockSpec((None, HALO, width), next_map),
    ]


def _conv_silu_kernel(prev_ref, main_ref, next_ref, w_ref, b_ref, o_ref):
    ext = _with_halo(prev_ref[...], main_ref[...], next_ref[...]).astype(F32)
    o_ref[...] = _silu(_conv_rows(ext, w_ref[...], b_ref[...], o_ref.shape[0])).astype(o_ref.dtype)


def _ssd_conv(u3, conv_w, conv_b, tm, tc):
    b, s, _ = u3.shape
    tm = min(tm, s)
    n_c = SSD_CONV_DIM // tc
    first = U_XBC // tc
    est = 2 * (tm + 2 * HALO) * tc * 2 + 2 * tm * tc * 2 + 8 * (tm + 2 * HALO) * tc * 4
    return pl.pallas_call(
        _conv_silu_kernel,
        grid=(b, s // tm, n_c),
        in_specs=_halo_specs(s, tm, tc, lambda c: first + c) + [
            pl.BlockSpec((SSD_CONV, tc), lambda bi, i, c: (0, c)),
            pl.BlockSpec((1, tc), lambda bi, i, c: (0, c)),
        ],
        out_specs=pl.BlockSpec((None, tm, tc), lambda bi, i, c: (bi, i, c)),
        out_shape=jax.ShapeDtypeStruct((b, s, SSD_CONV_DIM), BF16),
        compiler_params=_cparams(("parallel", "parallel", "parallel"), est),
        name="ssd_conv_silu",
    )(u3, u3, u3, conv_w.astype(F32), conv_b.reshape(1, -1).astype(F32))


def _softplus(x):
    return jnp.maximum(x, 0.0) + jnp.log1p(jnp.exp(-jnp.abs(x)))


def _chunk_cumsums(a, axis, pos_in_chunk, is_fwd):
    n = a.shape[axis]
    cf, cb = a, a
    step = 1
    while step < SSD_CHUNK:
        cf = cf + jnp.where(pos_in_chunk >= step, pltpu.roll(cf, step, axis), 0.0)
        cb = cb + jnp.where(pos_in_chunk < SSD_CHUNK - step, pltpu.roll(cb, n - step, axis), 0.0)
        step *= 2
    return jnp.where(is_fwd, cf, cb)


def _dt_prep_kernel(dtc_ref, dtr_ref, alog_c_ref, bias_c_ref, alog_r_ref, bias_r_ref,
                    acs_c_ref, ef_ref, wst_ref, cd_ref, dt_r_ref, acs_r_ref):
    s = dtc_ref.shape[0]
    n_chunks = s // SSD_CHUNK
    dt = _softplus(dtc_ref[...] + bias_c_ref[...])
    a = -jnp.exp(alog_c_ref[...]) * dt
    pos = lax.broadcasted_iota(jnp.int32, a.shape, 0) % SSD_CHUNK
    fwd = lax.broadcasted_iota(jnp.int32, a.shape, 1) < SSD_HEADS
    acs = _chunk_cumsums(a, 0, pos, fwd)
    acs3 = acs.reshape(n_chunks, SSD_CHUNK, V7X_LANES)
    fwd3 = lax.broadcasted_iota(jnp.int32, (n_chunks, 1, V7X_LANES), 2) < SSD_HEADS
    end = jnp.where(fwd3, acs3[:, SSD_CHUNK - 1:], acs3[:, :1])
    acs_c_ref[...] = acs
    ef_ref[...] = jnp.exp(acs)
    wst_ref[...] = (dt.reshape(acs3.shape) * jnp.exp(end - acs3)).reshape(acs.shape)
    cd_ref[...] = jnp.exp(end)
    dt_r = _softplus(dtr_ref[...] + bias_r_ref[...])
    a_r = -jnp.exp(alog_r_ref[...]) * dt_r
    pos_r = lax.broadcasted_iota(jnp.int32, a_r.shape, 1) % SSD_CHUNK
    fwd_r = lax.broadcasted_iota(jnp.int32, a_r.shape, 0) < SSD_HEADS
    dt_r_ref[...] = dt_r
    acs_r_ref[...] = _chunk_cumsums(a_r, 1, pos_r, fwd_r)


def _dt_prep(misc3, dt_rows, a_log, dt_bias):
    b, s, _ = misc3.shape
    n_chunks = s // SSD_CHUNK
    col = pl.BlockSpec((None, s, V7X_LANES), lambda bi: (bi, 0, 0))
    row = pl.BlockSpec((None, V7X_LANES, s), lambda bi: (bi, 0, 0))
    cvec = pl.BlockSpec((1, V7X_LANES), lambda bi: (0, 0))
    rvec = pl.BlockSpec((V7X_LANES, 1), lambda bi: (0, 0))
    est = 2 * 8 * s * V7X_LANES * 4 + 24 * s * V7X_LANES * 4
    return pl.pallas_call(
        _dt_prep_kernel,
        grid=(b,),
        in_specs=[pl.BlockSpec((None, s, V7X_LANES), lambda bi: (bi, 0, 1)), row, cvec, cvec, rvec, rvec],
        out_specs=[col, col, col, pl.BlockSpec((None, n_chunks, 1, V7X_LANES), lambda bi: (bi, 0, 0, 0)), row, row],
        out_shape=[jax.ShapeDtypeStruct((b, s, V7X_LANES), F32)] * 3
        + [jax.ShapeDtypeStruct((b, n_chunks, 1, V7X_LANES), F32)]
        + [jax.ShapeDtypeStruct((b, V7X_LANES, s), F32)] * 2,
        compiler_params=_cparams(("parallel",), est),
        name="ssd_dt_prep",
    )(misc3, dt_rows, a_log.reshape(1, -1), dt_bias.reshape(1, -1), a_log.reshape(-1, 1), dt_bias.reshape(-1, 1))


R = SSD_HEADS_PER_GROUP
COL_ACS_F, COL_ACS_B, COL_EF_F, COL_EF_B, COL_WST_F, COL_WST_B = (R * n for n in range(6))
ROW_DT_F, ROW_DT_B, ROW_ACS_F, ROW_ACS_B = (R * n for n in range(4))


def _expand_heads(cols, first):
    q = cols.shape[0]
    low = lax.broadcasted_iota(jnp.int32, (q, V7X_LANES), 1) < SSD_HEAD_DIM
    parts = []
    for pair in range(R // 2):
        c0 = first + 2 * pair
        parts.append(jnp.where(low,
                               jnp.broadcast_to(cols[:, c0:c0 + 1], (q, V7X_LANES)),
                               jnp.broadcast_to(cols[:, c0 + 1:c0 + 2], (q, V7X_LANES))))
    return jnp.concatenate(parts, axis=1)


def _ssd_kernel(xs_ref, b_ref, c_ref, colq_ref, rowq_ref, cd_ref, dskip_ref, y_ref, hb_ref, state_ref):
    q = SSD_CHUNK
    n_chunks = xs_ref.shape[0] // q

    def transposed_b(r0):
        return b_ref[pl.ds(r0, q), :].astype(F32).T.astype(BF16)

    state_ref[...] = jnp.zeros_like(state_ref)

    def bwd_body(i, carry):
        c = n_chunks - 1 - i
        r0 = pl.multiple_of(c * q, q)
        hb_ref[c] = state_ref[...].astype(hb_ref.dtype)
        wst = _expand_heads(colq_ref[pl.ds(r0, q), :], COL_WST_B)
        xdw = (xs_ref[pl.ds(r0, q), :].astype(F32) * wst).astype(BF16)
        contrib = jnp.dot(transposed_b(r0), xdw, preferred_element_type=F32)
        state_ref[...] = state_ref[...] * cd_ref[pl.ds(n_chunks + c, 1), :] + contrib
        return carry

    lax.fori_loop(0, n_chunks, bwd_body, 0)

    state_ref[...] = jnp.zeros_like(state_ref)
    li = lax.broadcasted_iota(jnp.int32, (q, q), 0)
    si = lax.broadcasted_iota(jnp.int32, (q, q), 1)
    lower, upper = li >= si, li <= si
    lane_low = lax.broadcasted_iota(jnp.int32, (q, V7X_LANES), 1) < SSD_HEAD_DIM

    def fwd_body(c, carry):
        r0 = pl.multiple_of(c * q, q)
        x_bf = xs_ref[pl.ds(r0, q), :]
        x = x_bf.astype(F32)
        bm = b_ref[pl.ds(r0, q), :]
        cm = c_ref[pl.ds(r0, q), :]
        cols = colq_ref[pl.ds(r0, q), :]
        rows = rowq_ref[:, pl.ds(r0, q)]
        cb = lax.dot_general(cm, bm, (((1,), (1,)), ((), ())), preferred_element_type=F32)

        def head_matrix(r):
            seg_f = cols[:, COL_ACS_F + r:COL_ACS_F + r + 1] - rows[ROW_ACS_F + r:ROW_ACS_F + r + 1, :]
            seg_b = cols[:, COL_ACS_B + r:COL_ACS_B + r + 1] - rows[ROW_ACS_B + r:ROW_ACS_B + r + 1, :]
            lf = jnp.where(lower, jnp.exp(seg_f), 0.0) * rows[ROW_DT_F + r:ROW_DT_F + r + 1, :]
            lb = jnp.where(upper, jnp.exp(seg_b), 0.0) * rows[ROW_DT_B + r:ROW_DT_B + r + 1, :]
            return (cb * (lf + lb)).astype(BF16)

        diag = []
        for pair in range(R // 2):
            xp = x_bf[:, pair * V7X_LANES:(pair + 1) * V7X_LANES]
            zero = jnp.zeros_like(xp)
            rhs = jnp.concatenate([jnp.where(lane_low, xp, zero), jnp.where(lane_low, zero, xp)], axis=0)
            lhs = jnp.concatenate([head_matrix(2 * pair), head_matrix(2 * pair + 1)], axis=1)
            diag.append(jnp.dot(lhs, rhs, preferred_element_type=F32))
        y = jnp.concatenate(diag, axis=1)

        h_prev = state_ref[...]
        y = y + jnp.dot(cm, h_prev.astype(BF16), preferred_element_type=F32) * _expand_heads(cols, COL_EF_F)
        y = y + jnp.dot(cm, hb_ref[c], preferred_element_type=F32) * _expand_heads(cols, COL_EF_B)
        y = y + x * dskip_ref[...]
        y_ref[pl.ds(r0, q), :] = y.astype(y_ref.dtype)

        xdw = (x * _expand_heads(cols, COL_WST_F)).astype(BF16)
        contrib = jnp.dot(transposed_b(r0), xdw, preferred_element_type=F32)
        state_ref[...] = h_prev * cd_ref[pl.ds(c, 1), :] + contrib
        return carry

    lax.fori_loop(0, n_chunks, fwd_body, 0)


def _ssd_scan(xbc_c, colq, rowq, cd, dskip):
    b, s, _ = xbc_c.shape
    n_chunks = s // SSD_CHUNK
    gw = SSD_GROUP_WIDTH
    b_first = SSD_D_INNER // SSD_STATE
    c_first = b_first + SSD_GROUPS
    n_col, n_row = colq.shape[-1], rowq.shape[-2]
    est = (2 * s * gw * 2 * 2 + 4 * s * SSD_STATE * 2 + 2 * s * V7X_LANES * 4 + 2 * n_row * s * 4
           + n_chunks * SSD_STATE * gw * 2 + 4 * n_chunks * gw * 4 + 64 * SSD_CHUNK * gw * 4)
    return pl.pallas_call(
        _ssd_kernel,
        grid=(b, SSD_GROUPS),
        in_specs=[
            pl.BlockSpec((None, s, gw), lambda bi, g: (bi, 0, g)),
            pl.BlockSpec((None, s, SSD_STATE), lambda bi, g: (bi, 0, b_first + g)),
            pl.BlockSpec((None, s, SSD_STATE), lambda bi, g: (bi, 0, c_first + g)),
            pl.BlockSpec((None, None, s, n_col), lambda bi, g: (bi, g, 0, 0)),
            pl.BlockSpec((None, None, n_row, s), lambda bi, g: (bi, g, 0, 0)),
            pl.BlockSpec((None, None, 2 * n_chunks, gw), lambda bi, g: (bi, g, 0, 0)),
            pl.BlockSpec((None, 1, gw), lambda bi, g: (g, 0, 0)),
        ],
        out_specs=pl.BlockSpec((None, s, gw), lambda bi, g: (bi, 0, g)),
        out_shape=jax.ShapeDtypeStruct((b, s, SSD_D_INNER), BF16),
        scratch_shapes=[
            pltpu.VMEM((n_chunks, SSD_STATE, gw), BF16),
            pltpu.VMEM((SSD_STATE, gw), F32),
        ],
        compiler_params=_cparams(("parallel", "parallel"), est),
        name="ssd_scan",
    )(xbc_c, xbc_c, xbc_c, colq, rowq, cd, dskip)


def _mix_kernel(o_ref, y_ref, z_ref, nw_ref, wa_ref, ws_ref, ga_ref, gs_ref, m_ref, yn_ref):
    @pl.when(pl.program_id(1) == 0)
    def _():
        yz = y_ref[...].astype(F32) * _silu(z_ref[...].astype(F32))
        yn_ref[...] = (yz * _rms_scale(yz) * nw_ref[...]).astype(yn_ref.dtype)

    attn = jnp.dot(o_ref[...], wa_ref[...], preferred_element_type=F32)
    ssd = jnp.dot(yn_ref[...], ws_ref[...], preferred_element_type=F32)
    mixed = _sigmoid(ga_ref[...].astype(F32)) * attn + _sigmoid(gs_ref[...].astype(F32)) * ssd
    m_ref[...] = mixed.astype(m_ref.dtype)


def _mix(o2, y2, u2, norm_w, wa, ws, tm, tn):
    t = o2.shape[0]
    tm = min(tm, t)
    d_attn, d_ssd = o2.shape[1], y2.shape[1]
    est = (2 * tm * d_attn * 2 + 4 * tm * d_ssd * 2 + tm * d_ssd * 2 + 2 * (d_attn + d_ssd) * tn * 2
           + 6 * tm * tn * 2 + 4 * tm * tn * 4)
    return pl.pallas_call(
        _mix_kernel,
        grid=(t // tm, D_MODEL // tn),
        in_specs=[
            pl.BlockSpec((tm, d_attn), lambda i, j: (i, 0)),
            pl.BlockSpec((tm, d_ssd), lambda i, j: (i, 0)),
            pl.BlockSpec((tm, d_ssd), lambda i, j: (i, U_Z // SSD_D_INNER)),
            pl.BlockSpec((1, d_ssd), lambda i, j: (0, 0)),
            pl.BlockSpec((d_attn, tn), lambda i, j: (0, j)),
            pl.BlockSpec((d_ssd, tn), lambda i, j: (0, j)),
            pl.BlockSpec((tm, tn), lambda i, j: (i, U_GA // tn + j)),
            pl.BlockSpec((tm, tn), lambda i, j: (i, U_GS // tn + j)),
        ],
        out_specs=pl.BlockSpec((tm, tn), lambda i, j: (i, j)),
        out_shape=jax.ShapeDtypeStruct((t, D_MODEL), BF16),
        scratch_shapes=[pltpu.VMEM((tm, d_ssd), BF16)],
        compiler_params=_cparams(("parallel", "arbitrary"), est),
        name="branch_mix",
    )(o2, y2, u2, norm_w.reshape(1, -1).astype(F32), wa, ws, u2, u2)


def _residual_matmul_kernel(a_ref, w_ref, x_ref, o_ref):
    o_ref[...] = x_ref[...] + jnp.dot(a_ref[...], w_ref[...], preferred_element_type=F32)


def _residual_matmul(a, w, x, tm, tn):
    t, k = a.shape
    n = w.shape[1]
    tm, tn = min(tm, t), min(tn, n)
    est = 2 * tm * k * 2 + 2 * k * tn * 2 + 4 * tm * tn * 4
    return pl.pallas_call(
        _residual_matmul_kernel,
        grid=(t // tm, n // tn),
        in_specs=[
            pl.BlockSpec((tm, k), lambda i, j: (i, 0)),
            pl.BlockSpec((k, tn), lambda i, j: (0, j)),
            pl.BlockSpec((tm, tn), lambda i, j: (i, j)),
        ],
        out_specs=pl.BlockSpec((tm, tn), lambda i, j: (i, j)),
        out_shape=jax.ShapeDtypeStruct((t, n), F32),
        compiler_params=_cparams(("parallel", "parallel"), est),
        name="residual_matmul",
    )(a, w, x)


def _ffn_kernel(prev_ref, main_ref, next_ref, g_ref, wg_ref, wv_ref, cwg_ref, cwv_ref, cbg_ref, cbv_ref,
                wd_ref, gf_ref, o_ref, hn_ref):
    j, n_j = pl.program_id(2), pl.num_programs(2)
    tm = main_ref.shape[0]

    @pl.when(j == 0)
    def _():
        ext = _with_halo(prev_ref[...], main_ref[...], next_ref[...])
        hn_ref[...] = (ext * _rms_scale(ext) * g_ref[...]).astype(hn_ref.dtype)

    hn = hn_ref[...]
    gate = _conv_rows(jnp.dot(hn, wg_ref[...], preferred_element_type=F32), cwg_ref[...], cbg_ref[...], tm)
    val = _conv_rows(jnp.dot(hn, wv_ref[...], preferred_element_type=F32), cwv_ref[...], cbv_ref[...], tm)
    act = (_silu(gate) * val).astype(BF16)
    down = jnp.dot(act, wd_ref[...], preferred_element_type=F32)

    @pl.when(j == 0)
    def _():
        o_ref[...] = main_ref[...] + down

    @pl.when(j > 0)
    def _():
        o_ref[...] += down

    @pl.when(j == n_j - 1)
    def _():
        h = o_ref[...]
        o_ref[...] = h * _rms_scale(h) * gf_ref[...]


def _ffn(h3, gain, w_up, conv_w, conv_b, w_down, final_gain, tm, tf):
    b, s, d = h3.shape
    tm = min(tm, s)
    n_f = FFN_DIM // tf
    row = lambda bi, i, j: (0, 0)
    est = (2 * (tm + 2 * HALO) * d * 4 + (tm + 2 * HALO) * d * 2 + 2 * 2 * d * tf * 2 + 2 * tf * d * 2
           + 2 * tm * d * 4 + 8 * (tm + 2 * HALO) * tf * 4)
    return pl.pallas_call(
        _ffn_kernel,
        grid=(b, s // tm, n_f),
        in_specs=_halo_specs(s, tm, d, lambda j: 0) + [
            pl.BlockSpec((1, d), row),
            pl.BlockSpec((d, tf), lambda bi, i, j: (0, j)),
            pl.BlockSpec((d, tf), lambda bi, i, j: (0, n_f + j)),
            pl.BlockSpec((FFN_CONV, tf), lambda bi, i, j: (0, j)),
            pl.BlockSpec((FFN_CONV, tf), lambda bi, i, j: (0, n_f + j)),
            pl.BlockSpec((1, tf), lambda bi, i, j: (0, j)),
            pl.BlockSpec((1, tf), lambda bi, i, j: (0, n_f + j)),
            pl.BlockSpec((tf, d), lambda bi, i, j: (j, 0)),
            pl.BlockSpec((1, d), row),
        ],
        out_specs=pl.BlockSpec((None, tm, d), lambda bi, i, j: (bi, i, 0)),
        out_shape=jax.ShapeDtypeStruct((b, s, d), F32),
        scratch_shapes=[pltpu.VMEM((tm + 2 * HALO, d), BF16)],
        compiler_params=_cparams(("parallel", "parallel", "arbitrary"), est),
        name="conv_ffn",
    )(h3, h3, h3, gain.reshape(1, -1).astype(F32), w_up, w_up, conv_w.astype(F32), conv_w.astype(F32),
      conv_b.reshape(1, -1).astype(F32), conv_b.reshape(1, -1).astype(F32), w_down,
      final_gain.reshape(1, -1).astype(F32))


def _swap_halves(w):
    half = w.shape[-1] // 2
    return jnp.concatenate([w[..., half:], w[..., :half]], axis=-1)


def _group_major_cols(t):
    b, s, _ = t.shape
    return t.reshape(b, s, 2, SSD_GROUPS, R).transpose(0, 3, 1, 2, 4)


def _layer(h, cos_t, sin_t, norm_mix_w, w_in, q_norm_w, w_uq, kv_norm_w, w_ukv, w_o_attn, ssd_conv_w,
           ssd_conv_b, a_log_fwd, a_log_bwd, dt_bias_fwd, dt_bias_bwd, ssd_d, ssd_norm_w, w_o_ssd, w_out):
    b, s, d = h.shape
    t = b * s
    n_chunks = s // SSD_CHUNK
    x2 = h.reshape(t, d)

    splits = np.cumsum([Q_LORA_RANK, KV_LORA_RANK, QK_ROPE_DIM, SSD_D_INNER, SSD_CONV_DIM, SSD_HEADS,
                        SSD_HEADS, D_MODEL])
    w_ql, w_kvl, w_kr, w_z, w_xbc, w_dtf, w_dtb, w_ga, w_gs = jnp.split(w_in, [int(v) for v in splits], axis=1)
    w_main = jnp.concatenate([w_z, w_xbc, w_ga, w_gs, w_ql, w_kvl], axis=1).astype(BF16)
    w_misc = jnp.concatenate([w_kr, _swap_halves(w_kr), w_dtf, w_dtb], axis=1).astype(BF16)
    u2 = _norm_matmul(x2, norm_mix_w, w_main, BF16, tm=1024, tn=512)
    misc2 = _norm_matmul(x2, norm_mix_w, w_misc, F32, tm=1024, tn=2 * V7X_LANES)
    u3 = u2.reshape(b, s, U_WIDTH)
    misc3 = misc2.reshape(b, s, 2 * V7X_LANES)

    wq = w_uq.reshape(Q_LORA_RANK, MLA_HEADS, QK_NOPE_DIM + QK_ROPE_DIM)
    wq_rope = wq[..., QK_NOPE_DIM:]
    wq = jnp.concatenate([wq[..., :QK_NOPE_DIM], wq_rope, _swap_halves(wq_rope)], axis=-1)
    wq = wq.transpose(1, 0, 2).astype(BF16)
    wkv = w_ukv.reshape(KV_LORA_RANK, MLA_HEADS, QK_NOPE_DIM + V_HEAD_DIM).transpose(1, 0, 2).astype(BF16)
    q, k, vt = _mla_projections(u3, misc3, cos_t, sin_t, q_norm_w, wq, kv_norm_w, wkv, tm=512)
    o = _attention(q, k, vt, tq=256, tk=512)

    xbc_c = _ssd_conv(u3, ssd_conv_w, ssd_conv_b, tm=512, tc=512)
    dt_rows = misc3[:, :, V7X_LANES:].transpose(0, 2, 1)
    a_log = jnp.concatenate([a_log_fwd, a_log_bwd]).astype(F32)
    dt_bias = jnp.concatenate([dt_bias_fwd, dt_bias_bwd]).astype(F32)
    acs_c, ef_c, wst_c, cd, dt_r, acs_r = _dt_prep(misc3, dt_rows, a_log, dt_bias)
    colq = jnp.concatenate([_group_major_cols(v) for v in (acs_c, ef_c, wst_c)], axis=3)
    colq = colq.reshape(b, SSD_GROUPS, s, 6 * R)
    rowq = jnp.stack([dt_r, acs_r], axis=1).reshape(b, 2, 2, SSD_GROUPS, R, s)
    rowq = rowq.transpose(0, 3, 1, 2, 4, 5).reshape(b, SSD_GROUPS, 4 * R, s)
    cd = jnp.repeat(cd.reshape(b, n_chunks, 2, SSD_GROUPS, R), SSD_HEAD_DIM, axis=-1)
    cd = cd.transpose(0, 3, 2, 1, 4).reshape(b, SSD_GROUPS, 2 * n_chunks, SSD_GROUP_WIDTH)
    dskip = jnp.repeat(ssd_d.astype(F32), SSD_HEAD_DIM).reshape(SSD_GROUPS, 1, SSD_GROUP_WIDTH)
    y = _ssd_scan(xbc_c, colq, rowq, cd, dskip)

    mixed = _mix(o.reshape(t, -1), y.reshape(t, -1), u2, ssd_norm_w, w_o_attn.astype(BF16),
                 w_o_ssd.astype(BF16), tm=512, tn=512)
    h1 = _residual_matmul(mixed, w_out.astype(BF16), x2, tm=1024, tn=1024)
    return h1.reshape(b, s, d)


def kernel(x, positions, norm_mix_w, w_in, q_norm_w, w_uq, kv_norm_w, w_ukv, w_o_attn, ssd_conv_w, ssd_conv_b,
           a_log_fwd, a_log_bwd, dt_bias_fwd, dt_bias_bwd, ssd_d, ssd_norm_w, w_o_ssd, w_out, norm_ffn_w,
           ffn_w_up, ffn_conv_w, ffn_conv_b, ffn_w_down, norm_final_w):
    depth = w_in.shape[0]
    assert depth == 1, "the conv-gated MLP kernel fuses the final RMSNorm, which is only valid for one layer"
    half = QK_ROPE_DIM // 2
    inv_freq = ROPE_THETA ** (-jnp.arange(half, dtype=F32) / half)
    ang = positions.astype(F32)[..., None] * inv_freq
    cos, sin = jnp.cos(ang), jnp.sin(ang)
    pad = jnp.zeros(cos.shape[:-1] + (V7X_LANES - QK_ROPE_DIM,), F32)
    cos_t = jnp.concatenate([cos, cos, pad], axis=-1)
    sin_t = jnp.concatenate([-sin, sin, pad], axis=-1)

    l = 0
    h1 = _layer(x, cos_t, sin_t, norm_mix_w[l], w_in[l], q_norm_w[l], w_uq[l], kv_norm_w[l], w_ukv[l],
                w_o_attn[l], ssd_conv_w[l], ssd_conv_b[l], a_log_fwd[l], a_log_bwd[l], dt_bias_fwd[l],
                dt_bias_bwd[l], ssd_d[l], ssd_norm_w[l], w_o_ssd[l], w_out[l])
    return _ffn(h1, norm_ffn_w[l], ffn_w_up[l].astype(BF16), ffn_conv_w[l], ffn_conv_b[l],
                ffn_w_down[l].astype(BF16), norm_final_w, tm=512, tf=512)
```

```python
import functools
import math

import jax
import jax.numpy as jnp
import numpy as np
from jax import lax
from jax.experimental import pallas as pl
from jax.experimental.pallas import tpu as pltpu

D_MODEL = 2048
MLA_HEADS = 16
Q_LORA_RANK = 512
KV_LORA_RANK = 512
QK_NOPE_DIM = 128
QK_ROPE_DIM = 64
V_HEAD_DIM = 128
ROPE_THETA = 10000.0
SSD_D_INNER = 2 * D_MODEL
SSD_HEAD_DIM = 64
SSD_HEADS = SSD_D_INNER // SSD_HEAD_DIM
SSD_GROUPS = 8
SSD_HEADS_PER_GROUP = SSD_HEADS // SSD_GROUPS
SSD_STATE = 128
SSD_CONV = 5
SSD_CHUNK = 128
SSD_CONV_DIM = SSD_D_INNER + 2 * SSD_GROUPS * SSD_STATE
SSD_GROUP_WIDTH = SSD_HEADS_PER_GROUP * SSD_HEAD_DIM
FFN_DIM = 5632
FFN_CONV = 3
EPS = 1e-6

V7X_LANES = 128
V7X_BF16_SUBLANES = 16
V7X_VMEM_BYTES = 64 * 1024 * 1024
VMEM_LIMIT_CAP = V7X_VMEM_BYTES - 8 * 1024 * 1024

F32 = jnp.float32
BF16 = jnp.bfloat16

U_Z = 0
U_XBC = U_Z + SSD_D_INNER
U_GA = U_XBC + SSD_CONV_DIM
U_GS = U_GA + D_MODEL
U_QL = U_GS + D_MODEL
U_KVL = U_QL + Q_LORA_RANK
U_WIDTH = U_KVL + KV_LORA_RANK


def _cparams(semantics, vmem_estimate_bytes):
    limit = int(min(max(vmem_estimate_bytes * 5 // 4, 16 * 1024 * 1024), VMEM_LIMIT_CAP))
    return pltpu.CompilerParams(dimension_semantics=semantics, vmem_limit_bytes=limit)


def _rms_scale(x):
    return lax.rsqrt(jnp.mean(x * x, axis=-1, keepdims=True) + EPS)


def _silu(x):
    return x * (1.0 / (1.0 + jnp.exp(-x)))


def _sigmoid(x):
    return 1.0 / (1.0 + jnp.exp(-x))


def _norm_matmul_kernel(x_ref, g_ref, w_ref, o_ref, xn_ref):
    @pl.when(pl.program_id(1) == 0)
    def _():
        x = x_ref[...].astype(F32)
        xn_ref[...] = (x * _rms_scale(x) * g_ref[...]).astype(xn_ref.dtype)

    o_ref[...] = jnp.dot(xn_ref[...], w_ref[...], preferred_element_type=F32).astype(o_ref.dtype)


def _norm_matmul(x, gain, w, out_dtype, tm, tn):
    t, k = x.shape
    n = w.shape[1]
    tm, tn = min(tm, t), min(tn, n)
    est = 2 * tm * k * x.dtype.itemsize + tm * k * 2 + 2 * k * tn * 2 + 2 * tm * tn * 4
    return pl.pallas_call(
        _norm_matmul_kernel,
        grid=(t // tm, n // tn),
        in_specs=[
            pl.BlockSpec((tm, k), lambda i, j: (i, 0)),
            pl.BlockSpec((1, k), lambda i, j: (0, 0)),
            pl.BlockSpec((k, tn), lambda i, j: (0, j)),
        ],
        out_specs=pl.BlockSpec((tm, tn), lambda i, j: (i, j)),
        out_shape=jax.ShapeDtypeStruct((t, n), out_dtype),
        scratch_shapes=[pltpu.VMEM((tm, k), BF16)],
        compiler_params=_cparams(("parallel", "arbitrary"), est),
        name="norm_matmul",
    )(x, gain.reshape(1, k).astype(F32), w)


def _rope_half(y2, cos_t, sin_t):
    return y2 * cos_t + pltpu.roll(y2, QK_ROPE_DIM, 1) * sin_t


def _q_proj_kernel(ql_ref, g_ref, w_ref, cos_ref, sin_ref, o_ref, *, scale):
    x = ql_ref[...].astype(F32)
    xn = (x * _rms_scale(x) * g_ref[...]).astype(BF16)
    cos_t = cos_ref[...] * scale
    sin_t = sin_ref[...] * scale
    for h in range(MLA_HEADS):
        y = jnp.dot(xn, w_ref[h], preferred_element_type=F32)
        o_ref[h, :, :QK_NOPE_DIM] = (y[:, :QK_NOPE_DIM] * scale).astype(o_ref.dtype)
        o_ref[h, :, QK_NOPE_DIM:] = _rope_half(y[:, QK_NOPE_DIM:], cos_t, sin_t).astype(o_ref.dtype)


def _kv_proj_kernel(kvl_ref, g_ref, w_ref, kr_ref, cos_ref, sin_ref, k_ref, vt_ref):
    x = kvl_ref[...].astype(F32)
    xn = (x * _rms_scale(x) * g_ref[...]).astype(BF16)
    roped = _rope_half(kr_ref[...], cos_ref[...], sin_ref[...]).astype(k_ref.dtype)
    for h in range(MLA_HEADS):
        y = jnp.dot(xn, w_ref[h], preferred_element_type=F32)
        k_ref[h, :, :QK_NOPE_DIM] = y[:, :QK_NOPE_DIM].astype(k_ref.dtype)
        k_ref[h, :, QK_NOPE_DIM:] = roped
        vt_ref[h] = y[:, QK_NOPE_DIM:].T.astype(vt_ref.dtype)


def _mla_projections(u3, misc3, cos_t, sin_t, q_gain, wq, kv_gain, wkv, tm):
    b, s, _ = u3.shape
    tm = min(tm, s)
    grid = (b, s // tm)
    head_w = 2 * V7X_LANES
    scale = (QK_NOPE_DIM + QK_ROPE_DIM) ** -0.5 * math.log2(math.e)
    table = pl.BlockSpec((None, tm, V7X_LANES), lambda bi, i: (bi, i, 0))
    gain = pl.BlockSpec((1, Q_LORA_RANK), lambda bi, i: (0, 0))
    wspec = pl.BlockSpec((MLA_HEADS, Q_LORA_RANK, head_w), lambda bi, i: (0, 0, 0))
    est = (2 * tm * Q_LORA_RANK * 2 + 2 * MLA_HEADS * Q_LORA_RANK * head_w * 2
           + 2 * MLA_HEADS * tm * (head_w + V_HEAD_DIM) * 2 + 8 * tm * V7X_LANES * 4)
    params = _cparams(("parallel", "parallel"), est)
    q = pl.pallas_call(
        functools.partial(_q_proj_kernel, scale=scale),
        grid=grid,
        in_specs=[
            pl.BlockSpec((None, tm, Q_LORA_RANK), lambda bi, i: (bi, i, U_QL // Q_LORA_RANK)),
            gain, wspec, table, table,
        ],
        out_specs=pl.BlockSpec((None, MLA_HEADS, tm, head_w), lambda bi, i: (bi, 0, i, 0)),
        out_shape=jax.ShapeDtypeStruct((b, MLA_HEADS, s, head_w), BF16),
        compiler_params=params,
        name="mla_q_proj",
    )(u3, q_gain.reshape(1, -1).astype(F32), wq, cos_t, sin_t)
    k, vt = pl.pallas_call(
        _kv_proj_kernel,
        grid=grid,
        in_specs=[
            pl.BlockSpec((None, tm, KV_LORA_RANK), lambda bi, i: (bi, i, U_KVL // KV_LORA_RANK)),
            gain, wspec,
            pl.BlockSpec((None, tm, V7X_LANES), lambda bi, i: (bi, i, 0)),
            table, table,
        ],
        out_specs=[
            pl.BlockSpec((None, MLA_HEADS, tm, head_w), lambda bi, i: (bi, 0, i, 0)),
            pl.BlockSpec((None, MLA_HEADS, V_HEAD_DIM, tm), lambda bi, i: (bi, 0, 0, i)),
        ],
        out_shape=[
            jax.ShapeDtypeStruct((b, MLA_HEADS, s, head_w), BF16),
            jax.ShapeDtypeStruct((b, MLA_HEADS, V_HEAD_DIM, s), BF16),
        ],
        compiler_params=params,
        name="mla_kv_proj",
    )(u3, kv_gain.reshape(1, -1).astype(F32), wkv, misc3, cos_t, sin_t)
    return q, k, vt


def _attn_kernel(q_ref, k_ref, vt_ref, o_ref, acc_ref, st0_ref, st1_ref, p0_ref, p1_ref, *, tk):
    q = q_ref[...]
    tq = q.shape[0]
    n_k = k_ref.shape[0] // tk
    st_refs, p_refs = (st0_ref, st1_ref), (p0_ref, p1_ref)

    def scores(j):
        st_refs[j % 2][...] = lax.dot_general(k_ref[j * tk:(j + 1) * tk, :], q, (((1,), (1,)), ((), ())),
                                              preferred_element_type=F32)

    m = jnp.full((1, tq), -jnp.inf, F32)
    l = jnp.zeros((1, tq), F32)
    scores(0)
    for j in range(n_k):
        if j + 1 < n_k:
            scores(j + 1)
        st = st_refs[j % 2][...]
        m_new = jnp.maximum(m, jnp.max(st, axis=0, keepdims=True))
        alpha = jnp.exp2(m - m_new)
        p = jnp.exp2(st - m_new)
        l = alpha * l + jnp.sum(p, axis=0, keepdims=True)
        m = m_new
        p_refs[j % 2][...] = p.astype(BF16)
        pv = jnp.dot(vt_ref[:, j * tk:(j + 1) * tk], p_refs[j % 2][...], preferred_element_type=F32)
        if j == 0:
            acc_ref[...] = pv
        else:
            acc_ref[...] = alpha * acc_ref[...] + pv
    o_ref[...] = (acc_ref[...] * (1.0 / l)).T.astype(o_ref.dtype)


def _attention(q, k, vt, tq, tk):
    b, h, s, dk = q.shape
    tq, tk = min(tq, s), min(tk, s)
    est = (2 * tq * dk * 2 + 2 * s * dk * 2 + 2 * V_HEAD_DIM * s * 2 + 2 * tq * V_HEAD_DIM * 2
           + V_HEAD_DIM * tq * 4 + 2 * tk * tq * (4 + 2) + 4 * tk * tq * 4)
    scratch = [pltpu.VMEM((V_HEAD_DIM, tq), F32), pltpu.VMEM((tk, tq), F32), pltpu.VMEM((tk, tq), F32),
               pltpu.VMEM((tk, tq), BF16), pltpu.VMEM((tk, tq), BF16)]
    return pl.pallas_call(
        functools.partial(_attn_kernel, tk=tk),
        grid=(b, h, s // tq),
        in_specs=[
            pl.BlockSpec((None, None, tq, dk), lambda bi, hi, i: (bi, hi, i, 0)),
            pl.BlockSpec((None, None, s, dk), lambda bi, hi, i: (bi, hi, 0, 0)),
            pl.BlockSpec((None, None, V_HEAD_DIM, s), lambda bi, hi, i: (bi, hi, 0, 0)),
        ],
        out_specs=pl.BlockSpec((None, tq, V_HEAD_DIM), lambda bi, hi, i: (bi, i, hi)),
        out_shape=jax.ShapeDtypeStruct((b, s, h * V_HEAD_DIM), BF16),
        scratch_shapes=scratch,
        compiler_params=_cparams(("parallel", "parallel", "arbitrary"), est),
        name="mla_attention",
    )(q, k, vt)


HALO = V7X_BF16_SUBLANES


def _with_halo(prev, main, nxt):
    i, n = pl.program_id(1), pl.num_programs(1)
    prev = jnp.where(i > 0, prev, jnp.zeros_like(prev))
    nxt = jnp.where(i < n - 1, nxt, jnp.zeros_like(nxt))
    return jnp.concatenate([prev, main, nxt], axis=0)


def _conv_rows(ext, w, bias, rows):
    taps = w.shape[0]
    acc = bias
    for kk in range(taps):
        off = HALO + kk - (taps - 1) // 2
        acc = acc + ext[off:off + rows] * w[kk:kk + 1]
    return acc


def _halo_specs(s, tm, width, col_block):
    per = tm // HALO
    last = s // HALO - 1

    def prev_map(bi, i, *rest):
        return (bi, jnp.maximum(i * per - 1, 0), col_block(*rest))

    def main_map(bi, i, *rest):
        return (bi, i, col_block(*rest))

    def next_map(bi, i, *rest):
        return (bi, jnp.minimum((i + 1) * per, last), col_block(*rest))

    return [
        pl.BlockSpec((None, HALO, width), prev_map),
        pl.BlockSpec((None, tm, width), main_map),
        pl.BlockSpec((None, HALO, width), next_map),
    ]


def _conv_silu_kernel(prev_ref, main_ref, next_ref, w_ref, b_ref, o_ref):
    ext = _with_halo(prev_ref[...], main_ref[...], next_ref[...]).astype(F32)
    o_ref[...] = _silu(_conv_rows(ext, w_ref[...], b_ref[...], o_ref.shape[0])).astype(o_ref.dtype)


def _ssd_conv(u3, conv_w, conv_b, tm, tc):
    b, s, _ = u3.shape
    tm = min(tm, s)
    n_c = SSD_CONV_DIM // tc
    first = U_XBC // tc
    est = 2 * (tm + 2 * HALO) * tc * 2 + 2 * tm * tc * 2 + 8 * (tm + 2 * HALO) * tc * 4
    return pl.pallas_call(
        _conv_silu_kernel,
        grid=(b, s // tm, n_c),
        in_specs=_halo_specs(s, tm, tc, lambda c: first + c) + [
            pl.BlockSpec((SSD_CONV, tc), lambda bi, i, c: (0, c)),
            pl.BlockSpec((1, tc), lambda bi, i, c: (0, c)),
        ],
        out_specs=pl.BlockSpec((None, tm, tc), lambda bi, i, c: (bi, i, c)),
        out_shape=jax.ShapeDtypeStruct((b, s, SSD_CONV_DIM), BF16),
        compiler_params=_cparams(("parallel", "parallel", "parallel"), est),
        name="ssd_conv_silu",
    )(u3, u3, u3, conv_w.astype(F32), conv_b.reshape(1, -1).astype(F32))


def _softplus(x):
    return jnp.maximum(x, 0.0) + jnp.log1p(jnp.exp(-jnp.abs(x)))


def _chunk_cumsums(a, axis, pos_in_chunk, is_fwd):
    n = a.shape[axis]
    cf, cb = a, a
    step = 1
    while step < SSD_CHUNK:
        cf = cf + jnp.where(pos_in_chunk >= step, pltpu.roll(cf, step, axis), 0.0)
        cb = cb + jnp.where(pos_in_chunk < SSD_CHUNK - step, pltpu.roll(cb, n - step, axis), 0.0)
        step *= 2
    return jnp.where(is_fwd, cf, cb)


def _dt_prep_kernel(dtc_ref, dtr_ref, alog_c_ref, bias_c_ref, alog_r_ref, bias_r_ref,
                    acs_c_ref, ef_ref, wst_ref, cd_ref, dt_r_ref, acs_r_ref):
    s = dtc_ref.shape[0]
    n_chunks = s // SSD_CHUNK
    dt = _softplus(dtc_ref[...] + bias_c_ref[...])
    a = -jnp.exp(alog_c_ref[...]) * dt
    pos = lax.broadcasted_iota(jnp.int32, a.shape, 0) % SSD_CHUNK
    fwd = lax.broadcasted_iota(jnp.int32, a.shape, 1) < SSD_HEADS
    acs = _chunk_cumsums(a, 0, pos, fwd)
    acs3 = acs.reshape(n_chunks, SSD_CHUNK, V7X_LANES)
    fwd3 = lax.broadcasted_iota(jnp.int32, (n_chunks, 1, V7X_LANES), 2) < SSD_HEADS
    end = jnp.where(fwd3, acs3[:, SSD_CHUNK - 1:], acs3[:, :1])
    acs_c_ref[...] = acs
    ef_ref[...] = jnp.exp(acs)
    wst_ref[...] = (dt.reshape(acs3.shape) * jnp.exp(end - acs3)).reshape(acs.shape)
    cd_ref[...] = jnp.exp(end)
    dt_r = _softplus(dtr_ref[...] + bias_r_ref[...])
    a_r = -jnp.exp(alog_r_ref[...]) * dt_r
    pos_r = lax.broadcasted_iota(jnp.int32, a_r.shape, 1) % SSD_CHUNK
    fwd_r = lax.broadcasted_iota(jnp.int32, a_r.shape, 0) < SSD_HEADS
    dt_r_ref[...] = dt_r
    acs_r_ref[...] = _chunk_cumsums(a_r, 1, pos_r, fwd_r)


def _dt_prep(misc3, dt_rows, a_log, dt_bias):
    b, s, _ = misc3.shape
    n_chunks = s // SSD_CHUNK
    col = pl.BlockSpec((None, s, V7X_LANES), lambda bi: (bi, 0, 0))
    row = pl.BlockSpec((None, V7X_LANES, s), lambda bi: (bi, 0, 0))
    cvec = pl.BlockSpec((1, V7X_LANES), lambda bi: (0, 0))
    rvec = pl.BlockSpec((V7X_LANES, 1), lambda bi: (0, 0))
    est = 2 * 8 * s * V7X_LANES * 4 + 24 * s * V7X_LANES * 4
    return pl.pallas_call(
        _dt_prep_kernel,
        grid=(b,),
        in_specs=[pl.BlockSpec((None, s, V7X_LANES), lambda bi: (bi, 0, 1)), row, cvec, cvec, rvec, rvec],
        out_specs=[col, col, col, pl.BlockSpec((None, n_chunks, 1, V7X_LANES), lambda bi: (bi, 0, 0, 0)), row, row],
        out_shape=[jax.ShapeDtypeStruct((b, s, V7X_LANES), F32)] * 3
        + [jax.ShapeDtypeStruct((b, n_chunks, 1, V7X_LANES), F32)]
        + [jax.ShapeDtypeStruct((b, V7X_LANES, s), F32)] * 2,
        compiler_params=_cparams(("parallel",), est),
        name="ssd_dt_prep",
    )(misc3, dt_rows, a_log.reshape(1, -1), dt_bias.reshape(1, -1), a_log.reshape(-1, 1), dt_bias.reshape(-1, 1))


R = SSD_HEADS_PER_GROUP
COL_ACS_F, COL_ACS_B, COL_EF_F, COL_EF_B, COL_WST_F, COL_WST_B = (R * n for n in range(6))
ROW_DT_F, ROW_DT_B, ROW_ACS_F, ROW_ACS_B = (R * n for n in range(4))


def _expand_heads(cols, first):
    q = cols.shape[0]
    low = lax.broadcasted_iota(jnp.int32, (q, V7X_LANES), 1) < SSD_HEAD_DIM
    parts = []
    for pair in range(R // 2):
        c0 = first + 2 * pair
        parts.append(jnp.where(low,
                               jnp.broadcast_to(cols[:, c0:c0 + 1], (q, V7X_LANES)),
                               jnp.broadcast_to(cols[:, c0 + 1:c0 + 2], (q, V7X_LANES))))
    return jnp.concatenate(parts, axis=1)


def _ssd_kernel(xs_ref, b_ref, c_ref, colq_ref, rowq_ref, cd_ref, dskip_ref, y_ref, hb_ref, state_ref):
    q = SSD_CHUNK
    n_chunks = xs_ref.shape[0] // q

    def transposed_b(r0):
        return b_ref[pl.ds(r0, q), :].astype(F32).T.astype(BF16)

    state_ref[...] = jnp.zeros_like(state_ref)

    def bwd_body(i, carry):
        c = n_chunks - 1 - i
        r0 = pl.multiple_of(c * q, q)
        hb_ref[c] = state_ref[...].astype(hb_ref.dtype)
        wst = _expand_heads(colq_ref[pl.ds(r0, q), :], COL_WST_B)
        xdw = (xs_ref[pl.ds(r0, q), :].astype(F32) * wst).astype(BF16)
        contrib = jnp.dot(transposed_b(r0), xdw, preferred_element_type=F32)
        state_ref[...] = state_ref[...] * cd_ref[pl.ds(n_chunks + c, 1), :] + contrib
        return carry

    lax.fori_loop(0, n_chunks, bwd_body, 0)

    state_ref[...] = jnp.zeros_like(state_ref)
    li = lax.broadcasted_iota(jnp.int32, (q, q), 0)
    si = lax.broadcasted_iota(jnp.int32, (q, q), 1)
    lower, upper = li >= si, li <= si
    lane_low = lax.broadcasted_iota(jnp.int32, (q, V7X_LANES), 1) < SSD_HEAD_DIM

    def fwd_body(c, carry):
        r0 = pl.multiple_of(c * q, q)
        x_bf = xs_ref[pl.ds(r0, q), :]
        x = x_bf.astype(F32)
        bm = b_ref[pl.ds(r0, q), :]
        cm = c_ref[pl.ds(r0, q), :]
        cols = colq_ref[pl.ds(r0, q), :]
        rows = rowq_ref[:, pl.ds(r0, q)]
        cb = lax.dot_general(cm, bm, (((1,), (1,)), ((), ())), preferred_element_type=F32)

        def head_matrix(r):
            seg_f = cols[:, COL_ACS_F + r:COL_ACS_F + r + 1] - rows[ROW_ACS_F + r:ROW_ACS_F + r + 1, :]
            seg_b = cols[:, COL_ACS_B + r:COL_ACS_B + r + 1] - rows[ROW_ACS_B + r:ROW_ACS_B + r + 1, :]
            lf = jnp.where(lower, jnp.exp(seg_f), 0.0) * rows[ROW_DT_F + r:ROW_DT_F + r + 1, :]
            lb = jnp.where(upper, jnp.exp(seg_b), 0.0) * rows[ROW_DT_B + r:ROW_DT_B + r + 1, :]
            return (cb * (lf + lb)).astype(BF16)

        diag = []
        for pair in range(R // 2):
            xp = x_bf[:, pair * V7X_LANES:(pair + 1) * V7X_LANES]
            zero = jnp.zeros_like(xp)
            rhs = jnp.concatenate([jnp.where(lane_low, xp, zero), jnp.where(lane_low, zero, xp)], axis=0)
            lhs = jnp.concatenate([head_matrix(2 * pair), head_matrix(2 * pair + 1)], axis=1)
            diag.append(jnp.dot(lhs, rhs, preferred_element_type=F32))
        y = jnp.concatenate(diag, axis=1)

        h_prev = state_ref[...]
        y = y + jnp.dot(cm, h_prev.astype(BF16), preferred_element_type=F32) * _expand_heads(cols, COL_EF_F)
        y = y + jnp.dot(cm, hb_ref[c], preferred_element_type=F32) * _expand_heads(cols, COL_EF_B)
        y = y + x * dskip_ref[...]
        y_ref[pl.ds(r0, q), :] = y.astype(y_ref.dtype)

        xdw = (x * _expand_heads(cols, COL_WST_F)).astype(BF16)
        contrib = jnp.dot(transposed_b(r0), xdw, preferred_element_type=F32)
        state_ref[...] = h_prev * cd_ref[pl.ds(c, 1), :] + contrib
        return carry

    lax.fori_loop(0, n_chunks, fwd_body, 0)


def _ssd_scan(xbc_c, colq, rowq, cd, dskip):
    b, s, _ = xbc_c.shape
    n_chunks = s // SSD_CHUNK
    gw = SSD_GROUP_WIDTH
    b_first = SSD_D_INNER // SSD_STATE
    c_first = b_first + SSD_GROUPS
    n_col, n_row = colq.shape[-1], rowq.shape[-2]
    est = (2 * s * gw * 2 * 2 + 4 * s * SSD_STATE * 2 + 2 * s * V7X_LANES * 4 + 2 * n_row * s * 4
           + n_chunks * SSD_STATE * gw * 2 + 4 * n_chunks * gw * 4 + 64 * SSD_CHUNK * gw * 4)
    return pl.pallas_call(
        _ssd_kernel,
        grid=(b, SSD_GROUPS),
        in_specs=[
            pl.BlockSpec((None, s, gw), lambda bi, g: (bi, 0, g)),
            pl.BlockSpec((None, s, SSD_STATE), lambda bi, g: (bi, 0, b_first + g)),
            pl.BlockSpec((None, s, SSD_STATE), lambda bi, g: (bi, 0, c_first + g)),
            pl.BlockSpec((None, None, s, n_col), lambda bi, g: (bi, g, 0, 0)),
            pl.BlockSpec((None, None, n_row, s), lambda bi, g: (bi, g, 0, 0)),
            pl.BlockSpec((None, None, 2 * n_chunks, gw), lambda bi, g: (bi, g, 0, 0)),
            pl.BlockSpec((None, 1, gw), lambda bi, g: (g, 0, 0)),
        ],
        out_specs=pl.BlockSpec((None, s, gw), lambda bi, g: (bi, 0, g)),
        out_shape=jax.ShapeDtypeStruct((b, s, SSD_D_INNER), BF16),
        scratch_shapes=[
            pltpu.VMEM((n_chunks, SSD_STATE, gw), BF16),
            pltpu.VMEM((SSD_STATE, gw), F32),
        ],
        compiler_params=_cparams(("parallel", "parallel"), est),
        name="ssd_scan",
    )(xbc_c, xbc_c, xbc_c, colq, rowq, cd, dskip)


def _mix_kernel(o_ref, y_ref, z_ref, nw_ref, wa_ref, ws_ref, ga_ref, gs_ref, m_ref, yn_ref):
    @pl.when(pl.program_id(1) == 0)
    def _():
        yz = y_ref[...].astype(F32) * _silu(z_ref[...].astype(F32))
        yn_ref[...] = (yz * _rms_scale(yz) * nw_ref[...]).astype(yn_ref.dtype)

    attn = jnp.dot(o_ref[...], wa_ref[...], preferred_element_type=F32)
    ssd = jnp.dot(yn_ref[...], ws_ref[...], preferred_element_type=F32)
    mixed = _sigmoid(ga_ref[...].astype(F32)) * attn + _sigmoid(gs_ref[...].astype(F32)) * ssd
    m_ref[...] = mixed.astype(m_ref.dtype)


def _mix(o2, y2, u2, norm_w, wa, ws, tm, tn):
    t = o2.shape[0]
    tm = min(tm, t)
    d_attn, d_ssd = o2.shape[1], y2.shape[1]
    est = (2 * tm * d_attn * 2 + 4 * tm * d_ssd * 2 + tm * d_ssd * 2 + 2 * (d_attn + d_ssd) * tn * 2
           + 6 * tm * tn * 2 + 4 * tm * tn * 4)
    return pl.pallas_call(
        _mix_kernel,
        grid=(t // tm, D_MODEL // tn),
        in_specs=[
            pl.BlockSpec((tm, d_attn), lambda i, j: (i, 0)),
            pl.BlockSpec((tm, d_ssd), lambda i, j: (i, 0)),
            pl.BlockSpec((tm, d_ssd), lambda i, j: (i, U_Z // SSD_D_INNER)),
            pl.BlockSpec((1, d_ssd), lambda i, j: (0, 0)),
            pl.BlockSpec((d_attn, tn), lambda i, j: (0, j)),
            pl.BlockSpec((d_ssd, tn), lambda i, j: (0, j)),
            pl.BlockSpec((tm, tn), lambda i, j: (i, U_GA // tn + j)),
            pl.BlockSpec((tm, tn), lambda i, j: (i, U_GS // tn + j)),
        ],
        out_specs=pl.BlockSpec((tm, tn), lambda i, j: (i, j)),
        out_shape=jax.ShapeDtypeStruct((t, D_MODEL), BF16),
        scratch_shapes=[pltpu.VMEM((tm, d_ssd), BF16)],
        compiler_params=_cparams(("parallel", "arbitrary"), est),
        name="branch_mix",
    )(o2, y2, u2, norm_w.reshape(1, -1).astype(F32), wa, ws, u2, u2)


def _residual_matmul_kernel(a_ref, w_ref, x_ref, o_ref):
    o_ref[...] = x_ref[...] + jnp.dot(a_ref[...], w_ref[...], preferred_element_type=F32)


def _residual_matmul(a, w, x, tm, tn):
    t, k = a.shape
    n = w.shape[1]
    tm, tn = min(tm, t), min(tn, n)
    est = 2 * tm * k * 2 + 2 * k * tn * 2 + 4 * tm * tn * 4
    return pl.pallas_call(
        _residual_matmul_kernel,
        grid=(t // tm, n // tn),
        in_specs=[
            pl.BlockSpec((tm, k), lambda i, j: (i, 0)),
            pl.BlockSpec((k, tn), lambda i, j: (0, j)),
            pl.BlockSpec((tm, tn), lambda i, j: (i, j)),
        ],
        out_specs=pl.BlockSpec((tm, tn), lambda i, j: (i, j)),
        out_shape=jax.ShapeDtypeStruct((t, n), F32),
        compiler_params=_cparams(("parallel", "parallel"), est),
        name="residual_matmul",
    )(a, w, x)


def _ffn_kernel(prev_ref, main_ref, next_ref, g_ref, wg_ref, wv_ref, cwg_ref, cwv_ref, cbg_ref, cbv_ref,
                wd_ref, gf_ref, o_ref, hn_ref):
    j, n_j = pl.program_id(2), pl.num_programs(2)
    tm = main_ref.shape[0]

    @pl.when(j == 0)
    def _():
        ext = _with_halo(prev_ref[...], main_ref[...], next_ref[...])
        hn_ref[...] = (ext * _rms_scale(ext) * g_ref[...]).astype(hn_ref.dtype)

    hn = hn_ref[...]
    gate = _conv_rows(jnp.dot(hn, wg_ref[...], preferred_element_type=F32), cwg_ref[...], cbg_ref[...], tm)
    val = _conv_rows(jnp.dot(hn, wv_ref[...], preferred_element_type=F32), cwv_ref[...], cbv_ref[...], tm)
    act = (_silu(gate) * val).astype(BF16)
    down = jnp.dot(act, wd_ref[...], preferred_element_type=F32)

    @pl.when(j == 0)
    def _():
        o_ref[...] = main_ref[...] + down

    @pl.when(j > 0)
    def _():
        o_ref[...] += down

    @pl.when(j == n_j - 1)
    def _():
        h = o_ref[...]
        o_ref[...] = h * _rms_scale(h) * gf_ref[...]


def _ffn(h3, gain, w_up, conv_w, conv_b, w_down, final_gain, tm, tf):
    b, s, d = h3.shape
    tm = min(tm, s)
    n_f = FFN_DIM // tf
    row = lambda bi, i, j: (0, 0)
    est = (2 * (tm + 2 * HALO) * d * 4 + (tm + 2 * HALO) * d * 2 + 2 * 2 * d * tf * 2 + 2 * tf * d * 2
           + 2 * tm * d * 4 + 8 * (tm + 2 * HALO) * tf * 4)
    return pl.pallas_call(
        _ffn_kernel,
        grid=(b, s // tm, n_f),
        in_specs=_halo_specs(s, tm, d, lambda j: 0) + [
            pl.BlockSpec((1, d), row),
            pl.BlockSpec((d, tf), lambda bi, i, j: (0, j)),
            pl.BlockSpec((d, tf), lambda bi, i, j: (0, n_f + j)),
            pl.BlockSpec((FFN_CONV, tf), lambda bi, i, j: (0, j)),
            pl.BlockSpec((FFN_CONV, tf), lambda bi, i, j: (0, n_f + j)),
            pl.BlockSpec((1, tf), lambda bi, i, j: (0, j)),
            pl.BlockSpec((1, tf), lambda bi, i, j: (0, n_f + j)),
            pl.BlockSpec((tf, d), lambda bi, i, j: (j, 0)),
            pl.BlockSpec((1, d), row),
        ],
        out_specs=pl.BlockSpec((None, tm, d), lambda bi, i, j: (bi, i, 0)),
        out_shape=jax.ShapeDtypeStruct((b, s, d), F32),
        scratch_shapes=[pltpu.VMEM((tm + 2 * HALO, d), BF16)],
        compiler_params=_cparams(("parallel", "parallel", "arbitrary"), est),
        name="conv_ffn",
    )(h3, h3, h3, gain.reshape(1, -1).astype(F32), w_up, w_up, conv_w.astype(F32), conv_w.astype(F32),
      conv_b.reshape(1, -1).astype(F32), conv_b.reshape(1, -1).astype(F32), w_down,
      final_gain.reshape(1, -1).astype(F32))


def _swap_halves(w):
    half = w.shape[-1] // 2
    return jnp.concatenate([w[..., half:], w[..., :half]], axis=-1)


def _group_major_cols(t):
    b, s, _ = t.shape
    return t.reshape(b, s, 2, SSD_GROUPS, R).transpose(0, 3, 1, 2, 4)


def _layer(h, cos_t, sin_t, norm_mix_w, w_in, q_norm_w, w_uq, kv_norm_w, w_ukv, w_o_attn, ssd_conv_w,
           ssd_conv_b, a_log_fwd, a_log_bwd, dt_bias_fwd, dt_bias_bwd, ssd_d, ssd_norm_w, w_o_ssd, w_out):
    b, s, d = h.shape
    t = b * s
    n_chunks = s // SSD_CHUNK
    x2 = h.reshape(t, d)

    splits = np.cumsum([Q_LORA_RANK, KV_LORA_RANK, QK_ROPE_DIM, SSD_D_INNER, SSD_CONV_DIM, SSD_HEADS,
                        SSD_HEADS, D_MODEL])
    w_ql, w_kvl, w_kr, w_z, w_xbc, w_dtf, w_dtb, w_ga, w_gs = jnp.split(w_in, [int(v) for v in splits], axis=1)
    w_main = jnp.concatenate([w_z, w_xbc, w_ga, w_gs, w_ql, w_kvl], axis=1).astype(BF16)
    w_misc = jnp.concatenate([w_kr, _swap_halves(w_kr), w_dtf, w_dtb], axis=1).astype(BF16)
    u2 = _norm_matmul(x2, norm_mix_w, w_main, BF16, tm=1024, tn=512)
    misc2 = _norm_matmul(x2, norm_mix_w, w_misc, F32, tm=1024, tn=2 * V7X_LANES)
    u3 = u2.reshape(b, s, U_WIDTH)
    misc3 = misc2.reshape(b, s, 2 * V7X_LANES)

    wq = w_uq.reshape(Q_LORA_RANK, MLA_HEADS, QK_NOPE_DIM + QK_ROPE_DIM)
    wq_rope = wq[..., QK_NOPE_DIM:]
    wq = jnp.concatenate([wq[..., :QK_NOPE_DIM], wq_rope, _swap_halves(wq_rope)], axis=-1)
    wq = wq.transpose(1, 0, 2).astype(BF16)
    wkv = w_ukv.reshape(KV_LORA_RANK, MLA_HEADS, QK_NOPE_DIM + V_HEAD_DIM).transpose(1, 0, 2).astype(BF16)
    q, k, vt = _mla_projections(u3, misc3, cos_t, sin_t, q_norm_w, wq, kv_norm_w, wkv, tm=512)
    o = _attention(q, k, vt, tq=1024, tk=1024)

    xbc_c = _ssd_conv(u3, ssd_conv_w, ssd_conv_b, tm=512, tc=512)
    dt_rows = misc3[:, :, V7X_LANES:].transpose(0, 2, 1)
    a_log = jnp.concatenate([a_log_fwd, a_log_bwd]).astype(F32)
    dt_bias = jnp.concatenate([dt_bias_fwd, dt_bias_bwd]).astype(F32)
    acs_c, ef_c, wst_c, cd, dt_r, acs_r = _dt_prep(misc3, dt_rows, a_log, dt_bias)
    colq = jnp.concatenate([_group_major_cols(v) for v in (acs_c, ef_c, wst_c)], axis=3)
    colq = colq.reshape(b, SSD_GROUPS, s, 6 * R)
    rowq = jnp.stack([dt_r, acs_r], axis=1).reshape(b, 2, 2, SSD_GROUPS, R, s)
    rowq = rowq.transpose(0, 3, 1, 2, 4, 5).reshape(b, SSD_GROUPS, 4 * R, s)
    cd = jnp.repeat(cd.reshape(b, n_chunks, 2, SSD_GROUPS, R), SSD_HEAD_DIM, axis=-1)
    cd = cd.transpose(0, 3, 2, 1, 4).reshape(b, SSD_GROUPS, 2 * n_chunks, SSD_GROUP_WIDTH)
    dskip = jnp.repeat(ssd_d.astype(F32), SSD_HEAD_DIM).reshape(SSD_GROUPS, 1, SSD_GROUP_WIDTH)
    y = _ssd_scan(xbc_c, colq, rowq, cd, dskip)

    mixed = _mix(o.reshape(t, -1), y.reshape(t, -1), u2, ssd_norm_w, w_o_attn.astype(BF16),
                 w_o_ssd.astype(BF16), tm=512, tn=512)
    h1 = _residual_matmul(mixed, w_out.astype(BF16), x2, tm=1024, tn=1024)
    return h1.reshape(b, s, d)


def kernel(x, positions, norm_mix_w, w_in, q_norm_w, w_uq, kv_norm_w, w_ukv, w_o_attn, ssd_conv_w, ssd_conv_b,
           a_log_fwd, a_log_bwd, dt_bias_fwd, dt_bias_bwd, ssd_d, ssd_norm_w, w_o_ssd, w_out, norm_ffn_w,
           ffn_w_up, ffn_conv_w, ffn_conv_b, ffn_w_down, norm_final_w):
    depth = w_in.shape[0]
    assert depth == 1, "the conv-gated MLP kernel fuses the final RMSNorm, which is only valid for one layer"
    half = QK_ROPE_DIM // 2
    inv_freq = ROPE_THETA ** (-jnp.arange(half, dtype=F32) / half)
    ang = positions.astype(F32)[..., None] * inv_freq
    cos, sin = jnp.cos(ang), jnp.sin(ang)
    pad = jnp.zeros(cos.shape[:-1] + (V7X_LANES - QK_ROPE_DIM,), F32)
    cos_t = jnp.concatenate([cos, cos, pad], axis=-1)
    sin_t = jnp.concatenate([-sin, sin, pad], axis=-1)

    l = 0
    h1 = _layer(x, cos_t, sin_t, norm_mix_w[l], w_in[l], q_norm_w[l], w_uq[l], kv_norm_w[l], w_ukv[l],
                w_o_attn[l], ssd_conv_w[l], ssd_conv_b[l], a_log_fwd[l], a_log_bwd[l], dt_bias_fwd[l],
                dt_bias_bwd[l], ssd_d[l], ssd_norm_w[l], w_o_ssd[l], w_out[l])
    return _ffn(h1, norm_ffn_w[l], ffn_w_up[l].astype(BF16), ffn_conv_w[l], ffn_conv_b[l],
                ffn_w_down[l].astype(BF16), norm_final_w, tm=512, tf=512)
```

```python
import functools
import math

import jax
import jax.numpy as jnp
import numpy as np
from jax import lax
from jax.experimental import pallas as pl
from jax.experimental.pallas import tpu as pltpu

D_MODEL = 2048
MLA_HEADS = 16
Q_LORA_RANK = 512
KV_LORA_RANK = 512
QK_NOPE_DIM = 128
QK_ROPE_DIM = 64
V_HEAD_DIM = 128
ROPE_THETA = 10000.0
SSD_D_INNER = 2 * D_MODEL
SSD_HEAD_DIM = 64
SSD_HEADS = SSD_D_INNER // SSD_HEAD_DIM
SSD_GROUPS = 8
SSD_HEADS_PER_GROUP = SSD_HEADS // SSD_GROUPS
SSD_STATE = 128
SSD_CONV = 5
SSD_CHUNK = 128
SSD_CONV_DIM = SSD_D_INNER + 2 * SSD_GROUPS * SSD_STATE
SSD_GROUP_WIDTH = SSD_HEADS_PER_GROUP * SSD_HEAD_DIM
FFN_DIM = 5632
FFN_CONV = 3
EPS = 1e-6

V7X_LANES = 128
V7X_BF16_SUBLANES = 16
V7X_VMEM_BYTES = 64 * 1024 * 1024
VMEM_LIMIT_CAP = V7X_VMEM_BYTES - 8 * 1024 * 1024

F32 = jnp.float32
BF16 = jnp.bfloat16

U_Z = 0
U_XBC = U_Z + SSD_D_INNER
U_GA = U_XBC + SSD_CONV_DIM
U_GS = U_GA + D_MODEL
U_QL = U_GS + D_MODEL
U_KVL = U_QL + Q_LORA_RANK
U_WIDTH = U_KVL + KV_LORA_RANK


def _cparams(semantics, vmem_estimate_bytes):
    limit = int(min(max(vmem_estimate_bytes * 5 // 4, 16 * 1024 * 1024), VMEM_LIMIT_CAP))
    return pltpu.CompilerParams(dimension_semantics=semantics, vmem_limit_bytes=limit)


def _rms_scale(x):
    return lax.rsqrt(jnp.mean(x * x, axis=-1, keepdims=True) + EPS)


def _silu(x):
    return x * (1.0 / (1.0 + jnp.exp(-x)))


def _sigmoid(x):
    return 1.0 / (1.0 + jnp.exp(-x))


def _norm_matmul_kernel(x_ref, g_ref, w_ref, o_ref, xn_ref):
    @pl.when(pl.program_id(1) == 0)
    def _():
        x = x_ref[...].astype(F32)
        xn_ref[...] = (x * _rms_scale(x) * g_ref[...]).astype(xn_ref.dtype)

    o_ref[...] = jnp.dot(xn_ref[...], w_ref[...], preferred_element_type=F32).astype(o_ref.dtype)


def _norm_matmul(x, gain, w, out_dtype, tm, tn):
    t, k = x.shape
    n = w.shape[1]
    tm, tn = min(tm, t), min(tn, n)
    est = 2 * tm * k * x.dtype.itemsize + tm * k * 2 + 2 * k * tn * 2 + 2 * tm * tn * 4
    return pl.pallas_call(
        _norm_matmul_kernel,
        grid=(t // tm, n // tn),
        in_specs=[
            pl.BlockSpec((tm, k), lambda i, j: (i, 0)),
            pl.BlockSpec((1, k), lambda i, j: (0, 0)),
            pl.BlockSpec((k, tn), lambda i, j: (0, j)),
        ],
        out_specs=pl.BlockSpec((tm, tn), lambda i, j: (i, j)),
        out_shape=jax.ShapeDtypeStruct((t, n), out_dtype),
        scratch_shapes=[pltpu.VMEM((tm, k), BF16)],
        compiler_params=_cparams(("parallel", "arbitrary"), est),
        name="norm_matmul",
    )(x, gain.reshape(1, k).astype(F32), w)


def _rope_half(y2, cos_t, sin_t):
    return y2 * cos_t + pltpu.roll(y2, QK_ROPE_DIM, 1) * sin_t


def _q_proj_kernel(ql_ref, g_ref, w_ref, cos_ref, sin_ref, o_ref, *, scale):
    x = ql_ref[...].astype(F32)
    xn = (x * _rms_scale(x) * g_ref[...]).astype(BF16)
    cos_t = cos_ref[...] * scale
    sin_t = sin_ref[...] * scale
    for h in range(MLA_HEADS):
        y = jnp.dot(xn, w_ref[h], preferred_element_type=F32)
        o_ref[h, :, :QK_NOPE_DIM] = (y[:, :QK_NOPE_DIM] * scale).astype(o_ref.dtype)
        o_ref[h, :, QK_NOPE_DIM:] = _rope_half(y[:, QK_NOPE_DIM:], cos_t, sin_t).astype(o_ref.dtype)


def _kv_proj_kernel(kvl_ref, g_ref, w_ref, kr_ref, cos_ref, sin_ref, k_ref, vt_ref):
    x = kvl_ref[...].astype(F32)
    xn = (x * _rms_scale(x) * g_ref[...]).astype(BF16)
    roped = _rope_half(kr_ref[...], cos_ref[...], sin_ref[...]).astype(k_ref.dtype)
    for h in range(MLA_HEADS):
        y = jnp.dot(xn, w_ref[h], preferred_element_type=F32)
        k_ref[h, :, :QK_NOPE_DIM] = y[:, :QK_NOPE_DIM].astype(k_ref.dtype)
        k_ref[h, :, QK_NOPE_DIM:] = roped
        vt_ref[h] = y[:, QK_NOPE_DIM:].T.astype(vt_ref.dtype)


def _mla_projections(u3, misc3, cos_t, sin_t, q_gain, wq, kv_gain, wkv, tm):
    b, s, _ = u3.shape
    tm = min(tm, s)
    grid = (b, s // tm)
    head_w = 2 * V7X_LANES
    scale = (QK_NOPE_DIM + QK_ROPE_DIM) ** -0.5 * math.log2(math.e)
    table = pl.BlockSpec((None, tm, V7X_LANES), lambda bi, i: (bi, i, 0))
    gain = pl.BlockSpec((1, Q_LORA_RANK), lambda bi, i: (0, 0))
    wspec = pl.BlockSpec((MLA_HEADS, Q_LORA_RANK, head_w), lambda bi, i: (0, 0, 0))
    est = (2 * tm * Q_LORA_RANK * 2 + 2 * MLA_HEADS * Q_LORA_RANK * head_w * 2
           + 2 * MLA_HEADS * tm * (head_w + V_HEAD_DIM) * 2 + 8 * tm * V7X_LANES * 4)
    params = _cparams(("parallel", "parallel"), est)
    q = pl.pallas_call(
        functools.partial(_q_proj_kernel, scale=scale),
        grid=grid,
        in_specs=[
            pl.BlockSpec((None, tm, Q_LORA_RANK), lambda bi, i: (bi, i, U_QL // Q_LORA_RANK)),
            gain, wspec, table, table,
        ],
        out_specs=pl.BlockSpec((None, MLA_HEADS, tm, head_w), lambda bi, i: (bi, 0, i, 0)),
        out_shape=jax.ShapeDtypeStruct((b, MLA_HEADS, s, head_w), BF16),
        compiler_params=params,
        name="mla_q_proj",
    )(u3, q_gain.reshape(1, -1).astype(F32), wq, cos_t, sin_t)
    k, vt = pl.pallas_call(
        _kv_proj_kernel,
        grid=grid,
        in_specs=[
            pl.BlockSpec((None, tm, KV_LORA_RANK), lambda bi, i: (bi, i, U_KVL // KV_LORA_RANK)),
            gain, wspec,
            pl.BlockSpec((None, tm, V7X_LANES), lambda bi, i: (bi, i, 0)),
            table, table,
        ],
        out_specs=[
            pl.BlockSpec((None, MLA_HEADS, tm, head_w), lambda bi, i: (bi, 0, i, 0)),
            pl.BlockSpec((None, MLA_HEADS, V_HEAD_DIM, tm), lambda bi, i: (bi, 0, 0, i)),
        ],
        out_shape=[
            jax.ShapeDtypeStruct((b, MLA_HEADS, s, head_w), BF16),
            jax.ShapeDtypeStruct((b, MLA_HEADS, V_HEAD_DIM, s), BF16),
        ],
        compiler_params=params,
        name="mla_kv_proj",
    )(u3, kv_gain.reshape(1, -1).astype(F32), wkv, misc3, cos_t, sin_t)
    return q, k, vt


def _attn_kernel(q_ref, k_ref, vt_ref, o_ref, acc_ref, st0_ref, st1_ref, p0_ref, p1_ref, *, tk):
    q = q_ref[...]
    tq = q.shape[0]
    n_k = k_ref.shape[0] // tk
    st_refs, p_refs = (st0_ref, st1_ref), (p0_ref, p1_ref)

    def scores(j):
        st_refs[j % 2][...] = lax.dot_general(k_ref[j * tk:(j + 1) * tk, :], q, (((1,), (1,)), ((), ())),
                                              preferred_element_type=F32)

    m = jnp.full((1, tq), -jnp.inf, F32)
    l = jnp.zeros((1, tq), F32)
    scores(0)
    for j in range(n_k):
        if j + 1 < n_k:
            scores(j + 1)
        st = st_refs[j % 2][...]
        m_new = jnp.maximum(m, jnp.max(st, axis=0, keepdims=True))
        alpha = jnp.exp2(m - m_new)
        p = jnp.exp2(st - m_new)
        l = alpha * l + jnp.sum(p, axis=0, keepdims=True)
        m = m_new
        p_refs[j % 2][...] = p.astype(BF16)
        pv = jnp.dot(vt_ref[:, j * tk:(j + 1) * tk], p_refs[j % 2][...], preferred_element_type=F32)
        if j == 0:
            acc_ref[...] = pv
        else:
            acc_ref[...] = alpha * acc_ref[...] + pv
    o_ref[...] = (acc_ref[...] * (1.0 / l)).T.astype(o_ref.dtype)


def _attention(q, k, vt, tq, tk):
    b, h, s, dk = q.shape
    tq, tk = min(tq, s), min(tk, s)
    est = (2 * tq * dk * 2 + 2 * s * dk * 2 + 2 * V_HEAD_DIM * s * 2 + 2 * tq * V_HEAD_DIM * 2
           + V_HEAD_DIM * tq * 4 + 2 * tk * tq * (4 + 2) + 4 * tk * tq * 4)
    scratch = [pltpu.VMEM((V_HEAD_DIM, tq), F32), pltpu.VMEM((tk, tq), F32), pltpu.VMEM((tk, tq), F32),
               pltpu.VMEM((tk, tq), BF16), pltpu.VMEM((tk, tq), BF16)]
    return pl.pallas_call(
        functools.partial(_attn_kernel, tk=tk),
        grid=(b, h, s // tq),
        in_specs=[
            pl.BlockSpec((None, None, tq, dk), lambda bi, hi, i: (bi, hi, i, 0)),
            pl.BlockSpec((None, None, s, dk), lambda bi, hi, i: (bi, hi, 0, 0)),
            pl.BlockSpec((None, None, V_HEAD_DIM, s), lambda bi, hi, i: (bi, hi, 0, 0)),
        ],
        out_specs=pl.BlockSpec((None, tq, V_HEAD_DIM), lambda bi, hi, i: (bi, i, hi)),
        out_shape=jax.ShapeDtypeStruct((b, s, h * V_HEAD_DIM), BF16),
        scratch_shapes=scratch,
        compiler_params=_cparams(("parallel", "parallel", "arbitrary"), est),
        name="mla_attention",
    )(q, k, vt)


HALO = V7X_BF16_SUBLANES


def _with_halo(prev, main, nxt):
    i, n = pl.program_id(1), pl.num_programs(1)
    prev = jnp.where(i > 0, prev, jnp.zeros_like(prev))
    nxt = jnp.where(i < n - 1, nxt, jnp.zeros_like(nxt))
    return jnp.concatenate([prev, main, nxt], axis=0)


def _conv_rows(ext, w, bias, rows):
    taps = w.shape[0]
    acc = bias
    for kk in range(taps):
        off = HALO + kk - (taps - 1) // 2
        acc = acc + ext[off:off + rows] * w[kk:kk + 1]
    return acc


def _halo_specs(s, tm, width, col_block):
    per = tm // HALO
    last = s // HALO - 1

    def prev_map(bi, i, *rest):
        return (bi, jnp.maximum(i * per - 1, 0), col_block(*rest))

    def main_map(bi, i, *rest):
        return (bi, i, col_block(*rest))

    def next_map(bi, i, *rest):
        return (bi, jnp.minimum((i + 1) * per, last), col_block(*rest))

    return [
        pl.BlockSpec((None, HALO, width), prev_map),
        pl.BlockSpec((None, tm, width), main_map),
        pl.BlockSpec((None, HALO, width), next_map),
    ]


def _conv_silu_kernel(prev_ref, main_ref, next_ref, w_ref, b_ref, o_ref):
    ext = _with_halo(prev_ref[...], main_ref[...], next_ref[...]).astype(F32)
    o_ref[...] = _silu(_conv_rows(ext, w_ref[...], b_ref[...], o_ref.shape[0])).astype(o_ref.dtype)


def _ssd_conv(u3, conv_w, conv_b, tm, tc):
    b, s, _ = u3.shape
    tm = min(tm, s)
    n_c = SSD_CONV_DIM // tc
    first = U_XBC // tc
    est = 2 * (tm + 2 * HALO) * tc * 2 + 2 * tm * tc * 2 + 8 * (tm + 2 * HALO) * tc * 4
    return pl.pallas_call(
        _conv_silu_kernel,
        grid=(b, s // tm, n_c),
        in_specs=_halo_specs(s, tm, tc, lambda c: first + c) + [
            pl.BlockSpec((SSD_CONV, tc), lambda bi, i, c: (0, c)),
            pl.BlockSpec((1, tc), lambda bi, i, c: (0, c)),
        ],
        out_specs=pl.BlockSpec((None, tm, tc), lambda bi, i, c: (bi, i, c)),
        out_shape=jax.ShapeDtypeStruct((b, s, SSD_CONV_DIM), BF16),
        compiler_params=_cparams(("parallel", "parallel", "parallel"), est),
        name="ssd_conv_silu",
    )(u3, u3, u3, conv_w.astype(F32), conv_b.reshape(1, -1).astype(F32))


def _softplus(x):
    return jnp.maximum(x, 0.0) + jnp.log1p(jnp.exp(-jnp.abs(x)))


def _chunk_cumsums(a, axis, pos_in_chunk, is_fwd):
    n = a.shape[axis]
    cf, cb = a, a
    step = 1
    while step < SSD_CHUNK:
        cf = cf + jnp.where(pos_in_chunk >= step, pltpu.roll(cf, step, axis), 0.0)
        cb = cb + jnp.where(pos_in_chunk < SSD_CHUNK - step, pltpu.roll(cb, n - step, axis), 0.0)
        step *= 2
    return jnp.where(is_fwd, cf, cb)


def _split3(x):
    hi = x.astype(BF16)
    rest = x - hi.astype(F32)
    mid = rest.astype(BF16)
    lo = (rest - mid.astype(F32)).astype(BF16)
    return hi, mid, lo


def _dt_prep_kernel(dtc_ref, dtr_ref, alog_c_ref, bias_c_ref, alog_r_ref, bias_r_ref,
                    col_ref, cd_ref, dt_r_ref, nacs_r_ref):
    s = dtc_ref.shape[0]
    n_chunks = s // SSD_CHUNK
    dt = _softplus(dtc_ref[...] + bias_c_ref[...])
    a = -jnp.exp(alog_c_ref[...]) * dt
    pos = lax.broadcasted_iota(jnp.int32, a.shape, 0) % SSD_CHUNK
    fwd = lax.broadcasted_iota(jnp.int32, a.shape, 1) < SSD_HEADS
    acs = _chunk_cumsums(a, 0, pos, fwd)
    acs3 = acs.reshape(n_chunks, SSD_CHUNK, V7X_LANES)
    fwd3 = lax.broadcasted_iota(jnp.int32, (n_chunks, 1, V7X_LANES), 2) < SSD_HEADS
    end = jnp.where(fwd3, acs3[:, SSD_CHUNK - 1:], acs3[:, :1])
    wst = (dt.reshape(acs3.shape) * jnp.exp(end - acs3)).reshape(acs.shape)
    for qi, table in enumerate((acs, jnp.exp(acs), wst)):
        for pi, piece in enumerate(_split3(table)):
            col_ref[3 * qi + pi] = piece
    cd_ref[...] = jnp.exp(end)
    dt_r = _softplus(dtr_ref[...] + bias_r_ref[...])
    a_r = -jnp.exp(alog_r_ref[...]) * dt_r
    pos_r = lax.broadcasted_iota(jnp.int32, a_r.shape, 1) % SSD_CHUNK
    fwd_r = lax.broadcasted_iota(jnp.int32, a_r.shape, 0) < SSD_HEADS
    dt_r_ref[...] = dt_r
    for pi, piece in enumerate(_split3(-_chunk_cumsums(a_r, 1, pos_r, fwd_r))):
        nacs_r_ref[pi] = piece


def _dt_prep(misc3, dt_rows, a_log, dt_bias):
    b, s, _ = misc3.shape
    n_chunks = s // SSD_CHUNK
    row = pl.BlockSpec((None, V7X_LANES, s), lambda bi: (bi, 0, 0))
    cvec = pl.BlockSpec((1, V7X_LANES), lambda bi: (0, 0))
    rvec = pl.BlockSpec((V7X_LANES, 1), lambda bi: (0, 0))
    est = 2 * 8 * s * V7X_LANES * 4 + 24 * s * V7X_LANES * 4
    return pl.pallas_call(
        _dt_prep_kernel,
        grid=(b,),
        in_specs=[pl.BlockSpec((None, s, V7X_LANES), lambda bi: (bi, 0, 1)), row, cvec, cvec, rvec, rvec],
        out_specs=[
            pl.BlockSpec((None, 9, s, V7X_LANES), lambda bi: (bi, 0, 0, 0)),
            pl.BlockSpec((None, n_chunks, 1, V7X_LANES), lambda bi: (bi, 0, 0, 0)),
            row,
            pl.BlockSpec((None, 3, V7X_LANES, s), lambda bi: (bi, 0, 0, 0)),
        ],
        out_shape=[
            jax.ShapeDtypeStruct((b, 9, s, V7X_LANES), BF16),
            jax.ShapeDtypeStruct((b, n_chunks, 1, V7X_LANES), F32),
            jax.ShapeDtypeStruct((b, V7X_LANES, s), F32),
            jax.ShapeDtypeStruct((b, 3, V7X_LANES, s), BF16),
        ],
        compiler_params=_cparams(("parallel",), est),
        name="ssd_dt_prep",
    )(misc3, dt_rows, a_log.reshape(1, -1), dt_bias.reshape(1, -1), a_log.reshape(-1, 1), dt_bias.reshape(-1, 1))


R = SSD_HEADS_PER_GROUP
N_PAIRS = R // 2
PIECES = 3
ACS_LANES = PIECES * 2 * R
ACS_TABLE_WIDTH = 64
EXPAND_EF, EXPAND_WST = 0, 1
SSD_BWD_CHUNKS_PER_STEP = 8
SSD_FWD_UNROLL = 4


def _ssd_constants():
    sel = np.zeros((2 * N_PAIRS, ACS_LANES, 2 * V7X_LANES), np.float32)
    expand = np.zeros((4, V7X_LANES, SSD_GROUP_WIDTH), np.float32)
    for piece in range(PIECES):
        for d in range(2):
            for pair in range(N_PAIRS):
                for half in range(2):
                    lane = 2 * R * piece + R * d + 2 * pair + half
                    sel[2 * pair + d, lane, half * V7X_LANES:(half + 1) * V7X_LANES] = 1.0
            for t in range(2):
                for r in range(R):
                    lane = ACS_LANES * t + 2 * R * piece + R * d + r
                    expand[2 * t + d, lane, r * SSD_HEAD_DIM:(r + 1) * SSD_HEAD_DIM] = 1.0
    return jnp.asarray(sel, BF16), jnp.asarray(expand, BF16)


def _ssd_kernel(xs_ref, b_ref, c_ref, acs_ref, exp_ref, seg_ref, dt_ref, cd_ref, dskip_ref, sel_ref, expand_ref,
                y_ref, hb_ref, state_ref):
    q = SSD_CHUNK
    n_chunks = xs_ref.shape[0] // q

    def transposed_b(r0):
        return b_ref[pl.ds(r0, q), :].astype(F32).T.astype(BF16)

    def expand_heads(table, which, direction):
        return jnp.dot(table, expand_ref[2 * which + direction], preferred_element_type=F32)

    state_ref[...] = jnp.zeros_like(state_ref)

    n_batch = math.gcd(n_chunks, SSD_BWD_CHUNKS_PER_STEP)

    def bwd_body(i, carry):
        chunks = [n_chunks - 1 - (n_batch * i + k) for k in range(n_batch)]
        starts = [pl.multiple_of(c * q, q) for c in chunks]
        weights = [expand_heads(exp_ref[pl.ds(r0, q), :], EXPAND_WST, 1) for r0 in starts]
        xdws = [(xs_ref[pl.ds(r0, q), :].astype(F32) * w).astype(BF16) for r0, w in zip(starts, weights)]
        contribs = [jnp.dot(transposed_b(r0), xdw, preferred_element_type=F32)
                    for r0, xdw in zip(starts, xdws)]
        state = state_ref[...]
        for c, contrib in zip(chunks, contribs):
            hb_ref[c] = state.astype(hb_ref.dtype)
            state = state * cd_ref[pl.ds(n_chunks + c, 1), :] + contrib
        state_ref[...] = state
        return carry

    lax.fori_loop(0, n_chunks // n_batch, bwd_body, 0)

    state_ref[...] = jnp.zeros_like(state_ref)
    li = lax.broadcasted_iota(jnp.int32, (q, 2 * q), 0)
    si = lax.broadcasted_iota(jnp.int32, (q, 2 * q), 1) % q
    lower, upper = li >= si, li <= si
    lane_low = lax.broadcasted_iota(jnp.int32, (q, V7X_LANES), 1) < SSD_HEAD_DIM

    def fwd_body(c, carry):
        r0 = pl.multiple_of(c * q, q)
        x_bf = xs_ref[pl.ds(r0, q), :]
        x = x_bf.astype(F32)
        bm = b_ref[pl.ds(r0, q), :]
        cm = c_ref[pl.ds(r0, q), :]
        acs_tab = acs_ref[pl.ds(r0, q), :]
        exp_tab = exp_ref[pl.ds(r0, q), :]
        dts = dt_ref[:, pl.ds(r0, q)]
        cb = lax.dot_general(cm, bm, (((1,), (1,)), ((), ())), preferred_element_type=F32)
        cb2 = jnp.concatenate([cb, cb], axis=1)

        def pair_exponents(pair, d):
            rhs = jnp.concatenate([sel_ref[2 * pair + d], seg_ref[c, 2 * pair + d]], axis=0)
            return jnp.dot(acs_tab, rhs, preferred_element_type=F32)

        def pair_dt(pair, d):
            row = R * d + 2 * pair
            return jnp.concatenate([dts[row:row + 1, :], dts[row + 1:row + 2, :]], axis=1)

        segs = [(pair_exponents(pair, 0), pair_exponents(pair, 1)) for pair in range(N_PAIRS)]
        h_prev = state_ref[...]
        off_f = jnp.dot(cm, h_prev.astype(BF16), preferred_element_type=F32)
        off_b = jnp.dot(cm, hb_ref[c], preferred_element_type=F32)
        ef_f = expand_heads(exp_tab, EXPAND_EF, 0)
        ef_b = expand_heads(exp_tab, EXPAND_EF, 1)
        wst_f = expand_heads(exp_tab, EXPAND_WST, 0)

        diag = []
        for pair in range(N_PAIRS):
            xp = x_bf[:, pair * V7X_LANES:(pair + 1) * V7X_LANES]
            zero = jnp.zeros_like(xp)
            rhs = jnp.concatenate([jnp.where(lane_low, xp, zero), jnp.where(lane_low, zero, xp)], axis=0)
            lf = jnp.where(lower, jnp.exp(segs[pair][0]), 0.0) * pair_dt(pair, 0)
            lb = jnp.where(upper, jnp.exp(segs[pair][1]), 0.0) * pair_dt(pair, 1)
            lhs = (cb2 * (lf + lb)).astype(BF16)
            diag.append(jnp.dot(lhs, rhs, preferred_element_type=F32))
        y = jnp.concatenate(diag, axis=1)
        y = y + off_f * ef_f + off_b * ef_b + x * dskip_ref[...]
        y_ref[pl.ds(r0, q), :] = y.astype(y_ref.dtype)

        xdw = (x * wst_f).astype(BF16)
        contrib = jnp.dot(transposed_b(r0), xdw, preferred_element_type=F32)
        state_ref[...] = h_prev * cd_ref[pl.ds(c, 1), :] + contrib
        return carry

    lax.fori_loop(0, n_chunks, fwd_body, 0, unroll=math.gcd(n_chunks, SSD_FWD_UNROLL))


def _ssd_scan(xbc_c, acs_tab, exp_tab, seg_rows, dt_rows, cd, dskip):
    b, s, _ = xbc_c.shape
    n_chunks = s // SSD_CHUNK
    gw = SSD_GROUP_WIDTH
    b_first = SSD_D_INNER // SSD_STATE
    c_first = b_first + SSD_GROUPS
    sel, expand = _ssd_constants()
    seg_block = seg_rows.shape[2:]
    est = (2 * s * gw * 2 * 2 + 4 * s * SSD_STATE * 2 + 4 * s * V7X_LANES * 2 + 2 * 2 * R * s * 4
           + 2 * int(np.prod(seg_block)) * 2 + 2 * (sel.size + expand.size) * 2
           + n_chunks * SSD_STATE * gw * 2 + 4 * n_chunks * gw * 4 + 64 * SSD_CHUNK * gw * 4)
    return pl.pallas_call(
        _ssd_kernel,
        grid=(b, SSD_GROUPS),
        in_specs=[
            pl.BlockSpec((None, s, gw), lambda bi, g: (bi, 0, g)),
            pl.BlockSpec((None, s, SSD_STATE), lambda bi, g: (bi, 0, b_first + g)),
            pl.BlockSpec((None, s, SSD_STATE), lambda bi, g: (bi, 0, c_first + g)),
            pl.BlockSpec((None, None, s, ACS_TABLE_WIDTH), lambda bi, g: (bi, g, 0, 0)),
            pl.BlockSpec((None, None, s, V7X_LANES), lambda bi, g: (bi, g, 0, 0)),
            pl.BlockSpec((None, None) + seg_block, lambda bi, g: (bi, g, 0, 0, 0, 0)),
            pl.BlockSpec((None, None, 2 * R, s), lambda bi, g: (bi, g, 0, 0)),
            pl.BlockSpec((None, None, 2 * n_chunks, gw), lambda bi, g: (bi, g, 0, 0)),
            pl.BlockSpec((None, 1, gw), lambda bi, g: (g, 0, 0)),
            pl.BlockSpec(sel.shape, lambda bi, g: (0, 0, 0)),
            pl.BlockSpec(expand.shape, lambda bi, g: (0, 0, 0)),
        ],
        out_specs=pl.BlockSpec((None, s, gw), lambda bi, g: (bi, 0, g)),
        out_shape=jax.ShapeDtypeStruct((b, s, SSD_D_INNER), BF16),
        scratch_shapes=[
            pltpu.VMEM((n_chunks, SSD_STATE, gw), BF16),
            pltpu.VMEM((SSD_STATE, gw), F32),
        ],
        compiler_params=_cparams(("parallel", "parallel"), est),
        name="ssd_scan",
    )(xbc_c, xbc_c, xbc_c, acs_tab, exp_tab, seg_rows, dt_rows, cd, dskip, sel, expand)


def _mix_kernel(o_ref, y_ref, z_ref, nw_ref, wa_ref, ws_ref, ga_ref, gs_ref, m_ref, yn_ref):
    @pl.when(pl.program_id(1) == 0)
    def _():
        yz = y_ref[...].astype(F32) * _silu(z_ref[...].astype(F32))
        yn_ref[...] = (yz * _rms_scale(yz) * nw_ref[...]).astype(yn_ref.dtype)

    attn = jnp.dot(o_ref[...], wa_ref[...], preferred_element_type=F32)
    ssd = jnp.dot(yn_ref[...], ws_ref[...], preferred_element_type=F32)
    mixed = _sigmoid(ga_ref[...].astype(F32)) * attn + _sigmoid(gs_ref[...].astype(F32)) * ssd
    m_ref[...] = mixed.astype(m_ref.dtype)


def _mix(o2, y2, u2, norm_w, wa, ws, tm, tn):
    t = o2.shape[0]
    tm = min(tm, t)
    d_attn, d_ssd = o2.shape[1], y2.shape[1]
    est = (2 * tm * d_attn * 2 + 4 * tm * d_ssd * 2 + tm * d_ssd * 2 + 2 * (d_attn + d_ssd) * tn * 2
           + 6 * tm * tn * 2 + 4 * tm * tn * 4)
    return pl.pallas_call(
        _mix_kernel,
        grid=(t // tm, D_MODEL // tn),
        in_specs=[
            pl.BlockSpec((tm, d_attn), lambda i, j: (i, 0)),
            pl.BlockSpec((tm, d_ssd), lambda i, j: (i, 0)),
            pl.BlockSpec((tm, d_ssd), lambda i, j: (i, U_Z // SSD_D_INNER)),
            pl.BlockSpec((1, d_ssd), lambda i, j: (0, 0)),
            pl.BlockSpec((d_attn, tn), lambda i, j: (0, j)),
            pl.BlockSpec((d_ssd, tn), lambda i, j: (0, j)),
            pl.BlockSpec((tm, tn), lambda i, j: (i, U_GA // tn + j)),
            pl.BlockSpec((tm, tn), lambda i, j: (i, U_GS // tn + j)),
        ],
        out_specs=pl.BlockSpec((tm, tn), lambda i, j: (i, j)),
        out_shape=jax.ShapeDtypeStruct((t, D_MODEL), BF16),
        scratch_shapes=[pltpu.VMEM((tm, d_ssd), BF16)],
        compiler_params=_cparams(("parallel", "arbitrary"), est),
        name="branch_mix",
    )(o2, y2, u2, norm_w.reshape(1, -1).astype(F32), wa, ws, u2, u2)


def _residual_matmul_kernel(a_ref, w_ref, x_ref, o_ref):
    o_ref[...] = x_ref[...] + jnp.dot(a_ref[...], w_ref[...], preferred_element_type=F32)


def _residual_matmul(a, w, x, tm, tn):
    t, k = a.shape
    n = w.shape[1]
    tm, tn = min(tm, t), min(tn, n)
    est = 2 * tm * k * 2 + 2 * k * tn * 2 + 4 * tm * tn * 4
    return pl.pallas_call(
        _residual_matmul_kernel,
        grid=(t // tm, n // tn),
        in_specs=[
            pl.BlockSpec((tm, k), lambda i, j: (i, 0)),
            pl.BlockSpec((k, tn), lambda i, j: (0, j)),
            pl.BlockSpec((tm, tn), lambda i, j: (i, j)),
        ],
        out_specs=pl.BlockSpec((tm, tn), lambda i, j: (i, j)),
        out_shape=jax.ShapeDtypeStruct((t, n), F32),
        compiler_params=_cparams(("parallel", "parallel"), est),
        name="residual_matmul",
    )(a, w, x)


def _ffn_kernel(prev_ref, main_ref, next_ref, g_ref, wg_ref, wv_ref, cwg_ref, cwv_ref, cbg_ref, cbv_ref,
                wd_ref, gf_ref, o_ref, hn_ref):
    j, n_j = pl.program_id(2), pl.num_programs(2)
    tm = main_ref.shape[0]

    @pl.when(j == 0)
    def _():
        ext = _with_halo(prev_ref[...], main_ref[...], next_ref[...])
        hn_ref[...] = (ext * _rms_scale(ext) * g_ref[...]).astype(hn_ref.dtype)
        o_ref[...] = main_ref[...]

    hn = hn_ref[...]
    gate = _conv_rows(jnp.dot(hn, wg_ref[...], preferred_element_type=F32), cwg_ref[...], cbg_ref[...], tm)
    val = _conv_rows(jnp.dot(hn, wv_ref[...], preferred_element_type=F32), cwv_ref[...], cbv_ref[...], tm)
    act = (_silu(gate) * val).astype(BF16)
    o_ref[...] += jnp.dot(act, wd_ref[...], preferred_element_type=F32)

    @pl.when(j == n_j - 1)
    def _():
        h = o_ref[...]
        o_ref[...] = h * _rms_scale(h) * gf_ref[...]


def _ffn(h3, gain, w_up, conv_w, conv_b, w_down, final_gain, tm, tf):
    b, s, d = h3.shape
    tm = min(tm, s)
    n_f = FFN_DIM // tf
    row = lambda bi, i, j: (0, 0)
    est = (2 * (tm + 2 * HALO) * d * 4 + (tm + 2 * HALO) * d * 2 + 2 * 2 * d * tf * 2 + 2 * tf * d * 2
           + 2 * tm * d * 4 + 8 * (tm + 2 * HALO) * tf * 4)
    return pl.pallas_call(
        _ffn_kernel,
        grid=(b, s // tm, n_f),
        in_specs=_halo_specs(s, tm, d, lambda j: 0) + [
            pl.BlockSpec((1, d), row),
            pl.BlockSpec((d, tf), lambda bi, i, j: (0, j)),
            pl.BlockSpec((d, tf), lambda bi, i, j: (0, n_f + j)),
            pl.BlockSpec((FFN_CONV, tf), lambda bi, i, j: (0, j)),
            pl.BlockSpec((FFN_CONV, tf), lambda bi, i, j: (0, n_f + j)),
            pl.BlockSpec((1, tf), lambda bi, i, j: (0, j)),
            pl.BlockSpec((1, tf), lambda bi, i, j: (0, n_f + j)),
            pl.BlockSpec((tf, d), lambda bi, i, j: (j, 0)),
            pl.BlockSpec((1, d), row),
        ],
        out_specs=pl.BlockSpec((None, tm, d), lambda bi, i, j: (bi, i, 0)),
        out_shape=jax.ShapeDtypeStruct((b, s, d), F32),
        scratch_shapes=[pltpu.VMEM((tm + 2 * HALO, d), BF16)],
        compiler_params=_cparams(("parallel", "parallel", "arbitrary"), est),
        name="conv_ffn",
    )(h3, h3, h3, gain.reshape(1, -1).astype(F32), w_up, w_up, conv_w.astype(F32), conv_w.astype(F32),
      conv_b.reshape(1, -1).astype(F32), conv_b.reshape(1, -1).astype(F32), w_down,
      final_gain.reshape(1, -1).astype(F32))


def _swap_halves(w):
    half = w.shape[-1] // 2
    return jnp.concatenate([w[..., half:], w[..., :half]], axis=-1)


def _layer(h, cos_t, sin_t, norm_mix_w, w_in, q_norm_w, w_uq, kv_norm_w, w_ukv, w_o_attn, ssd_conv_w,
           ssd_conv_b, a_log_fwd, a_log_bwd, dt_bias_fwd, dt_bias_bwd, ssd_d, ssd_norm_w, w_o_ssd, w_out):
    b, s, d = h.shape
    t = b * s
    n_chunks = s // SSD_CHUNK
    x2 = h.reshape(t, d)

    splits = np.cumsum([Q_LORA_RANK, KV_LORA_RANK, QK_ROPE_DIM, SSD_D_INNER, SSD_CONV_DIM, SSD_HEADS,
                        SSD_HEADS, D_MODEL])
    w_ql, w_kvl, w_kr, w_z, w_xbc, w_dtf, w_dtb, w_ga, w_gs = jnp.split(w_in, [int(v) for v in splits], axis=1)
    w_main = jnp.concatenate([w_z, w_xbc, w_ga, w_gs, w_ql, w_kvl], axis=1).astype(BF16)
    w_misc = jnp.concatenate([w_kr, _swap_halves(w_kr), w_dtf, w_dtb], axis=1).astype(BF16)
    u2 = _norm_matmul(x2, norm_mix_w, w_main, BF16, tm=1024, tn=512)
    misc2 = _norm_matmul(x2, norm_mix_w, w_misc, F32, tm=1024, tn=2 * V7X_LANES)
    u3 = u2.reshape(b, s, U_WIDTH)
    misc3 = misc2.reshape(b, s, 2 * V7X_LANES)

    wq = w_uq.reshape(Q_LORA_RANK, MLA_HEADS, QK_NOPE_DIM + QK_ROPE_DIM)
    wq_rope = wq[..., QK_NOPE_DIM:]
    wq = jnp.concatenate([wq[..., :QK_NOPE_DIM], wq_rope, _swap_halves(wq_rope)], axis=-1)
    wq = wq.transpose(1, 0, 2).astype(BF16)
    wkv = w_ukv.reshape(KV_LORA_RANK, MLA_HEADS, QK_NOPE_DIM + V_HEAD_DIM).transpose(1, 0, 2).astype(BF16)
    q, k, vt = _mla_projections(u3, misc3, cos_t, sin_t, q_norm_w, wq, kv_norm_w, wkv, tm=512)
    o = _attention(q, k, vt, tq=1024, tk=1024)

    xbc_c = _ssd_conv(u3, ssd_conv_w, ssd_conv_b, tm=512, tc=512)
    dt_rows = misc3[:, :, V7X_LANES:].transpose(0, 2, 1)
    a_log = jnp.concatenate([a_log_fwd, a_log_bwd]).astype(F32)
    dt_bias = jnp.concatenate([dt_bias_fwd, dt_bias_bwd]).astype(F32)
    col9, cd, dt_r, nacs3 = _dt_prep(misc3, dt_rows, a_log, dt_bias)
    g_, q_ = SSD_GROUPS, SSD_CHUNK
    cols = col9.reshape(b, 3, PIECES, s, 2, g_, R).transpose(0, 5, 3, 1, 2, 4, 6)
    acs_tab = jnp.concatenate([cols[:, :, :, 0].reshape(b, g_, s, ACS_LANES),
                               jnp.ones((b, g_, s, PIECES), BF16),
                               jnp.zeros((b, g_, s, ACS_TABLE_WIDTH - ACS_LANES - PIECES), BF16)], axis=-1)
    exp_tab = jnp.concatenate([cols[:, :, :, 1:].reshape(b, g_, s, 2 * ACS_LANES),
                               jnp.zeros((b, g_, s, V7X_LANES - 2 * ACS_LANES), BF16)], axis=-1)
    seg_rows = nacs3.reshape(b, PIECES, 2, g_, N_PAIRS, 2, n_chunks, q_).transpose(0, 3, 6, 4, 2, 1, 5, 7)
    seg_rows = seg_rows.reshape(b, g_, n_chunks, 2 * N_PAIRS, PIECES, 2 * q_)
    seg_rows = jnp.pad(seg_rows, ((0, 0),) * 4 + ((0, V7X_BF16_SUBLANES - PIECES), (0, 0)))
    dt_heads = dt_r.reshape(b, 2, g_, R, s).transpose(0, 2, 1, 3, 4).reshape(b, g_, 2 * R, s)
    cd = jnp.repeat(cd.reshape(b, n_chunks, 2, g_, R), SSD_HEAD_DIM, axis=-1)
    cd = cd.transpose(0, 3, 2, 1, 4).reshape(b, g_, 2 * n_chunks, SSD_GROUP_WIDTH)
    dskip = jnp.repeat(ssd_d.astype(F32), SSD_HEAD_DIM).reshape(g_, 1, SSD_GROUP_WIDTH)
    y = _ssd_scan(xbc_c, acs_tab, exp_tab, seg_rows, dt_heads, cd, dskip)

    mixed = _mix(o.reshape(t, -1), y.reshape(t, -1), u2, ssd_norm_w, w_o_attn.astype(BF16),
                 w_o_ssd.astype(BF16), tm=512, tn=512)
    h1 = _residual_matmul(mixed, w_out.astype(BF16), x2, tm=1024, tn=1024)
    return h1.reshape(b, s, d)


def kernel(x, positions, norm_mix_w, w_in, q_norm_w, w_uq, kv_norm_w, w_ukv, w_o_attn, ssd_conv_w, ssd_conv_b,
           a_log_fwd, a_log_bwd, dt_bias_fwd, dt_bias_bwd, ssd_d, ssd_norm_w, w_o_ssd, w_out, norm_ffn_w,
           ffn_w_up, ffn_conv_w, ffn_conv_b, ffn_w_down, norm_final_w):
    depth = w_in.shape[0]
    assert depth == 1, "the conv-gated MLP kernel fuses the final RMSNorm, which is only valid for one layer"
    half = QK_ROPE_DIM // 2
    inv_freq = ROPE_THETA ** (-jnp.arange(half, dtype=F32) / half)
    ang = positions.astype(F32)[..., None] * inv_freq
    cos, sin = jnp.cos(ang), jnp.sin(ang)
    pad = jnp.zeros(cos.shape[:-1] + (V7X_LANES - QK_ROPE_DIM,), F32)
    cos_t = jnp.concatenate([cos, cos, pad], axis=-1)
    sin_t = jnp.concatenate([-sin, sin, pad], axis=-1)

    l = 0
    h1 = _layer(x, cos_t, sin_t, norm_mix_w[l], w_in[l], q_norm_w[l], w_uq[l], kv_norm_w[l], w_ukv[l],
                w_o_attn[l], ssd_conv_w[l], ssd_conv_b[l], a_log_fwd[l], a_log_bwd[l], dt_bias_fwd[l],
                dt_bias_bwd[l], ssd_d[l], ssd_norm_w[l], w_o_ssd[l], w_out[l])
    return _ffn(h1, norm_ffn_w[l], ffn_w_up[l].astype(BF16), ffn_conv_w[l], ffn_conv_b[l],
                ffn_w_down[l].astype(BF16), norm_final_w, tm=512, tf=512)
```

```python
import functools
import math

import jax
import jax.numpy as jnp
import numpy as np
from jax import lax
from jax.experimental import pallas as pl
from jax.experimental.pallas import tpu as pltpu

D_MODEL = 2048
MLA_HEADS = 16
Q_LORA_RANK = 512
KV_LORA_RANK = 512
QK_NOPE_DIM = 128
QK_ROPE_DIM = 64
V_HEAD_DIM = 128
ROPE_THETA = 10000.0
SSD_D_INNER = 2 * D_MODEL
SSD_HEAD_DIM = 64
SSD_HEADS = SSD_D_INNER // SSD_HEAD_DIM
SSD_GROUPS = 8
SSD_HEADS_PER_GROUP = SSD_HEADS // SSD_GROUPS
SSD_STATE = 128
SSD_CONV = 5
SSD_CHUNK = 128
SSD_CONV_DIM = SSD_D_INNER + 2 * SSD_GROUPS * SSD_STATE
SSD_GROUP_WIDTH = SSD_HEADS_PER_GROUP * SSD_HEAD_DIM
FFN_DIM = 5632
FFN_CONV = 3
EPS = 1e-6

V7X_LANES = 128
V7X_BF16_SUBLANES = 16
V7X_VMEM_BYTES = 64 * 1024 * 1024
VMEM_LIMIT_CAP = V7X_VMEM_BYTES - 8 * 1024 * 1024

F32 = jnp.float32
BF16 = jnp.bfloat16

IN_QL = 0
IN_KVL = IN_QL + Q_LORA_RANK
IN_KR = IN_KVL + KV_LORA_RANK
IN_Z = IN_KR + QK_ROPE_DIM
IN_XBC = IN_Z + SSD_D_INNER
IN_DTF = IN_XBC + SSD_CONV_DIM
IN_DTB = IN_DTF + SSD_HEADS
IN_GA = IN_DTB + SSD_HEADS
IN_GS = IN_GA + D_MODEL
IN_WIDTH = IN_GS + D_MODEL

U_Z = 0
U_XBC = U_Z + SSD_D_INNER
U_GA = U_XBC + SSD_CONV_DIM
U_GS = U_GA + D_MODEL
U_WIDTH = U_GS + D_MODEL
SM_QL = 0
SM_KVL = SM_QL + Q_LORA_RANK
SM_ROPE = SM_KVL + KV_LORA_RANK
SM_DT = SM_ROPE + 2 * QK_ROPE_DIM
SM_WIDTH = SM_DT + 2 * SSD_HEADS


def _cparams(semantics, vmem_estimate_bytes):
    limit = int(min(max(vmem_estimate_bytes * 5 // 4, 16 * 1024 * 1024), VMEM_LIMIT_CAP))
    return pltpu.CompilerParams(dimension_semantics=semantics, vmem_limit_bytes=limit)


def _rms_scale(x):
    return lax.rsqrt(jnp.mean(x * x, axis=-1, keepdims=True) + EPS)


def _silu(x):
    return x * (1.0 / (1.0 + jnp.exp(-x)))


def _sigmoid(x):
    return 1.0 / (1.0 + jnp.exp(-x))


W_PREP_BLOCK = 512


def _w_prep_kernel(t0_ref, t1_ref, t2_ref, t3_ref, t4_ref, o_ref, *, shift):
    w = jnp.concatenate([r[...] for r in (t0_ref, t1_ref, t2_ref, t3_ref, t4_ref)], axis=1)
    o_ref[...] = w[:, shift:shift + o_ref.shape[1]].astype(o_ref.dtype)


def _main_in_weights(w_in):
    k = w_in.shape[0]
    shift = IN_Z % V7X_LANES
    assert shift == IN_GA % V7X_LANES and IN_GA - IN_DTF == V7X_LANES
    n_first = (IN_DTF - IN_Z) // W_PREP_BLOCK
    per = W_PREP_BLOCK // V7X_LANES

    def src_block(j, extra):
        return IN_Z // V7X_LANES + per * j + jnp.where(j >= n_first, 1, 0) + extra

    specs = [pl.BlockSpec((k, V7X_LANES), functools.partial(lambda j, e: (0, src_block(j, e)), e=e))
             for e in range(per + 1)]
    est = 2 * (per + 1) * k * V7X_LANES * 4 + 2 * k * W_PREP_BLOCK * 2 + 2 * k * (per + 1) * V7X_LANES * 4
    return pl.pallas_call(
        functools.partial(_w_prep_kernel, shift=shift),
        grid=(U_WIDTH // W_PREP_BLOCK,),
        in_specs=specs,
        out_specs=pl.BlockSpec((k, W_PREP_BLOCK), lambda j: (0, j)),
        out_shape=jax.ShapeDtypeStruct((k, U_WIDTH), BF16),
        compiler_params=_cparams(("parallel",), est),
        name="w_in_prep",
    )(*([w_in] * (per + 1)))


def _norm_matmul_kernel(x_ref, g_ref, w_ref, o_ref, xn_ref):
    @pl.when(pl.program_id(1) == 0)
    def _():
        x = x_ref[...].astype(F32)
        xn_ref[...] = (x * _rms_scale(x) * g_ref[...]).astype(xn_ref.dtype)

    o_ref[...] = jnp.dot(xn_ref[...], w_ref[...], preferred_element_type=F32).astype(o_ref.dtype)


def _norm_matmul(x, gain, w, out_dtype, tm, tn):
    t, k = x.shape
    n = w.shape[1]
    tm, tn = min(tm, t), min(tn, n)
    est = 2 * tm * k * x.dtype.itemsize + tm * k * 2 + 2 * k * tn * 2 + 2 * tm * tn * 4
    return pl.pallas_call(
        _norm_matmul_kernel,
        grid=(t // tm, n // tn),
        in_specs=[
            pl.BlockSpec((tm, k), lambda i, j: (i, 0)),
            pl.BlockSpec((1, k), lambda i, j: (0, 0)),
            pl.BlockSpec((k, tn), lambda i, j: (0, j)),
        ],
        out_specs=pl.BlockSpec((tm, tn), lambda i, j: (i, j)),
        out_shape=jax.ShapeDtypeStruct((t, n), out_dtype),
        scratch_shapes=[pltpu.VMEM((tm, k), BF16)],
        compiler_params=_cparams(("parallel", "arbitrary"), est),
        name="norm_matmul",
    )(x, gain.reshape(1, k).astype(F32), w)


def _rope_half(y2, cos_t, sin_t):
    return y2 * cos_t + pltpu.roll(y2, QK_ROPE_DIM, 1) * sin_t


def _q_proj_kernel(ql_ref, g_ref, w_ref, cos_ref, sin_ref, o_ref, *, scale):
    x = ql_ref[...].astype(F32)
    xn = (x * _rms_scale(x) * g_ref[...]).astype(BF16)
    cos_t = cos_ref[...] * scale
    sin_t = sin_ref[...] * scale
    for h in range(MLA_HEADS):
        y = jnp.dot(xn, w_ref[h], preferred_element_type=F32)
        o_ref[h, :, :QK_NOPE_DIM] = (y[:, :QK_NOPE_DIM] * scale).astype(o_ref.dtype)
        o_ref[h, :, QK_NOPE_DIM:] = _rope_half(y[:, QK_NOPE_DIM:], cos_t, sin_t).astype(o_ref.dtype)


def _kv_proj_kernel(kvl_ref, g_ref, w_ref, kr_ref, cos_ref, sin_ref, k_ref, vt_ref):
    x = kvl_ref[...].astype(F32)
    xn = (x * _rms_scale(x) * g_ref[...]).astype(BF16)
    roped = _rope_half(kr_ref[...], cos_ref[...], sin_ref[...]).astype(k_ref.dtype)
    for h in range(MLA_HEADS):
        y = jnp.dot(xn, w_ref[h], preferred_element_type=F32)
        k_ref[h, :, :QK_NOPE_DIM] = y[:, :QK_NOPE_DIM].astype(k_ref.dtype)
        k_ref[h, :, QK_NOPE_DIM:] = roped
        vt_ref[h] = y[:, QK_NOPE_DIM:].T.astype(vt_ref.dtype)


def _mla_projections(small3, cos_t, sin_t, q_gain, wq, kv_gain, wkv, tm):
    b, s, _ = small3.shape
    tm = min(tm, s)
    grid = (b, s // tm)
    head_w = 2 * V7X_LANES
    scale = (QK_NOPE_DIM + QK_ROPE_DIM) ** -0.5 * math.log2(math.e)
    table = pl.BlockSpec((None, tm, V7X_LANES), lambda bi, i: (bi, i, 0))
    gain = pl.BlockSpec((1, Q_LORA_RANK), lambda bi, i: (0, 0))
    wspec = pl.BlockSpec((MLA_HEADS, Q_LORA_RANK, head_w), lambda bi, i: (0, 0, 0))
    est = (2 * tm * Q_LORA_RANK * 4 + 2 * MLA_HEADS * Q_LORA_RANK * head_w * 2
           + 2 * MLA_HEADS * tm * (head_w + V_HEAD_DIM) * 2 + 8 * tm * V7X_LANES * 4)
    params = _cparams(("parallel", "parallel"), est)
    q = pl.pallas_call(
        functools.partial(_q_proj_kernel, scale=scale),
        grid=grid,
        in_specs=[
            pl.BlockSpec((None, tm, Q_LORA_RANK), lambda bi, i: (bi, i, SM_QL // Q_LORA_RANK)),
            gain, wspec, table, table,
        ],
        out_specs=pl.BlockSpec((None, MLA_HEADS, tm, head_w), lambda bi, i: (bi, 0, i, 0)),
        out_shape=jax.ShapeDtypeStruct((b, MLA_HEADS, s, head_w), BF16),
        compiler_params=params,
        name="mla_q_proj",
    )(small3, q_gain.reshape(1, -1).astype(F32), wq, cos_t, sin_t)
    k, vt = pl.pallas_call(
        _kv_proj_kernel,
        grid=grid,
        in_specs=[
            pl.BlockSpec((None, tm, KV_LORA_RANK), lambda bi, i: (bi, i, SM_KVL // KV_LORA_RANK)),
            gain, wspec,
            pl.BlockSpec((None, tm, V7X_LANES), lambda bi, i: (bi, i, SM_ROPE // V7X_LANES)),
            table, table,
        ],
        out_specs=[
            pl.BlockSpec((None, MLA_HEADS, tm, head_w), lambda bi, i: (bi, 0, i, 0)),
            pl.BlockSpec((None, MLA_HEADS, V_HEAD_DIM, tm), lambda bi, i: (bi, 0, 0, i)),
        ],
        out_shape=[
            jax.ShapeDtypeStruct((b, MLA_HEADS, s, head_w), BF16),
            jax.ShapeDtypeStruct((b, MLA_HEADS, V_HEAD_DIM, s), BF16),
        ],
        compiler_params=params,
        name="mla_kv_proj",
    )(small3, kv_gain.reshape(1, -1).astype(F32), wkv, small3, cos_t, sin_t)
    return q, k, vt


def _attn_kernel(q_ref, k_ref, vt_ref, o_ref, acc_ref, st0_ref, st1_ref, p0_ref, p1_ref, *, tk):
    q = q_ref[...]
    tq = q.shape[0]
    n_k = k_ref.shape[0] // tk
    st_refs, p_refs = (st0_ref, st1_ref), (p0_ref, p1_ref)

    def scores(j):
        st_refs[j % 2][...] = lax.dot_general(k_ref[j * tk:(j + 1) * tk, :], q, (((1,), (1,)), ((), ())),
                                              preferred_element_type=F32)

    m = jnp.full((1, tq), -jnp.inf, F32)
    l = jnp.zeros((1, tq), F32)
    scores(0)
    for j in range(n_k):
        if j + 1 < n_k:
            scores(j + 1)
        st = st_refs[j % 2][...]
        m_new = jnp.maximum(m, jnp.max(st, axis=0, keepdims=True))
        alpha = jnp.exp2(m - m_new)
        p = jnp.exp2(st - m_new)
        l = alpha * l + jnp.sum(p, axis=0, keepdims=True)
        m = m_new
        p_refs[j % 2][...] = p.astype(BF16)
        pv = jnp.dot(vt_ref[:, j * tk:(j + 1) * tk], p_refs[j % 2][...], preferred_element_type=F32)
        if j == 0:
            acc_ref[...] = pv
        else:
            acc_ref[...] = alpha * acc_ref[...] + pv
    o_ref[...] = (acc_ref[...] * (1.0 / l)).T.astype(o_ref.dtype)


def _attention(q, k, vt, tq, tk):
    b, h, s, dk = q.shape
    tq, tk = min(tq, s), min(tk, s)
    est = (2 * tq * dk * 2 + 2 * s * dk * 2 + 2 * V_HEAD_DIM * s * 2 + 2 * tq * V_HEAD_DIM * 2
           + V_HEAD_DIM * tq * 4 + 2 * tk * tq * (4 + 2) + 4 * tk * tq * 4)
    scratch = [pltpu.VMEM((V_HEAD_DIM, tq), F32), pltpu.VMEM((tk, tq), F32), pltpu.VMEM((tk, tq), F32),
               pltpu.VMEM((tk, tq), BF16), pltpu.VMEM((tk, tq), BF16)]
    return pl.pallas_call(
        functools.partial(_attn_kernel, tk=tk),
        grid=(b, h, s // tq),
        in_specs=[
            pl.BlockSpec((None, None, tq, dk), lambda bi, hi, i: (bi, hi, i, 0)),
            pl.BlockSpec((None, None, s, dk), lambda bi, hi, i: (bi, hi, 0, 0)),
            pl.BlockSpec((None, None, V_HEAD_DIM, s), lambda bi, hi, i: (bi, hi, 0, 0)),
        ],
        out_specs=pl.BlockSpec((None, tq, V_HEAD_DIM), lambda bi, hi, i: (bi, i, hi)),
        out_shape=jax.ShapeDtypeStruct((b, s, h * V_HEAD_DIM), BF16),
        scratch_shapes=scratch,
        compiler_params=_cparams(("parallel", "parallel", "arbitrary"), est),
        name="mla_attention",
    )(q, k, vt)


HALO = V7X_BF16_SUBLANES


def _with_halo(prev, main, nxt):
    i, n = pl.program_id(1), pl.num_programs(1)
    prev = jnp.where(i > 0, prev, jnp.zeros_like(prev))
    nxt = jnp.where(i < n - 1, nxt, jnp.zeros_like(nxt))
    return jnp.concatenate([prev, main, nxt], axis=0)


def _conv_rows(ext, w, bias, rows):
    taps = w.shape[0]
    acc = bias
    for kk in range(taps):
        off = HALO + kk - (taps - 1) // 2
        acc = acc + ext[off:off + rows] * w[kk:kk + 1]
    return acc


def _halo_specs(s, tm, width, col_block):
    per = tm // HALO
    last = s // HALO - 1

    def prev_map(bi, i, *rest):
        return (bi, jnp.maximum(i * per - 1, 0), col_block(*rest))

    def main_map(bi, i, *rest):
        return (bi, i, col_block(*rest))

    def next_map(bi, i, *rest):
        return (bi, jnp.minimum((i + 1) * per, last), col_block(*rest))

    return [
        pl.BlockSpec((None, HALO, width), prev_map),
        pl.BlockSpec((None, tm, width), main_map),
        pl.BlockSpec((None, HALO, width), next_map),
    ]


def _conv_silu_kernel(prev_ref, main_ref, next_ref, w_ref, b_ref, o_ref):
    ext = _with_halo(prev_ref[...], main_ref[...], next_ref[...]).astype(F32)
    o_ref[...] = _silu(_conv_rows(ext, w_ref[...], b_ref[...], o_ref.shape[0])).astype(o_ref.dtype)


def _ssd_conv(u3, conv_w, conv_b, tm, tc):
    b, s, _ = u3.shape
    tm = min(tm, s)
    n_c = SSD_CONV_DIM // tc
    first = U_XBC // tc
    est = 2 * (tm + 2 * HALO) * tc * 2 + 2 * tm * tc * 2 + 8 * (tm + 2 * HALO) * tc * 4
    return pl.pallas_call(
        _conv_silu_kernel,
        grid=(b, s // tm, n_c),
        in_specs=_halo_specs(s, tm, tc, lambda c: first + c) + [
            pl.BlockSpec((SSD_CONV, tc), lambda bi, i, c: (0, c)),
            pl.BlockSpec((1, tc), lambda bi, i, c: (0, c)),
        ],
        out_specs=pl.BlockSpec((None, tm, tc), lambda bi, i, c: (bi, i, c)),
        out_shape=jax.ShapeDtypeStruct((b, s, SSD_CONV_DIM), BF16),
        compiler_params=_cparams(("parallel", "parallel", "parallel"), est),
        name="ssd_conv_silu",
    )(u3, u3, u3, conv_w.astype(F32), conv_b.reshape(1, -1).astype(F32))


def _softplus(x):
    return jnp.maximum(x, 0.0) + jnp.log1p(jnp.exp(-jnp.abs(x)))


def _chunk_cumsums(a, axis, pos_in_chunk, is_fwd):
    n = a.shape[axis]
    cf, cb = a, a
    step = 1
    while step < SSD_CHUNK:
        cf = cf + jnp.where(pos_in_chunk >= step, pltpu.roll(cf, step, axis), 0.0)
        cb = cb + jnp.where(pos_in_chunk < SSD_CHUNK - step, pltpu.roll(cb, n - step, axis), 0.0)
        step *= 2
    return jnp.where(is_fwd, cf, cb)


def _split3(x):
    hi = x.astype(BF16)
    rest = x - hi.astype(F32)
    mid = rest.astype(BF16)
    lo = (rest - mid.astype(F32)).astype(BF16)
    return hi, mid, lo


def _dt_prep_kernel(dtc_ref, dtr_ref, alog_c_ref, bias_c_ref, alog_r_ref, bias_r_ref,
                    col_ref, cd_ref, dt_r_ref, nacs_r_ref):
    s = dtc_ref.shape[0]
    n_chunks = s // SSD_CHUNK
    dt = _softplus(dtc_ref[...] + bias_c_ref[...])
    a = -jnp.exp(alog_c_ref[...]) * dt
    pos = lax.broadcasted_iota(jnp.int32, a.shape, 0) % SSD_CHUNK
    fwd = lax.broadcasted_iota(jnp.int32, a.shape, 1) < SSD_HEADS
    acs = _chunk_cumsums(a, 0, pos, fwd)
    acs3 = acs.reshape(n_chunks, SSD_CHUNK, V7X_LANES)
    fwd3 = lax.broadcasted_iota(jnp.int32, (n_chunks, 1, V7X_LANES), 2) < SSD_HEADS
    end = jnp.where(fwd3, acs3[:, SSD_CHUNK - 1:], acs3[:, :1])
    wst = (dt.reshape(acs3.shape) * jnp.exp(end - acs3)).reshape(acs.shape)
    for qi, table in enumerate((acs, jnp.exp(acs), wst)):
        for pi, piece in enumerate(_split3(table)):
            col_ref[3 * qi + pi] = piece
    cd_ref[...] = jnp.exp(end)
    dt_r = _softplus(dtr_ref[...] + bias_r_ref[...])
    a_r = -jnp.exp(alog_r_ref[...]) * dt_r
    pos_r = lax.broadcasted_iota(jnp.int32, a_r.shape, 1) % SSD_CHUNK
    fwd_r = lax.broadcasted_iota(jnp.int32, a_r.shape, 0) < SSD_HEADS
    dt_r_ref[...] = dt_r
    for pi, piece in enumerate(_split3(-_chunk_cumsums(a_r, 1, pos_r, fwd_r))):
        nacs_r_ref[pi] = piece


def _dt_prep(small3, dt_rows, a_log, dt_bias):
    b, s, _ = small3.shape
    n_chunks = s // SSD_CHUNK
    row = pl.BlockSpec((None, V7X_LANES, s), lambda bi: (bi, 0, 0))
    cvec = pl.BlockSpec((1, V7X_LANES), lambda bi: (0, 0))
    rvec = pl.BlockSpec((V7X_LANES, 1), lambda bi: (0, 0))
    est = 2 * 8 * s * V7X_LANES * 4 + 24 * s * V7X_LANES * 4
    return pl.pallas_call(
        _dt_prep_kernel,
        grid=(b,),
        in_specs=[pl.BlockSpec((None, s, V7X_LANES), lambda bi: (bi, 0, SM_DT // V7X_LANES)),
                  row, cvec, cvec, rvec, rvec],
        out_specs=[
            pl.BlockSpec((None, 9, s, V7X_LANES), lambda bi: (bi, 0, 0, 0)),
            pl.BlockSpec((None, n_chunks, 1, V7X_LANES), lambda bi: (bi, 0, 0, 0)),
            row,
            pl.BlockSpec((None, 3, V7X_LANES, s), lambda bi: (bi, 0, 0, 0)),
        ],
        out_shape=[
            jax.ShapeDtypeStruct((b, 9, s, V7X_LANES), BF16),
            jax.ShapeDtypeStruct((b, n_chunks, 1, V7X_LANES), F32),
            jax.ShapeDtypeStruct((b, V7X_LANES, s), F32),
            jax.ShapeDtypeStruct((b, 3, V7X_LANES, s), BF16),
        ],
        compiler_params=_cparams(("parallel",), est),
        name="ssd_dt_prep",
    )(small3, dt_rows, a_log.reshape(1, -1), dt_bias.reshape(1, -1), a_log.reshape(-1, 1), dt_bias.reshape(-1, 1))


R = SSD_HEADS_PER_GROUP
N_PAIRS = R // 2
PIECES = 3
ACS_LANES = PIECES * 2 * R
ACS_TABLE_WIDTH = 64
EXPAND_EF, EXPAND_WST = 0, 1
SSD_BWD_CHUNKS_PER_STEP = 8
SSD_FWD_UNROLL = 4


def _ssd_constants():
    sel = np.zeros((2 * N_PAIRS, ACS_LANES, 2 * V7X_LANES), np.float32)
    expand = np.zeros((4, V7X_LANES, SSD_GROUP_WIDTH), np.float32)
    for piece in range(PIECES):
        for d in range(2):
            for pair in range(N_PAIRS):
                for half in range(2):
                    lane = 2 * R * piece + R * d + 2 * pair + half
                    sel[2 * pair + d, lane, half * V7X_LANES:(half + 1) * V7X_LANES] = 1.0
            for t in range(2):
                for r in range(R):
                    lane = ACS_LANES * t + 2 * R * piece + R * d + r
                    expand[2 * t + d, lane, r * SSD_HEAD_DIM:(r + 1) * SSD_HEAD_DIM] = 1.0
    return jnp.asarray(sel, BF16), jnp.asarray(expand, BF16)


def _ssd_kernel(xs_ref, b_ref, c_ref, z_ref, nw_ref, acs_ref, exp_ref, seg_ref, dt_ref, cd_ref, dskip_ref, sel_ref,
                expand_ref, gated_ref, ss_ref, hb_ref, state_ref):
    q = SSD_CHUNK
    n_chunks = xs_ref.shape[0] // q

    def transposed_b(r0):
        return b_ref[pl.ds(r0, q), :].astype(F32).T.astype(BF16)

    def expand_heads(table, which, direction):
        return jnp.dot(table, expand_ref[2 * which + direction], preferred_element_type=F32)

    state_ref[...] = jnp.zeros_like(state_ref)

    n_batch = math.gcd(n_chunks, SSD_BWD_CHUNKS_PER_STEP)

    def bwd_body(i, carry):
        chunks = [n_chunks - 1 - (n_batch * i + k) for k in range(n_batch)]
        starts = [pl.multiple_of(c * q, q) for c in chunks]
        weights = [expand_heads(exp_ref[pl.ds(r0, q), :], EXPAND_WST, 1) for r0 in starts]
        xdws = [(xs_ref[pl.ds(r0, q), :].astype(F32) * w).astype(BF16) for r0, w in zip(starts, weights)]
        contribs = [jnp.dot(transposed_b(r0), xdw, preferred_element_type=F32)
                    for r0, xdw in zip(starts, xdws)]
        state = state_ref[...]
        for c, contrib in zip(chunks, contribs):
            hb_ref[c] = state.astype(hb_ref.dtype)
            state = state * cd_ref[pl.ds(n_chunks + c, 1), :] + contrib
        state_ref[...] = state
        return carry

    lax.fori_loop(0, n_chunks // n_batch, bwd_body, 0)

    state_ref[...] = jnp.zeros_like(state_ref)
    li = lax.broadcasted_iota(jnp.int32, (q, 2 * q), 0)
    si = lax.broadcasted_iota(jnp.int32, (q, 2 * q), 1) % q
    lower, upper = li >= si, li <= si
    lane_low = lax.broadcasted_iota(jnp.int32, (q, V7X_LANES), 1) < SSD_HEAD_DIM

    def fwd_body(c, carry):
        r0 = pl.multiple_of(c * q, q)
        x_bf = xs_ref[pl.ds(r0, q), :]
        x = x_bf.astype(F32)
        bm = b_ref[pl.ds(r0, q), :]
        cm = c_ref[pl.ds(r0, q), :]
        acs_tab = acs_ref[pl.ds(r0, q), :]
        exp_tab = exp_ref[pl.ds(r0, q), :]
        dts = dt_ref[:, pl.ds(r0, q)]
        cb = lax.dot_general(cm, bm, (((1,), (1,)), ((), ())), preferred_element_type=F32)
        cb2 = jnp.concatenate([cb, cb], axis=1)

        def pair_exponents(pair, d):
            rhs = jnp.concatenate([sel_ref[2 * pair + d], seg_ref[c, 2 * pair + d]], axis=0)
            return jnp.dot(acs_tab, rhs, preferred_element_type=F32)

        def pair_dt(pair, d):
            row = R * d + 2 * pair
            return jnp.concatenate([dts[row:row + 1, :], dts[row + 1:row + 2, :]], axis=1)

        segs = [(pair_exponents(pair, 0), pair_exponents(pair, 1)) for pair in range(N_PAIRS)]
        h_prev = state_ref[...]
        off_f = jnp.dot(cm, h_prev.astype(BF16), preferred_element_type=F32)
        off_b = jnp.dot(cm, hb_ref[c], preferred_element_type=F32)
        ef_f = expand_heads(exp_tab, EXPAND_EF, 0)
        ef_b = expand_heads(exp_tab, EXPAND_EF, 1)
        wst_f = expand_heads(exp_tab, EXPAND_WST, 0)

        diag = []
        for pair in range(N_PAIRS):
            xp = x_bf[:, pair * V7X_LANES:(pair + 1) * V7X_LANES]
            zero = jnp.zeros_like(xp)
            rhs = jnp.concatenate([jnp.where(lane_low, xp, zero), jnp.where(lane_low, zero, xp)], axis=0)
            lf = jnp.where(lower, jnp.exp(segs[pair][0]), 0.0) * pair_dt(pair, 0)
            lb = jnp.where(upper, jnp.exp(segs[pair][1]), 0.0) * pair_dt(pair, 1)
            lhs = (cb2 * (lf + lb)).astype(BF16)
            diag.append(jnp.dot(lhs, rhs, preferred_element_type=F32))
        y = jnp.concatenate(diag, axis=1)
        y = y + off_f * ef_f + off_b * ef_b + x * dskip_ref[...]
        yz = y * _silu(z_ref[pl.ds(r0, q), :].astype(F32))
        ss_ref[pl.ds(r0, q), :] = jnp.sum(yz * yz, axis=1, keepdims=True)
        gated_ref[pl.ds(r0, q), :] = (yz * nw_ref[...]).astype(gated_ref.dtype)

        xdw = (x * wst_f).astype(BF16)
        contrib = jnp.dot(transposed_b(r0), xdw, preferred_element_type=F32)
        state_ref[...] = h_prev * cd_ref[pl.ds(c, 1), :] + contrib
        return carry

    lax.fori_loop(0, n_chunks, fwd_body, 0, unroll=math.gcd(n_chunks, SSD_FWD_UNROLL))


def _ssd_scan(xbc_c, u3, norm_w, acs_tab, exp_tab, seg_rows, dt_rows, cd, dskip):
    b, s, _ = xbc_c.shape
    n_chunks = s // SSD_CHUNK
    gw = SSD_GROUP_WIDTH
    b_first = SSD_D_INNER // SSD_STATE
    c_first = b_first + SSD_GROUPS
    sel, expand = _ssd_constants()
    seg_block = seg_rows.shape[2:]
    est = (3 * 2 * s * gw * 2 + 2 * s * V7X_LANES * 4 + 4 * s * SSD_STATE * 2 + 4 * s * V7X_LANES * 2
           + 2 * 2 * R * s * 4
           + 2 * int(np.prod(seg_block)) * 2 + 2 * (sel.size + expand.size) * 2
           + n_chunks * SSD_STATE * gw * 2 + 4 * n_chunks * gw * 4 + 64 * SSD_CHUNK * gw * 4)
    return pl.pallas_call(
        _ssd_kernel,
        grid=(b, SSD_GROUPS),
        in_specs=[
            pl.BlockSpec((None, s, gw), lambda bi, g: (bi, 0, g)),
            pl.BlockSpec((None, s, SSD_STATE), lambda bi, g: (bi, 0, b_first + g)),
            pl.BlockSpec((None, s, SSD_STATE), lambda bi, g: (bi, 0, c_first + g)),
            pl.BlockSpec((None, s, gw), lambda bi, g: (bi, 0, U_Z // gw + g)),
            pl.BlockSpec((1, gw), lambda bi, g: (0, g)),
            pl.BlockSpec((None, None, s, ACS_TABLE_WIDTH), lambda bi, g: (bi, g, 0, 0)),
            pl.BlockSpec((None, None, s, V7X_LANES), lambda bi, g: (bi, g, 0, 0)),
            pl.BlockSpec((None, None) + seg_block, lambda bi, g: (bi, g, 0, 0, 0, 0)),
            pl.BlockSpec((None, None, 2 * R, s), lambda bi, g: (bi, g, 0, 0)),
            pl.BlockSpec((None, None, 2 * n_chunks, gw), lambda bi, g: (bi, g, 0, 0)),
            pl.BlockSpec((None, 1, gw), lambda bi, g: (g, 0, 0)),
            pl.BlockSpec(sel.shape, lambda bi, g: (0, 0, 0)),
            pl.BlockSpec(expand.shape, lambda bi, g: (0, 0, 0)),
        ],
        out_specs=[
            pl.BlockSpec((None, s, gw), lambda bi, g: (bi, 0, g)),
            pl.BlockSpec((None, None, s, 1), lambda bi, g: (bi, g, 0, 0)),
        ],
        out_shape=[
            jax.ShapeDtypeStruct((b, s, SSD_D_INNER), BF16),
            jax.ShapeDtypeStruct((b, SSD_GROUPS, s, 1), F32),
        ],
        scratch_shapes=[
            pltpu.VMEM((n_chunks, SSD_STATE, gw), BF16),
            pltpu.VMEM((SSD_STATE, gw), F32),
        ],
        compiler_params=_cparams(("parallel", "parallel"), est),
        name="ssd_scan",
    )(xbc_c, xbc_c, xbc_c, u3, norm_w.reshape(1, -1).astype(F32), acs_tab, exp_tab, seg_rows, dt_rows, cd, dskip,
      sel, expand)


def _mix_kernel(o_ref, gated_ref, ss_ref, wa_ref, ws_ref, ga_ref, gs_ref, m_ref):
    attn = jnp.dot(o_ref[...], wa_ref[...], preferred_element_type=F32)
    ssd = jnp.dot(gated_ref[...], ws_ref[...], preferred_element_type=F32)
    mean_sq = jnp.sum(ss_ref[...], axis=1, keepdims=True) * (1.0 / gated_ref.shape[1])
    ssd = ssd * lax.rsqrt(mean_sq + EPS)
    mixed = _sigmoid(ga_ref[...].astype(F32)) * attn + _sigmoid(gs_ref[...].astype(F32)) * ssd
    m_ref[...] = mixed.astype(m_ref.dtype)


def _mix(o2, gated2, ss2, u2, wa, ws, tm, tn):
    t = o2.shape[0]
    tm = min(tm, t)
    d_attn, d_ssd = o2.shape[1], gated2.shape[1]
    est = (2 * tm * (d_attn + d_ssd) * 2 + 2 * tm * V7X_LANES * 4 + 2 * (d_attn + d_ssd) * tn * 2
           + 6 * tm * tn * 2 + 6 * tm * tn * 4)
    return pl.pallas_call(
        _mix_kernel,
        grid=(t // tm, D_MODEL // tn),
        in_specs=[
            pl.BlockSpec((tm, d_attn), lambda i, j: (i, 0)),
            pl.BlockSpec((tm, d_ssd), lambda i, j: (i, 0)),
            pl.BlockSpec((tm, ss2.shape[1]), lambda i, j: (i, 0)),
            pl.BlockSpec((d_attn, tn), lambda i, j: (0, j)),
            pl.BlockSpec((d_ssd, tn), lambda i, j: (0, j)),
            pl.BlockSpec((tm, tn), lambda i, j: (i, U_GA // tn + j)),
            pl.BlockSpec((tm, tn), lambda i, j: (i, U_GS // tn + j)),
        ],
        out_specs=pl.BlockSpec((tm, tn), lambda i, j: (i, j)),
        out_shape=jax.ShapeDtypeStruct((t, D_MODEL), BF16),
        compiler_params=_cparams(("parallel", "parallel"), est),
        name="branch_mix",
    )(o2, gated2, ss2, wa, ws, u2, u2)


def _residual_matmul_kernel(a_ref, w_ref, x_ref, o_ref):
    o_ref[...] = x_ref[...] + jnp.dot(a_ref[...], w_ref[...], preferred_element_type=F32)


def _residual_matmul(a, w, x, tm, tn):
    t, k = a.shape
    n = w.shape[1]
    tm, tn = min(tm, t), min(tn, n)
    est = 2 * tm * k * 2 + 2 * k * tn * 2 + 4 * tm * tn * 4
    return pl.pallas_call(
        _residual_matmul_kernel,
        grid=(t // tm, n // tn),
        in_specs=[
            pl.BlockSpec((tm, k), lambda i, j: (i, 0)),
            pl.BlockSpec((k, tn), lambda i, j: (0, j)),
            pl.BlockSpec((tm, tn), lambda i, j: (i, j)),
        ],
        out_specs=pl.BlockSpec((tm, tn), lambda i, j: (i, j)),
        out_shape=jax.ShapeDtypeStruct((t, n), F32),
        compiler_params=_cparams(("parallel", "parallel"), est),
        name="residual_matmul",
    )(a, w, x)


def _ffn_kernel(prev_ref, main_ref, next_ref, g_ref, wg_ref, wv_ref, cwg_ref, cwv_ref, cbg_ref, cbv_ref,
                wd_ref, gf_ref, o_ref, hn_ref):
    j, n_j = pl.program_id(2), pl.num_programs(2)
    tm = main_ref.shape[0]

    @pl.when(j == 0)
    def _():
        ext = _with_halo(prev_ref[...], main_ref[...], next_ref[...])
        hn_ref[...] = (ext * _rms_scale(ext) * g_ref[...]).astype(hn_ref.dtype)
        o_ref[...] = main_ref[...]

    hn = hn_ref[...]
    gate = _conv_rows(jnp.dot(hn, wg_ref[...], preferred_element_type=F32), cwg_ref[...], cbg_ref[...], tm)
    val = _conv_rows(jnp.dot(hn, wv_ref[...], preferred_element_type=F32), cwv_ref[...], cbv_ref[...], tm)
    act = (_silu(gate) * val).astype(BF16)
    o_ref[...] += jnp.dot(act, wd_ref[...], preferred_element_type=F32)

    @pl.when(j == n_j - 1)
    def _():
        h = o_ref[...]
        o_ref[...] = h * _rms_scale(h) * gf_ref[...]


def _ffn(h3, gain, w_up, conv_w, conv_b, w_down, final_gain, tm, tf):
    b, s, d = h3.shape
    tm = min(tm, s)
    n_f = FFN_DIM // tf
    row = lambda bi, i, j: (0, 0)
    est = (2 * (tm + 2 * HALO) * d * 4 + (tm + 2 * HALO) * d * 2 + 2 * 2 * d * tf * 2 + 2 * tf * d * 2
           + 2 * tm * d * 4 + 8 * (tm + 2 * HALO) * tf * 4)
    return pl.pallas_call(
        _ffn_kernel,
        grid=(b, s // tm, n_f),
        in_specs=_halo_specs(s, tm, d, lambda j: 0) + [
            pl.BlockSpec((1, d), row),
            pl.BlockSpec((d, tf), lambda bi, i, j: (0, j)),
            pl.BlockSpec((d, tf), lambda bi, i, j: (0, n_f + j)),
            pl.BlockSpec((FFN_CONV, tf), lambda bi, i, j: (0, j)),
            pl.BlockSpec((FFN_CONV, tf), lambda bi, i, j: (0, n_f + j)),
            pl.BlockSpec((1, tf), lambda bi, i, j: (0, j)),
            pl.BlockSpec((1, tf), lambda bi, i, j: (0, n_f + j)),
            pl.BlockSpec((tf, d), lambda bi, i, j: (j, 0)),
            pl.BlockSpec((1, d), row),
        ],
        out_specs=pl.BlockSpec((None, tm, d), lambda bi, i, j: (bi, i, 0)),
        out_shape=jax.ShapeDtypeStruct((b, s, d), F32),
        scratch_shapes=[pltpu.VMEM((tm + 2 * HALO, d), BF16)],
        compiler_params=_cparams(("parallel", "parallel", "arbitrary"), est),
        name="conv_ffn",
    )(h3, h3, h3, gain.reshape(1, -1).astype(F32), w_up, w_up, conv_w.astype(F32), conv_w.astype(F32),
      conv_b.reshape(1, -1).astype(F32), conv_b.reshape(1, -1).astype(F32), w_down,
      final_gain.reshape(1, -1).astype(F32))


def _swap_halves(w):
    half = w.shape[-1] // 2
    return jnp.concatenate([w[..., half:], w[..., :half]], axis=-1)


def _layer(h, cos_t, sin_t, norm_mix_w, w_in, q_norm_w, w_uq, kv_norm_w, w_ukv, w_o_attn, ssd_conv_w,
           ssd_conv_b, a_log_fwd, a_log_bwd, dt_bias_fwd, dt_bias_bwd, ssd_d, ssd_norm_w, w_o_ssd, w_out):
    b, s, d = h.shape
    t = b * s
    n_chunks = s // SSD_CHUNK
    x2 = h.reshape(t, d)

    assert w_in.shape[1] == IN_WIDTH
    w_kr = w_in[:, IN_KR:IN_Z]
    w_small = jnp.concatenate([w_in[:, IN_QL:IN_KR], w_kr, _swap_halves(w_kr), w_in[:, IN_DTF:IN_GA]],
                              axis=1).astype(BF16)
    u2 = _norm_matmul(x2, norm_mix_w, _main_in_weights(w_in), BF16, tm=1024, tn=1024)
    small2 = _norm_matmul(x2, norm_mix_w, w_small, F32, tm=1024, tn=2 * V7X_LANES)
    u3 = u2.reshape(b, s, U_WIDTH)
    small3 = small2.reshape(b, s, SM_WIDTH)

    wq = w_uq.reshape(Q_LORA_RANK, MLA_HEADS, QK_NOPE_DIM + QK_ROPE_DIM)
    wq_rope = wq[..., QK_NOPE_DIM:]
    wq = jnp.concatenate([wq[..., :QK_NOPE_DIM], wq_rope, _swap_halves(wq_rope)], axis=-1)
    wq = wq.transpose(1, 0, 2).astype(BF16)
    wkv = w_ukv.reshape(KV_LORA_RANK, MLA_HEADS, QK_NOPE_DIM + V_HEAD_DIM).transpose(1, 0, 2).astype(BF16)
    q, k, vt = _mla_projections(small3, cos_t, sin_t, q_norm_w, wq, kv_norm_w, wkv, tm=512)
    o = _attention(q, k, vt, tq=1024, tk=1024)

    xbc_c = _ssd_conv(u3, ssd_conv_w, ssd_conv_b, tm=512, tc=512)
    dt_rows = small3[:, :, SM_DT:].transpose(0, 2, 1)
    a_log = jnp.concatenate([a_log_fwd, a_log_bwd]).astype(F32)
    dt_bias = jnp.concatenate([dt_bias_fwd, dt_bias_bwd]).astype(F32)
    col9, cd, dt_r, nacs3 = _dt_prep(small3, dt_rows, a_log, dt_bias)
    g_, q_ = SSD_GROUPS, SSD_CHUNK
    cols = col9.reshape(b, 3, PIECES, s, 2, g_, R).transpose(0, 5, 3, 1, 2, 4, 6)
    acs_tab = jnp.concatenate([cols[:, :, :, 0].reshape(b, g_, s, ACS_LANES),
                               jnp.ones((b, g_, s, PIECES), BF16),
                               jnp.zeros((b, g_, s, ACS_TABLE_WIDTH - ACS_LANES - PIECES), BF16)], axis=-1)
    exp_tab = jnp.concatenate([cols[:, :, :, 1:].reshape(b, g_, s, 2 * ACS_LANES),
                               jnp.zeros((b, g_, s, V7X_LANES - 2 * ACS_LANES), BF16)], axis=-1)
    seg_rows = nacs3.reshape(b, PIECES, 2, g_, N_PAIRS, 2, n_chunks, q_).transpose(0, 3, 6, 4, 2, 1, 5, 7)
    seg_rows = seg_rows.reshape(b, g_, n_chunks, 2 * N_PAIRS, PIECES, 2 * q_)
    seg_rows = jnp.pad(seg_rows, ((0, 0),) * 4 + ((0, V7X_BF16_SUBLANES - PIECES), (0, 0)))
    dt_heads = dt_r.reshape(b, 2, g_, R, s).transpose(0, 2, 1, 3, 4).reshape(b, g_, 2 * R, s)
    cd = jnp.repeat(cd.reshape(b, n_chunks, 2, g_, R), SSD_HEAD_DIM, axis=-1)
    cd = cd.transpose(0, 3, 2, 1, 4).reshape(b, g_, 2 * n_chunks, SSD_GROUP_WIDTH)
    dskip = jnp.repeat(ssd_d.astype(F32), SSD_HEAD_DIM).reshape(g_, 1, SSD_GROUP_WIDTH)
    gated, ss = _ssd_scan(xbc_c, u3, ssd_norm_w, acs_tab, exp_tab, seg_rows, dt_heads, cd, dskip)
    ss2 = ss.reshape(b, g_, s).transpose(0, 2, 1).reshape(t, g_)

    mixed = _mix(o.reshape(t, -1), gated.reshape(t, -1), ss2, u2, w_o_attn.astype(BF16),
                 w_o_ssd.astype(BF16), tm=1024, tn=512)
    h1 = _residual_matmul(mixed, w_out.astype(BF16), x2, tm=1024, tn=1024)
    return h1.reshape(b, s, d)


def kernel(x, positions, norm_mix_w, w_in, q_norm_w, w_uq, kv_norm_w, w_ukv, w_o_attn, ssd_conv_w, ssd_conv_b,
           a_log_fwd, a_log_bwd, dt_bias_fwd, dt_bias_bwd, ssd_d, ssd_norm_w, w_o_ssd, w_out, norm_ffn_w,
           ffn_w_up, ffn_conv_w, ffn_conv_b, ffn_w_down, norm_final_w):
    depth = w_in.shape[0]
    assert depth == 1, "the conv-gated MLP kernel fuses the final RMSNorm, which is only valid for one layer"
    half = QK_ROPE_DIM // 2
    inv_freq = ROPE_THETA ** (-jnp.arange(half, dtype=F32) / half)
    ang = positions.astype(F32)[..., None] * inv_freq
    cos, sin = jnp.cos(ang), jnp.sin(ang)
    pad = jnp.zeros(cos.shape[:-1] + (V7X_LANES - QK_ROPE_DIM,), F32)
    cos_t = jnp.concatenate([cos, cos, pad], axis=-1)
    sin_t = jnp.concatenate([-sin, sin, pad], axis=-1)

    l = 0
    h1 = _layer(x, cos_t, sin_t, norm_mix_w[l], w_in[l], q_norm_w[l], w_uq[l], kv_norm_w[l], w_ukv[l],
                w_o_attn[l], ssd_conv_w[l], ssd_conv_b[l], a_log_fwd[l], a_log_bwd[l], dt_bias_fwd[l],
                dt_bias_bwd[l], ssd_d[l], ssd_norm_w[l], w_o_ssd[l], w_out[l])
    return _ffn(h1, norm_ffn_w[l], ffn_w_up[l].astype(BF16), ffn_conv_w[l], ffn_conv_b[l],
                ffn_w_down[l].astype(BF16), norm_final_w, tm=512, tf=512)
```

```python
import functools
import math

import jax
import jax.numpy as jnp
import numpy as np
from jax import lax
from jax.experimental import pallas as pl
from jax.experimental.pallas import tpu as pltpu

D_MODEL = 2048
MLA_HEADS = 16
Q_LORA_RANK = 512
KV_LORA_RANK = 512
QK_NOPE_DIM = 128
QK_ROPE_DIM = 64
V_HEAD_DIM = 128
ROPE_THETA = 10000.0
SSD_D_INNER = 2 * D_MODEL
SSD_HEAD_DIM = 64
SSD_HEADS = SSD_D_INNER // SSD_HEAD_DIM
SSD_GROUPS = 8
SSD_HEADS_PER_GROUP = SSD_HEADS // SSD_GROUPS
SSD_STATE = 128
SSD_CONV = 5
SSD_CHUNK = 128
SSD_CONV_DIM = SSD_D_INNER + 2 * SSD_GROUPS * SSD_STATE
SSD_GROUP_WIDTH = SSD_HEADS_PER_GROUP * SSD_HEAD_DIM
FFN_DIM = 5632
FFN_CONV = 3
EPS = 1e-6

V7X_LANES = 128
V7X_BF16_SUBLANES = 16
V7X_VMEM_BYTES = 64 * 1024 * 1024
VMEM_LIMIT_CAP = V7X_VMEM_BYTES - 8 * 1024 * 1024

F32 = jnp.float32
BF16 = jnp.bfloat16

IN_QL = 0
IN_KVL = IN_QL + Q_LORA_RANK
IN_KR = IN_KVL + KV_LORA_RANK
IN_Z = IN_KR + QK_ROPE_DIM
IN_XBC = IN_Z + SSD_D_INNER
IN_DTF = IN_XBC + SSD_CONV_DIM
IN_DTB = IN_DTF + SSD_HEADS
IN_GA = IN_DTB + SSD_HEADS
IN_GS = IN_GA + D_MODEL
IN_WIDTH = IN_GS + D_MODEL

U_Z = 0
U_XBC = U_Z + SSD_D_INNER
U_GA = U_XBC + SSD_CONV_DIM
U_GS = U_GA + D_MODEL
U_WIDTH = U_GS + D_MODEL
SM_QL = 0
SM_KVL = SM_QL + Q_LORA_RANK
SM_ROPE = SM_KVL + KV_LORA_RANK
SM_DT = SM_ROPE + 2 * QK_ROPE_DIM
SM_WIDTH = SM_DT + 2 * SSD_HEADS


def _cparams(semantics, vmem_estimate_bytes):
    limit = int(min(max(vmem_estimate_bytes * 5 // 4, 16 * 1024 * 1024), VMEM_LIMIT_CAP))
    return pltpu.CompilerParams(dimension_semantics=semantics, vmem_limit_bytes=limit)


def _rms_scale(x):
    return lax.rsqrt(jnp.mean(x * x, axis=-1, keepdims=True) + EPS)


def _silu(x):
    return x * (1.0 / (1.0 + jnp.exp(-x)))


def _sigmoid(x):
    return 1.0 / (1.0 + jnp.exp(-x))


W_PREP_BLOCK = 512


def _w_prep_kernel(t0_ref, t1_ref, t2_ref, t3_ref, t4_ref, o_ref, *, shift):
    w = jnp.concatenate([r[...] for r in (t0_ref, t1_ref, t2_ref, t3_ref, t4_ref)], axis=1)
    o_ref[...] = w[:, shift:shift + o_ref.shape[1]].astype(o_ref.dtype)


def _main_in_weights(w_in_stack, layer):
    _, k, _ = w_in_stack.shape
    shift = IN_Z % V7X_LANES
    assert shift == IN_GA % V7X_LANES and IN_GA - IN_DTF == V7X_LANES
    n_first = (IN_DTF - IN_Z) // W_PREP_BLOCK
    per = W_PREP_BLOCK // V7X_LANES

    def src_block(j, extra):
        return IN_Z // V7X_LANES + per * j + jnp.where(j >= n_first, 1, 0) + extra

    specs = [pl.BlockSpec((None, k, V7X_LANES), functools.partial(lambda j, e: (layer, 0, src_block(j, e)), e=e))
             for e in range(per + 1)]
    est = 2 * (per + 1) * k * V7X_LANES * 4 + 2 * k * W_PREP_BLOCK * 2 + 2 * k * (per + 1) * V7X_LANES * 4
    return pl.pallas_call(
        functools.partial(_w_prep_kernel, shift=shift),
        grid=(U_WIDTH // W_PREP_BLOCK,),
        in_specs=specs,
        out_specs=pl.BlockSpec((k, W_PREP_BLOCK), lambda j: (0, j)),
        out_shape=jax.ShapeDtypeStruct((k, U_WIDTH), BF16),
        compiler_params=_cparams(("parallel",), est),
        name="w_in_prep",
    )(*([w_in_stack] * (per + 1)))


def _norm_matmul_kernel(x_ref, g_ref, w_ref, o_ref, xn_ref):
    @pl.when(pl.program_id(1) == 0)
    def _():
        x = x_ref[...].astype(F32)
        xn_ref[...] = (x * _rms_scale(x) * g_ref[...]).astype(xn_ref.dtype)

    o_ref[...] = jnp.dot(xn_ref[...], w_ref[...], preferred_element_type=F32).astype(o_ref.dtype)


def _norm_matmul(x, gain, w, out_dtype, tm, tn):
    t, k = x.shape
    n = w.shape[1]
    tm, tn = min(tm, t), min(tn, n)
    est = 2 * tm * k * x.dtype.itemsize + tm * k * 2 + 2 * k * tn * 2 + 2 * tm * tn * 4
    return pl.pallas_call(
        _norm_matmul_kernel,
        grid=(t // tm, n // tn),
        in_specs=[
            pl.BlockSpec((tm, k), lambda i, j: (i, 0)),
            pl.BlockSpec((1, k), lambda i, j: (0, 0)),
            pl.BlockSpec((k, tn), lambda i, j: (0, j)),
        ],
        out_specs=pl.BlockSpec((tm, tn), lambda i, j: (i, j)),
        out_shape=jax.ShapeDtypeStruct((t, n), out_dtype),
        scratch_shapes=[pltpu.VMEM((tm, k), BF16)],
        compiler_params=_cparams(("parallel", "arbitrary"), est),
        name="norm_matmul",
    )(x, gain.reshape(1, k).astype(F32), w)


def _rope_half(y2, cos_t, sin_t):
    return y2 * cos_t + pltpu.roll(y2, QK_ROPE_DIM, 1) * sin_t


def _q_proj_kernel(ql_ref, g_ref, w_ref, cos_ref, sin_ref, o_ref, *, scale):
    x = ql_ref[...].astype(F32)
    xn = (x * _rms_scale(x) * g_ref[...]).astype(BF16)
    cos_t = cos_ref[...] * scale
    sin_t = sin_ref[...] * scale
    for h in range(MLA_HEADS):
        y = jnp.dot(xn, w_ref[h], preferred_element_type=F32)
        o_ref[h, :, :QK_NOPE_DIM] = (y[:, :QK_NOPE_DIM] * scale).astype(o_ref.dtype)
        o_ref[h, :, QK_NOPE_DIM:] = _rope_half(y[:, QK_NOPE_DIM:], cos_t, sin_t).astype(o_ref.dtype)


def _kv_proj_kernel(kvl_ref, g_ref, w_ref, kr_ref, cos_ref, sin_ref, k_ref, vt_ref):
    x = kvl_ref[...].astype(F32)
    xn = (x * _rms_scale(x) * g_ref[...]).astype(BF16)
    roped = _rope_half(kr_ref[...], cos_ref[...], sin_ref[...]).astype(k_ref.dtype)
    for h in range(MLA_HEADS):
        y = jnp.dot(xn, w_ref[h], preferred_element_type=F32)
        k_ref[h, :, :QK_NOPE_DIM] = y[:, :QK_NOPE_DIM].astype(k_ref.dtype)
        k_ref[h, :, QK_NOPE_DIM:] = roped
        vt_ref[h] = y[:, QK_NOPE_DIM:].T.astype(vt_ref.dtype)


def _mla_projections(small3, cos_t, sin_t, q_gain, wq, kv_gain, wkv, tm):
    b, s, _ = small3.shape
    tm = min(tm, s)
    grid = (b, s // tm)
    head_w = 2 * V7X_LANES
    scale = (QK_NOPE_DIM + QK_ROPE_DIM) ** -0.5 * math.log2(math.e)
    table = pl.BlockSpec((None, tm, V7X_LANES), lambda bi, i: (bi, i, 0))
    gain = pl.BlockSpec((1, Q_LORA_RANK), lambda bi, i: (0, 0))
    wspec = pl.BlockSpec((MLA_HEADS, Q_LORA_RANK, head_w), lambda bi, i: (0, 0, 0))
    est = (2 * tm * Q_LORA_RANK * 4 + 2 * MLA_HEADS * Q_LORA_RANK * head_w * 2
           + 2 * MLA_HEADS * tm * (head_w + V_HEAD_DIM) * 2 + 8 * tm * V7X_LANES * 4)
    params = _cparams(("parallel", "parallel"), est)
    q = pl.pallas_call(
        functools.partial(_q_proj_kernel, scale=scale),
        grid=grid,
        in_specs=[
            pl.BlockSpec((None, tm, Q_LORA_RANK), lambda bi, i: (bi, i, SM_QL // Q_LORA_RANK)),
            gain, wspec, table, table,
        ],
        out_specs=pl.BlockSpec((None, MLA_HEADS, tm, head_w), lambda bi, i: (bi, 0, i, 0)),
        out_shape=jax.ShapeDtypeStruct((b, MLA_HEADS, s, head_w), BF16),
        compiler_params=params,
        name="mla_q_proj",
    )(small3, q_gain.reshape(1, -1).astype(F32), wq, cos_t, sin_t)
    k, vt = pl.pallas_call(
        _kv_proj_kernel,
        grid=grid,
        in_specs=[
            pl.BlockSpec((None, tm, KV_LORA_RANK), lambda bi, i: (bi, i, SM_KVL // KV_LORA_RANK)),
            gain, wspec,
            pl.BlockSpec((None, tm, V7X_LANES), lambda bi, i: (bi, i, SM_ROPE // V7X_LANES)),
            table, table,
        ],
        out_specs=[
            pl.BlockSpec((None, MLA_HEADS, tm, head_w), lambda bi, i: (bi, 0, i, 0)),
            pl.BlockSpec((None, MLA_HEADS, V_HEAD_DIM, tm), lambda bi, i: (bi, 0, 0, i)),
        ],
        out_shape=[
            jax.ShapeDtypeStruct((b, MLA_HEADS, s, head_w), BF16),
            jax.ShapeDtypeStruct((b, MLA_HEADS, V_HEAD_DIM, s), BF16),
        ],
        compiler_params=params,
        name="mla_kv_proj",
    )(small3, kv_gain.reshape(1, -1).astype(F32), wkv, small3, cos_t, sin_t)
    return q, k, vt


def _attn_kernel(q_ref, k_ref, vt_ref, o_ref, acc_ref, st0_ref, st1_ref, p0_ref, p1_ref, *, tk):
    q = q_ref[...]
    tq = q.shape[0]
    n_k = k_ref.shape[0] // tk
    st_refs, p_refs = (st0_ref, st1_ref), (p0_ref, p1_ref)

    def scores(j):
        st_refs[j % 2][...] = lax.dot_general(k_ref[j * tk:(j + 1) * tk, :], q, (((1,), (1,)), ((), ())),
                                              preferred_element_type=F32)

    m = jnp.full((1, tq), -jnp.inf, F32)
    l = jnp.zeros((1, tq), F32)
    scores(0)
    for j in range(n_k):
        if j + 1 < n_k:
            scores(j + 1)
        st = st_refs[j % 2][...]
        m_new = jnp.maximum(m, jnp.max(st, axis=0, keepdims=True))
        alpha = jnp.exp2(m - m_new)
        p = jnp.exp2(st - m_new)
        l = alpha * l + jnp.sum(p, axis=0, keepdims=True)
        m = m_new
        p_refs[j % 2][...] = p.astype(BF16)
        pv = jnp.dot(vt_ref[:, j * tk:(j + 1) * tk], p_refs[j % 2][...], preferred_element_type=F32)
        if j == 0:
            acc_ref[...] = pv
        else:
            acc_ref[...] = alpha * acc_ref[...] + pv
    o_ref[...] = (acc_ref[...] * (1.0 / l)).T.astype(o_ref.dtype)


def _attention(q, k, vt, tq, tk):
    b, h, s, dk = q.shape
    tq, tk = min(tq, s), min(tk, s)
    est = (2 * tq * dk * 2 + 2 * s * dk * 2 + 2 * V_HEAD_DIM * s * 2 + 2 * tq * V_HEAD_DIM * 2
           + V_HEAD_DIM * tq * 4 + 2 * tk * tq * (4 + 2) + 4 * tk * tq * 4)
    scratch = [pltpu.VMEM((V_HEAD_DIM, tq), F32), pltpu.VMEM((tk, tq), F32), pltpu.VMEM((tk, tq), F32),
               pltpu.VMEM((tk, tq), BF16), pltpu.VMEM((tk, tq), BF16)]
    return pl.pallas_call(
        functools.partial(_attn_kernel, tk=tk),
        grid=(b, h, s // tq),
        in_specs=[
            pl.BlockSpec((None, None, tq, dk), lambda bi, hi, i: (bi, hi, i, 0)),
            pl.BlockSpec((None, None, s, dk), lambda bi, hi, i: (bi, hi, 0, 0)),
            pl.BlockSpec((None, None, V_HEAD_DIM, s), lambda bi, hi, i: (bi, hi, 0, 0)),
        ],
        out_specs=pl.BlockSpec((None, tq, V_HEAD_DIM), lambda bi, hi, i: (bi, i, hi)),
        out_shape=jax.ShapeDtypeStruct((b, s, h * V_HEAD_DIM), BF16),
        scratch_shapes=scratch,
        compiler_params=_cparams(("parallel", "parallel", "arbitrary"), est),
        name="mla_attention",
    )(q, k, vt)


HALO = V7X_BF16_SUBLANES


def _with_halo(prev, main, nxt):
    i, n = pl.program_id(1), pl.num_programs(1)
    prev = jnp.where(i > 0, prev, jnp.zeros_like(prev))
    nxt = jnp.where(i < n - 1, nxt, jnp.zeros_like(nxt))
    return jnp.concatenate([prev, main, nxt], axis=0)


def _conv_rows(ext, w, bias, rows):
    taps = w.shape[0]
    acc = bias
    for kk in range(taps):
        off = HALO + kk - (taps - 1) // 2
        acc = acc + ext[off:off + rows] * w[kk:kk + 1]
    return acc


def _halo_specs(s, tm, width, col_block):
    per = tm // HALO
    last = s // HALO - 1

    def prev_map(bi, i, *rest):
        return (bi, jnp.maximum(i * per - 1, 0), col_block(*rest))

    def main_map(bi, i, *rest):
        return (bi, i, col_block(*rest))

    def next_map(bi, i, *rest):
        return (bi, jnp.minimum((i + 1) * per, last), col_block(*rest))

    return [
        pl.BlockSpec((None, HALO, width), prev_map),
        pl.BlockSpec((None, tm, width), main_map),
        pl.BlockSpec((None, HALO, width), next_map),
    ]


def _conv_silu_kernel(prev_ref, main_ref, next_ref, w_ref, b_ref, o_ref):
    ext = _with_halo(prev_ref[...], main_ref[...], next_ref[...]).astype(F32)
    o_ref[...] = _silu(_conv_rows(ext, w_ref[...], b_ref[...], o_ref.shape[0])).astype(o_ref.dtype)


def _ssd_conv(u3, conv_w, conv_b, tm, tc):
    b, s, _ = u3.shape
    tm = min(tm, s)
    n_c = SSD_CONV_DIM // tc
    first = U_XBC // tc
    est = 2 * (tm + 2 * HALO) * tc * 2 + 2 * tm * tc * 2 + 8 * (tm + 2 * HALO) * tc * 4
    return pl.pallas_call(
        _conv_silu_kernel,
        grid=(b, s // tm, n_c),
        in_specs=_halo_specs(s, tm, tc, lambda c: first + c) + [
            pl.BlockSpec((SSD_CONV, tc), lambda bi, i, c: (0, c)),
            pl.BlockSpec((1, tc), lambda bi, i, c: (0, c)),
        ],
        out_specs=pl.BlockSpec((None, tm, tc), lambda bi, i, c: (bi, i, c)),
        out_shape=jax.ShapeDtypeStruct((b, s, SSD_CONV_DIM), BF16),
        compiler_params=_cparams(("parallel", "parallel", "parallel"), est),
        name="ssd_conv_silu",
    )(u3, u3, u3, conv_w.astype(F32), conv_b.reshape(1, -1).astype(F32))


def _softplus(x):
    return jnp.maximum(x, 0.0) + jnp.log1p(jnp.exp(-jnp.abs(x)))


def _chunk_cumsums(a, pos_in_chunk):
    n = a.shape[1]
    cf, cb = a, a
    step = 1
    while step < SSD_CHUNK:
        cf = cf + jnp.where(pos_in_chunk >= step, pltpu.roll(cf, step, 1), 0.0)
        cb = cb + jnp.where(pos_in_chunk < SSD_CHUNK - step, pltpu.roll(cb, n - step, 1), 0.0)
        step *= 2
    return cf, cb


def _split3(x):
    hi = x.astype(BF16)
    rest = x - hi.astype(F32)
    mid = rest.astype(BF16)
    lo = (rest - mid.astype(F32)).astype(BF16)
    return hi, mid, lo


def _dt_prep_kernel(dt_ref, alog_ref, bias_ref, rowsel_ref, acs_ref, exp_ref, seg_ref, dt_out_ref, cd_ref):
    rows, s = dt_ref.shape
    dt = _softplus(dt_ref[...] + bias_ref[...])
    a = -jnp.exp(alog_ref[...]) * dt
    pos = lax.broadcasted_iota(jnp.int32, a.shape, 1) % SSD_CHUNK
    row = lax.broadcasted_iota(jnp.int32, a.shape, 0)
    cf, cb = _chunk_cumsums(a, pos)
    total = cf + cb - a
    acs = jnp.where(row < R, cf, cb)
    dt_out_ref[...] = dt
    cd_ref[...] = jnp.exp(total)
    acs_pieces = [p.astype(F32) for p in _split3(acs)]
    ones_rows = jnp.where(row < PIECES, 1.0, 0.0)
    pad = jnp.zeros((V7X_LANES - (PIECES + 1) * rows, s), F32)
    acs_ref[...] = jnp.concatenate(acs_pieces + [ones_rows, pad], axis=0).T.astype(acs_ref.dtype)
    exp_pieces = [p.astype(F32) for table in (jnp.exp(acs), dt * jnp.exp(total - acs)) for p in _split3(table)]
    pad = jnp.zeros((V7X_LANES - 2 * PIECES * rows, s), F32)
    exp_ref[...] = jnp.concatenate(exp_pieces + [pad], axis=0).T.astype(exp_ref.dtype)
    neg = jnp.concatenate([-p for p in acs_pieces], axis=0).astype(BF16)
    seg_ref[...] = jnp.dot(rowsel_ref[...], neg, preferred_element_type=F32).astype(seg_ref.dtype)


def _dt_prep(dt_raw, a_log, dt_bias):
    b, g, rows, s = dt_raw.shape
    tile = V7X_BF16_SUBLANES
    rowsel = np.zeros((2 * N_PAIRS * 2 * tile, PIECES * rows), np.float32)
    for pair in range(N_PAIRS):
        for d in range(2):
            for half in range(2):
                for piece in range(PIECES):
                    rowsel[((2 * pair + d) * 2 + half) * tile + piece, piece * rows + R * d + 2 * pair + half] = 1.0
    rowsel = jnp.asarray(rowsel, BF16)
    vec = pl.BlockSpec((None, rows, 1), lambda bi, gi: (gi, 0, 0))
    per_group_rows = pl.BlockSpec((None, None, rows, s), lambda bi, gi: (bi, gi, 0, 0))
    per_group_cols = pl.BlockSpec((None, None, s, V7X_LANES), lambda bi, gi: (bi, gi, 0, 0))
    est = 4 * rows * s * 4 + 4 * s * V7X_LANES * 2 + 2 * rowsel.shape[0] * s * 2 + 24 * V7X_LANES * s * 4
    return pl.pallas_call(
        _dt_prep_kernel,
        grid=(b, g),
        in_specs=[per_group_rows, vec, vec, pl.BlockSpec(rowsel.shape, lambda bi, gi: (0, 0))],
        out_specs=[
            per_group_cols,
            per_group_cols,
            pl.BlockSpec((None, None, rowsel.shape[0], s), lambda bi, gi: (bi, gi, 0, 0)),
            per_group_rows,
            per_group_rows,
        ],
        out_shape=[
            jax.ShapeDtypeStruct((b, g, s, V7X_LANES), BF16),
            jax.ShapeDtypeStruct((b, g, s, V7X_LANES), BF16),
            jax.ShapeDtypeStruct((b, g, rowsel.shape[0], s), BF16),
            jax.ShapeDtypeStruct((b, g, rows, s), F32),
            jax.ShapeDtypeStruct((b, g, rows, s), F32),
        ],
        compiler_params=_cparams(("parallel", "parallel"), est),
        name="ssd_dt_prep",
    )(dt_raw, a_log, dt_bias, rowsel)


R = SSD_HEADS_PER_GROUP
N_PAIRS = R // 2
PIECES = 3
ACS_LANES = PIECES * 2 * R
EXPAND_EF, EXPAND_WST = 0, 1
SSD_BWD_CHUNKS_PER_STEP = 8
SSD_FWD_UNROLL = 4


def _ssd_constants():
    sel = np.zeros((2 * N_PAIRS, ACS_LANES, 2 * V7X_LANES), np.float32)
    expand = np.zeros((4, V7X_LANES, SSD_GROUP_WIDTH), np.float32)
    for piece in range(PIECES):
        for d in range(2):
            for pair in range(N_PAIRS):
                for half in range(2):
                    lane = 2 * R * piece + R * d + 2 * pair + half
                    sel[2 * pair + d, lane, half * V7X_LANES:(half + 1) * V7X_LANES] = 1.0
            for t in range(2):
                for r in range(R):
                    lane = ACS_LANES * t + 2 * R * piece + R * d + r
                    expand[2 * t + d, lane, r * SSD_HEAD_DIM:(r + 1) * SSD_HEAD_DIM] = 1.0
    return jnp.asarray(sel, BF16), jnp.asarray(expand, BF16)


def _ssd_kernel(xs_ref, b_ref, c_ref, z_ref, nw_ref, acs_ref, exp_ref, seg_ref, dt_ref, cd_ref, dskip_ref, sel_ref,
                expand_ref, gated_ref, ss_ref, hb_ref, state_ref):
    q = SSD_CHUNK
    n_chunks = xs_ref.shape[0] // q

    def transposed_b(r0):
        return b_ref[pl.ds(r0, q), :].astype(F32).T.astype(BF16)

    def expand_heads(table, which, direction):
        return jnp.dot(table, expand_ref[2 * which + direction], preferred_element_type=F32)

    state_ref[...] = jnp.zeros_like(state_ref)

    n_batch = math.gcd(n_chunks, SSD_BWD_CHUNKS_PER_STEP)

    def bwd_body(i, carry):
        chunks = [n_chunks - 1 - (n_batch * i + k) for k in range(n_batch)]
        starts = [pl.multiple_of(c * q, q) for c in chunks]
        weights = [expand_heads(exp_ref[pl.ds(r0, q), :], EXPAND_WST, 1) for r0 in starts]
        xdws = [(xs_ref[pl.ds(r0, q), :].astype(F32) * w).astype(BF16) for r0, w in zip(starts, weights)]
        contribs = [jnp.dot(transposed_b(r0), xdw, preferred_element_type=F32)
                    for r0, xdw in zip(starts, xdws)]
        state = state_ref[...]
        for c, contrib in zip(chunks, contribs):
            hb_ref[c] = state.astype(hb_ref.dtype)
            state = state * cd_ref[pl.ds(n_chunks + c, 1), :] + contrib
        state_ref[...] = state
        return carry

    lax.fori_loop(0, n_chunks // n_batch, bwd_body, 0)

    state_ref[...] = jnp.zeros_like(state_ref)
    li = lax.broadcasted_iota(jnp.int32, (q, 2 * q), 0)
    si = lax.broadcasted_iota(jnp.int32, (q, 2 * q), 1) % q
    lower, upper = li >= si, li <= si
    lane_low = lax.broadcasted_iota(jnp.int32, (q, V7X_LANES), 1) < SSD_HEAD_DIM
    unused_rows = jnp.zeros((V7X_LANES - ACS_LANES - V7X_BF16_SUBLANES, 2 * q), BF16)

    def fwd_body(c, carry):
        r0 = pl.multiple_of(c * q, q)
        x_bf = xs_ref[pl.ds(r0, q), :]
        x = x_bf.astype(F32)
        bm = b_ref[pl.ds(r0, q), :]
        cm = c_ref[pl.ds(r0, q), :]
        acs_tab = acs_ref[pl.ds(r0, q), :]
        exp_tab = exp_ref[pl.ds(r0, q), :]
        dts = dt_ref[:, pl.ds(r0, q)]
        cb = lax.dot_general(cm, bm, (((1,), (1,)), ((), ())), preferred_element_type=F32)
        cb2 = jnp.concatenate([cb, cb], axis=1)

        def pair_exponents(pair, d):
            k = 2 * pair + d
            tile = jnp.concatenate([seg_ref[k, 0, :, pl.ds(r0, q)], seg_ref[k, 1, :, pl.ds(r0, q)]], axis=1)
            rhs = jnp.concatenate([sel_ref[k], tile, unused_rows], axis=0)
            return jnp.dot(acs_tab, rhs, preferred_element_type=F32)

        def pair_dt(pair, d):
            row = R * d + 2 * pair
            return jnp.concatenate([dts[row:row + 1, :], dts[row + 1:row + 2, :]], axis=1)

        segs = [(pair_exponents(pair, 0), pair_exponents(pair, 1)) for pair in range(N_PAIRS)]
        h_prev = state_ref[...]
        off_f = jnp.dot(cm, h_prev.astype(BF16), preferred_element_type=F32)
        off_b = jnp.dot(cm, hb_ref[c], preferred_element_type=F32)
        ef_f = expand_heads(exp_tab, EXPAND_EF, 0)
        ef_b = expand_heads(exp_tab, EXPAND_EF, 1)
        wst_f = expand_heads(exp_tab, EXPAND_WST, 0)

        diag = []
        for pair in range(N_PAIRS):
            xp = x_bf[:, pair * V7X_LANES:(pair + 1) * V7X_LANES]
            zero = jnp.zeros_like(xp)
            rhs = jnp.concatenate([jnp.where(lane_low, xp, zero), jnp.where(lane_low, zero, xp)], axis=0)
            lf = jnp.where(lower, jnp.exp(segs[pair][0]), 0.0) * pair_dt(pair, 0)
            lb = jnp.where(upper, jnp.exp(segs[pair][1]), 0.0) * pair_dt(pair, 1)
            lhs = (cb2 * (lf + lb)).astype(BF16)
            diag.append(jnp.dot(lhs, rhs, preferred_element_type=F32))
        y = jnp.concatenate(diag, axis=1)
        y = y + off_f * ef_f + off_b * ef_b + x * dskip_ref[...]
        yz = y * _silu(z_ref[pl.ds(r0, q), :].astype(F32))
        ss_ref[pl.ds(r0, q), :] = jnp.sum(yz * yz, axis=1, keepdims=True)
        gated_ref[pl.ds(r0, q), :] = (yz * nw_ref[...]).astype(gated_ref.dtype)

        xdw = (x * wst_f).astype(BF16)
        contrib = jnp.dot(transposed_b(r0), xdw, preferred_element_type=F32)
        state_ref[...] = h_prev * cd_ref[pl.ds(c, 1), :] + contrib
        return carry

    lax.fori_loop(0, n_chunks, fwd_body, 0, unroll=math.gcd(n_chunks, SSD_FWD_UNROLL))


def _ssd_scan(xbc_c, u3, norm_w, acs_tab, exp_tab, seg_rows, dt_rows, cd, dskip):
    b, s, _ = xbc_c.shape
    n_chunks = s // SSD_CHUNK
    gw = SSD_GROUP_WIDTH
    b_first = SSD_D_INNER // SSD_STATE
    c_first = b_first + SSD_GROUPS
    sel, expand = _ssd_constants()
    seg_block = seg_rows.shape[2:]
    est = (3 * 2 * s * gw * 2 + 2 * s * V7X_LANES * 4 + 4 * s * SSD_STATE * 2 + 4 * s * V7X_LANES * 2
           + 2 * 2 * R * s * 4
           + 2 * int(np.prod(seg_block)) * 2 + 2 * (sel.size + expand.size) * 2
           + n_chunks * SSD_STATE * gw * 2 + 4 * n_chunks * gw * 4 + 64 * SSD_CHUNK * gw * 4)
    return pl.pallas_call(
        _ssd_kernel,
        grid=(b, SSD_GROUPS),
        in_specs=[
            pl.BlockSpec((None, s, gw), lambda bi, g: (bi, 0, g)),
            pl.BlockSpec((None, s, SSD_STATE), lambda bi, g: (bi, 0, b_first + g)),
            pl.BlockSpec((None, s, SSD_STATE), lambda bi, g: (bi, 0, c_first + g)),
            pl.BlockSpec((None, s, gw), lambda bi, g: (bi, 0, U_Z // gw + g)),
            pl.BlockSpec((1, gw), lambda bi, g: (0, g)),
            pl.BlockSpec((None, None, s, V7X_LANES), lambda bi, g: (bi, g, 0, 0)),
            pl.BlockSpec((None, None, s, V7X_LANES), lambda bi, g: (bi, g, 0, 0)),
            pl.BlockSpec((None, None) + seg_block, lambda bi, g: (bi, g, 0, 0, 0, 0)),
            pl.BlockSpec((None, None, 2 * R, s), lambda bi, g: (bi, g, 0, 0)),
            pl.BlockSpec((None, None, 2 * n_chunks, gw), lambda bi, g: (bi, g, 0, 0)),
            pl.BlockSpec((None, 1, gw), lambda bi, g: (g, 0, 0)),
            pl.BlockSpec(sel.shape, lambda bi, g: (0, 0, 0)),
            pl.BlockSpec(expand.shape, lambda bi, g: (0, 0, 0)),
        ],
        out_specs=[
            pl.BlockSpec((None, s, gw), lambda bi, g: (bi, 0, g)),
            pl.BlockSpec((None, None, s, 1), lambda bi, g: (bi, g, 0, 0)),
        ],
        out_shape=[
            jax.ShapeDtypeStruct((b, s, SSD_D_INNER), BF16),
            jax.ShapeDtypeStruct((b, SSD_GROUPS, s, 1), F32),
        ],
        scratch_shapes=[
            pltpu.VMEM((n_chunks, SSD_STATE, gw), BF16),
            pltpu.VMEM((SSD_STATE, gw), F32),
        ],
        compiler_params=_cparams(("parallel", "parallel"), est),
        name="ssd_scan",
    )(xbc_c, xbc_c, xbc_c, u3, norm_w.reshape(1, -1).astype(F32), acs_tab, exp_tab, seg_rows, dt_rows, cd, dskip,
      sel, expand)


def _mix_kernel(o_ref, gated_ref, ss_ref, wa_ref, ws_ref, ga_ref, gs_ref, m_ref):
    attn = jnp.dot(o_ref[...], wa_ref[...], preferred_element_type=F32)
    ssd = jnp.dot(gated_ref[...], ws_ref[...], preferred_element_type=F32)
    mean_sq = jnp.sum(ss_ref[...], axis=1, keepdims=True) * (1.0 / gated_ref.shape[1])
    ssd = ssd * lax.rsqrt(mean_sq + EPS)
    mixed = _sigmoid(ga_ref[...].astype(F32)) * attn + _sigmoid(gs_ref[...].astype(F32)) * ssd
    m_ref[...] = mixed.astype(m_ref.dtype)


def _mix(o2, gated2, ss2, u2, wa, ws, tm, tn):
    t = o2.shape[0]
    tm = min(tm, t)
    d_attn, d_ssd = o2.shape[1], gated2.shape[1]
    est = (2 * tm * (d_attn + d_ssd) * 2 + 2 * tm * V7X_LANES * 4 + 2 * (d_attn + d_ssd) * tn * 2
           + 6 * tm * tn * 2 + 6 * tm * tn * 4)
    return pl.pallas_call(
        _mix_kernel,
        grid=(t // tm, D_MODEL // tn),
        in_specs=[
            pl.BlockSpec((tm, d_attn), lambda i, j: (i, 0)),
            pl.BlockSpec((tm, d_ssd), lambda i, j: (i, 0)),
            pl.BlockSpec((tm, ss2.shape[1]), lambda i, j: (i, 0)),
            pl.BlockSpec((d_attn, tn), lambda i, j: (0, j)),
            pl.BlockSpec((d_ssd, tn), lambda i, j: (0, j)),
            pl.BlockSpec((tm, tn), lambda i, j: (i, U_GA // tn + j)),
            pl.BlockSpec((tm, tn), lambda i, j: (i, U_GS // tn + j)),
        ],
        out_specs=pl.BlockSpec((tm, tn), lambda i, j: (i, j)),
        out_shape=jax.ShapeDtypeStruct((t, D_MODEL), BF16),
        compiler_params=_cparams(("parallel", "parallel"), est),
        name="branch_mix",
    )(o2, gated2, ss2, wa, ws, u2, u2)


def _residual_matmul_kernel(a_ref, w_ref, x_ref, o_ref):
    o_ref[...] = x_ref[...] + jnp.dot(a_ref[...], w_ref[...], preferred_element_type=F32)


def _residual_matmul(a, w, x, tm, tn):
    t, k = a.shape
    n = w.shape[1]
    tm, tn = min(tm, t), min(tn, n)
    est = 2 * tm * k * 2 + 2 * k * tn * 2 + 4 * tm * tn * 4
    return pl.pallas_call(
        _residual_matmul_kernel,
        grid=(t // tm, n // tn),
        in_specs=[
            pl.BlockSpec((tm, k), lambda i, j: (i, 0)),
            pl.BlockSpec((k, tn), lambda i, j: (0, j)),
            pl.BlockSpec((tm, tn), lambda i, j: (i, j)),
        ],
        out_specs=pl.BlockSpec((tm, tn), lambda i, j: (i, j)),
        out_shape=jax.ShapeDtypeStruct((t, n), F32),
        compiler_params=_cparams(("parallel", "parallel"), est),
        name="residual_matmul",
    )(a, w, x)


def _ffn_kernel(prev_ref, main_ref, next_ref, g_ref, wg_ref, wv_ref, cwg_ref, cwv_ref, cbg_ref, cbv_ref,
                wd_ref, gf_ref, o_ref, hn_ref):
    j, n_j = pl.program_id(2), pl.num_programs(2)
    tm = main_ref.shape[0]

    @pl.when(j == 0)
    def _():
        ext = _with_halo(prev_ref[...], main_ref[...], next_ref[...])
        hn_ref[...] = (ext * _rms_scale(ext) * g_ref[...]).astype(hn_ref.dtype)
        o_ref[...] = main_ref[...]

    hn = hn_ref[...]
    gate = _conv_rows(jnp.dot(hn, wg_ref[...], preferred_element_type=F32), cwg_ref[...], cbg_ref[...], tm)
    val = _conv_rows(jnp.dot(hn, wv_ref[...], preferred_element_type=F32), cwv_ref[...], cbv_ref[...], tm)
    act = (_silu(gate) * val).astype(BF16)
    o_ref[...] += jnp.dot(act, wd_ref[...], preferred_element_type=F32)

    @pl.when(j == n_j - 1)
    def _():
        h = o_ref[...]
        o_ref[...] = h * _rms_scale(h) * gf_ref[...]


def _ffn(h3, gain, w_up, conv_w, conv_b, w_down, final_gain, tm, tf):
    b, s, d = h3.shape
    tm = min(tm, s)
    n_f = FFN_DIM // tf
    row = lambda bi, i, j: (0, 0)
    est = (2 * (tm + 2 * HALO) * d * 4 + (tm + 2 * HALO) * d * 2 + 2 * 2 * d * tf * 2 + 2 * tf * d * 2
           + 2 * tm * d * 4 + 8 * (tm + 2 * HALO) * tf * 4)
    return pl.pallas_call(
        _ffn_kernel,
        grid=(b, s // tm, n_f),
        in_specs=_halo_specs(s, tm, d, lambda j: 0) + [
            pl.BlockSpec((1, d), row),
            pl.BlockSpec((d, tf), lambda bi, i, j: (0, j)),
            pl.BlockSpec((d, tf), lambda bi, i, j: (0, n_f + j)),
            pl.BlockSpec((FFN_CONV, tf), lambda bi, i, j: (0, j)),
            pl.BlockSpec((FFN_CONV, tf), lambda bi, i, j: (0, n_f + j)),
            pl.BlockSpec((1, tf), lambda bi, i, j: (0, j)),
            pl.BlockSpec((1, tf), lambda bi, i, j: (0, n_f + j)),
            pl.BlockSpec((tf, d), lambda bi, i, j: (j, 0)),
            pl.BlockSpec((1, d), row),
        ],
        out_specs=pl.BlockSpec((None, tm, d), lambda bi, i, j: (bi, i, 0)),
        out_shape=jax.ShapeDtypeStruct((b, s, d), F32),
        scratch_shapes=[pltpu.VMEM((tm + 2 * HALO, d), BF16)],
        compiler_params=_cparams(("parallel", "parallel", "arbitrary"), est),
        name="conv_ffn",
    )(h3, h3, h3, gain.reshape(1, -1).astype(F32), w_up, w_up, conv_w.astype(F32), conv_w.astype(F32),
      conv_b.reshape(1, -1).astype(F32), conv_b.reshape(1, -1).astype(F32), w_down,
      final_gain.reshape(1, -1).astype(F32))


def _swap_halves(w):
    half = w.shape[-1] // 2
    return jnp.concatenate([w[..., half:], w[..., :half]], axis=-1)


def _layer(h, cos_t, sin_t, norm_mix_w, w_in, q_norm_w, w_uq, kv_norm_w, w_ukv, w_o_attn, ssd_conv_w,
           ssd_conv_b, a_log_fwd, a_log_bwd, dt_bias_fwd, dt_bias_bwd, ssd_d, ssd_norm_w, w_o_ssd, w_out):
    b, s, d = h.shape
    t = b * s
    n_chunks = s // SSD_CHUNK
    x2 = h.reshape(t, d)

    w_in_stack, layer = w_in
    assert w_in_stack.shape[2] == IN_WIDTH
    w_kr = w_in_stack[layer, :, IN_KR:IN_Z]
    w_small = jnp.concatenate([w_in_stack[layer, :, IN_QL:IN_KR], w_kr, _swap_halves(w_kr),
                               w_in_stack[layer, :, IN_DTF:IN_GA]], axis=1).astype(BF16)
    u2 = _norm_matmul(x2, norm_mix_w, _main_in_weights(w_in_stack, layer), BF16, tm=1024, tn=1024)
    small2 = _norm_matmul(x2, norm_mix_w, w_small, F32, tm=1024, tn=SM_WIDTH)
    u3 = u2.reshape(b, s, U_WIDTH)
    small3 = small2.reshape(b, s, SM_WIDTH)

    wq = w_uq.reshape(Q_LORA_RANK, MLA_HEADS, QK_NOPE_DIM + QK_ROPE_DIM)
    wq_rope = wq[..., QK_NOPE_DIM:]
    wq = jnp.concatenate([wq[..., :QK_NOPE_DIM], wq_rope, _swap_halves(wq_rope)], axis=-1)
    wq = wq.transpose(1, 0, 2).astype(BF16)
    wkv = w_ukv.reshape(KV_LORA_RANK, MLA_HEADS, QK_NOPE_DIM + V_HEAD_DIM).transpose(1, 0, 2).astype(BF16)
    q, k, vt = _mla_projections(small3, cos_t, sin_t, q_norm_w, wq, kv_norm_w, wkv, tm=512)
    o = _attention(q, k, vt, tq=1024, tk=1024)

    xbc_c = _ssd_conv(u3, ssd_conv_w, ssd_conv_b, tm=512, tc=512)
    g_ = SSD_GROUPS

    def per_group(v):
        lead = v.shape[:-1]
        v = jnp.moveaxis(v.reshape(lead + (2, g_, R)), -2, 0)
        return v.reshape((g_,) + lead + (2 * R,))

    dt_raw = per_group(small3[:, :, SM_DT:]).transpose(1, 0, 3, 2)
    a_log = per_group(jnp.concatenate([a_log_fwd, a_log_bwd]).astype(F32))[..., None]
    dt_bias = per_group(jnp.concatenate([dt_bias_fwd, dt_bias_bwd]).astype(F32))[..., None]
    acs_tab, exp_tab, seg_rows, dt_heads, cd = _dt_prep(dt_raw, a_log, dt_bias)
    seg_rows = seg_rows.reshape(b, g_, 2 * N_PAIRS, 2, V7X_BF16_SUBLANES, s)
    cd = cd[..., ::SSD_CHUNK].reshape(b, g_, 2, R, n_chunks).transpose(0, 1, 2, 4, 3)
    cd = jnp.repeat(cd, SSD_HEAD_DIM, axis=-1).reshape(b, g_, 2 * n_chunks, SSD_GROUP_WIDTH)
    dskip = jnp.repeat(ssd_d.astype(F32), SSD_HEAD_DIM).reshape(g_, 1, SSD_GROUP_WIDTH)
    gated, ss = _ssd_scan(xbc_c, u3, ssd_norm_w, acs_tab, exp_tab, seg_rows, dt_heads, cd, dskip)
    ss2 = ss.reshape(b, g_, s).transpose(0, 2, 1).reshape(t, g_)

    mixed = _mix(o.reshape(t, -1), gated.reshape(t, -1), ss2, u2, w_o_attn.astype(BF16),
                 w_o_ssd.astype(BF16), tm=1024, tn=512)
    h1 = _residual_matmul(mixed, w_out.astype(BF16), x2, tm=1024, tn=1024)
    return h1.reshape(b, s, d)


def kernel(x, positions, norm_mix_w, w_in, q_norm_w, w_uq, kv_norm_w, w_ukv, w_o_attn, ssd_conv_w, ssd_conv_b,
           a_log_fwd, a_log_bwd, dt_bias_fwd, dt_bias_bwd, ssd_d, ssd_norm_w, w_o_ssd, w_out, norm_ffn_w,
           ffn_w_up, ffn_conv_w, ffn_conv_b, ffn_w_down, norm_final_w):
    depth = w_in.shape[0]
    assert depth == 1, "the conv-gated MLP kernel fuses the final RMSNorm, which is only valid for one layer"
    half = QK_ROPE_DIM // 2
    inv_freq = ROPE_THETA ** (-jnp.arange(half, dtype=F32) / half)
    ang = positions.astype(F32)[..., None] * inv_freq
    cos, sin = jnp.cos(ang), jnp.sin(ang)
    pad = jnp.zeros(cos.shape[:-1] + (V7X_LANES - QK_ROPE_DIM,), F32)
    cos_t = jnp.concatenate([cos, cos, pad], axis=-1)
    sin_t = jnp.concatenate([-sin, sin, pad], axis=-1)

    l = 0
    h1 = _layer(x, cos_t, sin_t, norm_mix_w[l], (w_in, l), q_norm_w[l], w_uq[l], kv_norm_w[l], w_ukv[l],
                w_o_attn[l], ssd_conv_w[l], ssd_conv_b[l], a_log_fwd[l], a_log_bwd[l], dt_bias_fwd[l],
                dt_bias_bwd[l], ssd_d[l], ssd_norm_w[l], w_o_ssd[l], w_out[l])
    return _ffn(h1, norm_ffn_w[l], ffn_w_up[l].astype(BF16), ffn_conv_w[l], ffn_conv_b[l],
                ffn_w_down[l].astype(BF16), norm_final_w, tm=512, tf=512)
```

```python
import functools
import math

import jax
import jax.numpy as jnp
import numpy as np
from jax import lax
from jax.experimental import pallas as pl
from jax.experimental.pallas import tpu as pltpu

D_MODEL = 2048
MLA_HEADS = 16
Q_LORA_RANK = 512
KV_LORA_RANK = 512
QK_NOPE_DIM = 128
QK_ROPE_DIM = 64
V_HEAD_DIM = 128
ROPE_THETA = 10000.0
SSD_D_INNER = 2 * D_MODEL
SSD_HEAD_DIM = 64
SSD_HEADS = SSD_D_INNER // SSD_HEAD_DIM
SSD_GROUPS = 8
SSD_HEADS_PER_GROUP = SSD_HEADS // SSD_GROUPS
SSD_STATE = 128
SSD_CONV = 5
SSD_CHUNK = 128
SSD_CONV_DIM = SSD_D_INNER + 2 * SSD_GROUPS * SSD_STATE
SSD_GROUP_WIDTH = SSD_HEADS_PER_GROUP * SSD_HEAD_DIM
FFN_DIM = 5632
FFN_CONV = 3
EPS = 1e-6

V7X_LANES = 128
V7X_BF16_SUBLANES = 16
V7X_VMEM_BYTES = 64 * 1024 * 1024
VMEM_LIMIT_CAP = V7X_VMEM_BYTES - 8 * 1024 * 1024

F32 = jnp.float32
BF16 = jnp.bfloat16

IN_QL = 0
IN_KVL = IN_QL + Q_LORA_RANK
IN_KR = IN_KVL + KV_LORA_RANK
IN_Z = IN_KR + QK_ROPE_DIM
IN_XBC = IN_Z + SSD_D_INNER
IN_DTF = IN_XBC + SSD_CONV_DIM
IN_DTB = IN_DTF + SSD_HEADS
IN_GA = IN_DTB + SSD_HEADS
IN_GS = IN_GA + D_MODEL
IN_WIDTH = IN_GS + D_MODEL

U_Z = 0
U_XBC = U_Z + SSD_D_INNER
U_GA = U_XBC + SSD_CONV_DIM
U_GS = U_GA + D_MODEL
U_WIDTH = U_GS + D_MODEL
SM_QL = 0
SM_KVL = SM_QL + Q_LORA_RANK
SM_ROPE = SM_KVL + KV_LORA_RANK
SM_DT = SM_ROPE + 2 * QK_ROPE_DIM
SM_WIDTH = SM_DT + 2 * SSD_HEADS


def _cparams(semantics, vmem_estimate_bytes):
    limit = int(min(max(vmem_estimate_bytes * 5 // 4, 16 * 1024 * 1024), VMEM_LIMIT_CAP))
    return pltpu.CompilerParams(dimension_semantics=semantics, vmem_limit_bytes=limit)


def _rms_scale(x):
    return lax.rsqrt(jnp.mean(x * x, axis=-1, keepdims=True) + EPS)


def _silu(x):
    return x * (1.0 / (1.0 + jnp.exp(-x)))


def _sigmoid(x):
    return 1.0 / (1.0 + jnp.exp(-x))


W_PREP_BLOCK = 512


def _w_prep_kernel(w_ref, o_ref):
    o_ref[...] = w_ref[0].astype(o_ref.dtype)


def _main_in_weights(w_in_t_stack, layer):
    _, _, k = w_in_t_stack.shape
    n_first = (IN_DTF - IN_Z) // W_PREP_BLOCK

    unit = math.gcd(IN_Z, W_PREP_BLOCK, IN_GA - IN_DTF)

    def src_row(j):
        units = IN_Z // unit + (W_PREP_BLOCK // unit) * j + jnp.where(j >= n_first, (IN_GA - IN_DTF) // unit, 0)
        return units * unit

    est = 2 * W_PREP_BLOCK * k * (4 + 2) + 2 * W_PREP_BLOCK * k * 4
    return pl.pallas_call(
        _w_prep_kernel,
        grid=(U_WIDTH // W_PREP_BLOCK,),
        in_specs=[pl.BlockSpec((pl.Element(1), pl.Element(W_PREP_BLOCK), pl.Element(k)),
                               lambda j: (layer, src_row(j), 0))],
        out_specs=pl.BlockSpec((W_PREP_BLOCK, k), lambda j: (j, 0)),
        out_shape=jax.ShapeDtypeStruct((U_WIDTH, k), BF16),
        compiler_params=_cparams(("parallel",), est),
        name="w_in_prep",
    )(w_in_t_stack)


def _norm_matmul_kernel(x_ref, g_ref, w_ref, o_ref, xn_ref):
    @pl.when(pl.program_id(1) == 0)
    def _():
        x = x_ref[...].astype(F32)
        xn_ref[...] = (x * _rms_scale(x) * g_ref[...]).astype(xn_ref.dtype)

    o_ref[...] = lax.dot_general(xn_ref[...], w_ref[...], (((1,), (1,)), ((), ())),
                                 preferred_element_type=F32).astype(o_ref.dtype)


def _norm_matmul(x, gain, w_t, out_dtype, tm, tn):
    t, k = x.shape
    n = w_t.shape[0]
    tm, tn = min(tm, t), min(tn, n)
    est = 2 * tm * k * x.dtype.itemsize + tm * k * 2 + 2 * k * tn * 2 + 2 * tm * tn * 4
    return pl.pallas_call(
        _norm_matmul_kernel,
        grid=(t // tm, n // tn),
        in_specs=[
            pl.BlockSpec((tm, k), lambda i, j: (i, 0)),
            pl.BlockSpec((1, k), lambda i, j: (0, 0)),
            pl.BlockSpec((tn, k), lambda i, j: (j, 0)),
        ],
        out_specs=pl.BlockSpec((tm, tn), lambda i, j: (i, j)),
        out_shape=jax.ShapeDtypeStruct((t, n), out_dtype),
        scratch_shapes=[pltpu.VMEM((tm, k), BF16)],
        compiler_params=_cparams(("parallel", "arbitrary"), est),
        name="norm_matmul",
    )(x, gain.reshape(1, k).astype(F32), w_t)


def _rope_half(y2, cos_t, sin_t):
    return y2 * cos_t + pltpu.roll(y2, QK_ROPE_DIM, 1) * sin_t


def _q_proj_kernel(ql_ref, g_ref, w_ref, cos_ref, sin_ref, o_ref, *, scale):
    x = ql_ref[...].astype(F32)
    xn = (x * _rms_scale(x) * g_ref[...]).astype(BF16)
    cos_t = cos_ref[...] * scale
    sin_t = sin_ref[...] * scale
    for h in range(MLA_HEADS):
        y = jnp.dot(xn, w_ref[h], preferred_element_type=F32)
        o_ref[h, :, :QK_NOPE_DIM] = (y[:, :QK_NOPE_DIM] * scale).astype(o_ref.dtype)
        o_ref[h, :, QK_NOPE_DIM:] = _rope_half(y[:, QK_NOPE_DIM:], cos_t, sin_t).astype(o_ref.dtype)


def _kv_proj_kernel(kvl_ref, g_ref, w_ref, kr_ref, cos_ref, sin_ref, k_ref, vt_ref):
    x = kvl_ref[...].astype(F32)
    xn = (x * _rms_scale(x) * g_ref[...]).astype(BF16)
    roped = _rope_half(kr_ref[...], cos_ref[...], sin_ref[...]).astype(k_ref.dtype)
    for h in range(MLA_HEADS):
        y = jnp.dot(xn, w_ref[h], preferred_element_type=F32)
        k_ref[h, :, :QK_NOPE_DIM] = y[:, :QK_NOPE_DIM].astype(k_ref.dtype)
        k_ref[h, :, QK_NOPE_DIM:] = roped
        vt_ref[h] = y[:, QK_NOPE_DIM:].T.astype(vt_ref.dtype)


def _mla_projections(small3, cos_t, sin_t, q_gain, wq, kv_gain, wkv, tm):
    b, s, _ = small3.shape
    tm = min(tm, s)
    grid = (b, s // tm)
    head_w = 2 * V7X_LANES
    scale = (QK_NOPE_DIM + QK_ROPE_DIM) ** -0.5 * math.log2(math.e)
    table = pl.BlockSpec((None, tm, V7X_LANES), lambda bi, i: (bi, i, 0))
    gain = pl.BlockSpec((1, Q_LORA_RANK), lambda bi, i: (0, 0))
    wspec = pl.BlockSpec((MLA_HEADS, Q_LORA_RANK, head_w), lambda bi, i: (0, 0, 0))
    est = (2 * tm * Q_LORA_RANK * 4 + 2 * MLA_HEADS * Q_LORA_RANK * head_w * 2
           + 2 * MLA_HEADS * tm * (head_w + V_HEAD_DIM) * 2 + 8 * tm * V7X_LANES * 4)
    params = _cparams(("parallel", "parallel"), est)
    q = pl.pallas_call(
        functools.partial(_q_proj_kernel, scale=scale),
        grid=grid,
        in_specs=[
            pl.BlockSpec((None, tm, Q_LORA_RANK), lambda bi, i: (bi, i, SM_QL // Q_LORA_RANK)),
            gain, wspec, table, table,
        ],
        out_specs=pl.BlockSpec((None, MLA_HEADS, tm, head_w), lambda bi, i: (bi, 0, i, 0)),
        out_shape=jax.ShapeDtypeStruct((b, MLA_HEADS, s, head_w), BF16),
        compiler_params=params,
        name="mla_q_proj",
    )(small3, q_gain.reshape(1, -1).astype(F32), wq, cos_t, sin_t)
    k, vt = pl.pallas_call(
        _kv_proj_kernel,
        grid=grid,
        in_specs=[
            pl.BlockSpec((None, tm, KV_LORA_RANK), lambda bi, i: (bi, i, SM_KVL // KV_LORA_RANK)),
            gain, wspec,
            pl.BlockSpec((None, tm, V7X_LANES), lambda bi, i: (bi, i, SM_ROPE // V7X_LANES)),
            table, table,
        ],
        out_specs=[
            pl.BlockSpec((None, MLA_HEADS, tm, head_w), lambda bi, i: (bi, 0, i, 0)),
            pl.BlockSpec((None, MLA_HEADS, V_HEAD_DIM, tm), lambda bi, i: (bi, 0, 0, i)),
        ],
        out_shape=[
            jax.ShapeDtypeStruct((b, MLA_HEADS, s, head_w), BF16),
            jax.ShapeDtypeStruct((b, MLA_HEADS, V_HEAD_DIM, s), BF16),
        ],
        compiler_params=params,
        name="mla_kv_proj",
    )(small3, kv_gain.reshape(1, -1).astype(F32), wkv, small3, cos_t, sin_t)
    return q, k, vt


def _attn_kernel(q_ref, k_ref, vt_ref, o_ref, acc_ref, st0_ref, st1_ref, p0_ref, p1_ref, *, tk):
    q = q_ref[...]
    tq = q.shape[0]
    n_k = k_ref.shape[0] // tk
    st_refs, p_refs = (st0_ref, st1_ref), (p0_ref, p1_ref)

    def scores(j):
        st_refs[j % 2][...] = lax.dot_general(k_ref[j * tk:(j + 1) * tk, :], q, (((1,), (1,)), ((), ())),
                                              preferred_element_type=F32)

    m = jnp.full((1, tq), -jnp.inf, F32)
    l = jnp.zeros((1, tq), F32)
    scores(0)
    for j in range(n_k):
        if j + 1 < n_k:
            scores(j + 1)
        st = st_refs[j % 2][...]
        m_new = jnp.maximum(m, jnp.max(st, axis=0, keepdims=True))
        alpha = jnp.exp2(m - m_new)
        p = jnp.exp2(st - m_new)
        l = alpha * l + jnp.sum(p, axis=0, keepdims=True)
        m = m_new
        p_refs[j % 2][...] = p.astype(BF16)
        pv = jnp.dot(vt_ref[:, j * tk:(j + 1) * tk], p_refs[j % 2][...], preferred_element_type=F32)
        if j == 0:
            acc_ref[...] = pv
        else:
            acc_ref[...] = alpha * acc_ref[...] + pv
    o_ref[...] = (acc_ref[...] * (1.0 / l)).T.astype(o_ref.dtype)


def _attention(q, k, vt, tq, tk):
    b, h, s, dk = q.shape
    tq, tk = min(tq, s), min(tk, s)
    est = (2 * tq * dk * 2 + 2 * s * dk * 2 + 2 * V_HEAD_DIM * s * 2 + 2 * tq * V_HEAD_DIM * 2
           + V_HEAD_DIM * tq * 4 + 2 * tk * tq * (4 + 2) + 4 * tk * tq * 4)
    scratch = [pltpu.VMEM((V_HEAD_DIM, tq), F32), pltpu.VMEM((tk, tq), F32), pltpu.VMEM((tk, tq), F32),
               pltpu.VMEM((tk, tq), BF16), pltpu.VMEM((tk, tq), BF16)]
    return pl.pallas_call(
        functools.partial(_attn_kernel, tk=tk),
        grid=(b, h, s // tq),
        in_specs=[
            pl.BlockSpec((None, None, tq, dk), lambda bi, hi, i: (bi, hi, i, 0)),
            pl.BlockSpec((None, None, s, dk), lambda bi, hi, i: (bi, hi, 0, 0)),
            pl.BlockSpec((None, None, V_HEAD_DIM, s), lambda bi, hi, i: (bi, hi, 0, 0)),
        ],
        out_specs=pl.BlockSpec((None, tq, V_HEAD_DIM), lambda bi, hi, i: (bi, i, hi)),
        out_shape=jax.ShapeDtypeStruct((b, s, h * V_HEAD_DIM), BF16),
        scratch_shapes=scratch,
        compiler_params=_cparams(("parallel", "parallel", "arbitrary"), est),
        name="mla_attention",
    )(q, k, vt)


HALO = V7X_BF16_SUBLANES


def _with_halo(prev, main, nxt):
    i, n = pl.program_id(1), pl.num_programs(1)
    prev = jnp.where(i > 0, prev, jnp.zeros_like(prev))
    nxt = jnp.where(i < n - 1, nxt, jnp.zeros_like(nxt))
    return jnp.concatenate([prev, main, nxt], axis=0)


def _conv_rows(ext, w, bias, rows):
    taps = w.shape[0]
    acc = bias
    for kk in range(taps):
        off = HALO + kk - (taps - 1) // 2
        acc = acc + ext[off:off + rows] * w[kk:kk + 1]
    return acc


def _halo_specs(s, tm, width, col_block):
    per = tm // HALO
    last = s // HALO - 1

    def prev_map(bi, i, *rest):
        return (bi, jnp.maximum(i * per - 1, 0), col_block(*rest))

    def main_map(bi, i, *rest):
        return (bi, i, col_block(*rest))

    def next_map(bi, i, *rest):
        return (bi, jnp.minimum((i + 1) * per, last), col_block(*rest))

    return [
        pl.BlockSpec((None, HALO, width), prev_map),
        pl.BlockSpec((None, tm, width), main_map),
        pl.BlockSpec((None, HALO, width), next_map),
    ]


def _conv_silu_kernel(prev_ref, main_ref, next_ref, w_ref, b_ref, o_ref):
    ext = _with_halo(prev_ref[...], main_ref[...], next_ref[...]).astype(F32)
    o_ref[...] = _silu(_conv_rows(ext, w_ref[...], b_ref[...], o_ref.shape[0])).astype(o_ref.dtype)


def _ssd_conv(u3, conv_w, conv_b, tm, tc):
    b, s, _ = u3.shape
    tm = min(tm, s)
    n_c = SSD_CONV_DIM // tc
    first = U_XBC // tc
    est = 2 * (tm + 2 * HALO) * tc * 2 + 2 * tm * tc * 2 + 8 * (tm + 2 * HALO) * tc * 4
    return pl.pallas_call(
        _conv_silu_kernel,
        grid=(b, s // tm, n_c),
        in_specs=_halo_specs(s, tm, tc, lambda c: first + c) + [
            pl.BlockSpec((SSD_CONV, tc), lambda bi, i, c: (0, c)),
            pl.BlockSpec((1, tc), lambda bi, i, c: (0, c)),
        ],
        out_specs=pl.BlockSpec((None, tm, tc), lambda bi, i, c: (bi, i, c)),
        out_shape=jax.ShapeDtypeStruct((b, s, SSD_CONV_DIM), BF16),
        compiler_params=_cparams(("parallel", "parallel", "parallel"), est),
        name="ssd_conv_silu",
    )(u3, u3, u3, conv_w.astype(F32), conv_b.reshape(1, -1).astype(F32))


def _softplus(x):
    return jnp.maximum(x, 0.0) + jnp.log1p(jnp.exp(-jnp.abs(x)))


def _chunk_cumsums(a, pos_in_chunk):
    n = a.shape[1]
    cf, cb = a, a
    step = 1
    while step < SSD_CHUNK:
        cf = cf + jnp.where(pos_in_chunk >= step, pltpu.roll(cf, step, 1), 0.0)
        cb = cb + jnp.where(pos_in_chunk < SSD_CHUNK - step, pltpu.roll(cb, n - step, 1), 0.0)
        step *= 2
    return cf, cb


def _split3(x):
    hi = x.astype(BF16)
    rest = x - hi.astype(F32)
    mid = rest.astype(BF16)
    lo = (rest - mid.astype(F32)).astype(BF16)
    return hi, mid, lo


def _dt_prep_kernel(dt_ref, alog_ref, bias_ref, rowsel_ref, acs_ref, exp_ref, seg_ref, dt_out_ref, cd_ref):
    rows, s = dt_ref.shape
    dt = _softplus(dt_ref[...] + bias_ref[...])
    a = -jnp.exp(alog_ref[...]) * dt
    pos = lax.broadcasted_iota(jnp.int32, a.shape, 1) % SSD_CHUNK
    row = lax.broadcasted_iota(jnp.int32, a.shape, 0)
    cf, cb = _chunk_cumsums(a, pos)
    total = cf + cb - a
    acs = jnp.where(row < R, cf, cb)
    dt_out_ref[...] = dt
    cd_ref[...] = jnp.exp(total)
    acs_pieces = [p.astype(F32) for p in _split3(acs)]
    ones_rows = jnp.where(row < PIECES, 1.0, 0.0)
    pad = jnp.zeros((V7X_LANES - (PIECES + 1) * rows, s), F32)
    acs_ref[...] = jnp.concatenate(acs_pieces + [ones_rows, pad], axis=0).T.astype(acs_ref.dtype)
    exp_pieces = [p.astype(F32) for table in (jnp.exp(acs), dt * jnp.exp(total - acs)) for p in _split3(table)]
    pad = jnp.zeros((V7X_LANES - 2 * PIECES * rows, s), F32)
    exp_ref[...] = jnp.concatenate(exp_pieces + [pad], axis=0).T.astype(exp_ref.dtype)
    neg = jnp.concatenate([-p for p in acs_pieces], axis=0).astype(BF16)
    seg_ref[...] = jnp.dot(rowsel_ref[...], neg, preferred_element_type=F32).astype(seg_ref.dtype)


def _dt_prep(dt_raw, a_log, dt_bias):
    b, g, rows, s = dt_raw.shape
    tile = V7X_BF16_SUBLANES
    rowsel = np.zeros((2 * N_PAIRS * 2 * tile, PIECES * rows), np.float32)
    for pair in range(N_PAIRS):
        for d in range(2):
            for half in range(2):
                for piece in range(PIECES):
                    rowsel[((2 * pair + d) * 2 + half) * tile + piece, piece * rows + R * d + 2 * pair + half] = 1.0
    rowsel = jnp.asarray(rowsel, BF16)
    vec = pl.BlockSpec((None, rows, 1), lambda bi, gi: (gi, 0, 0))
    per_group_rows = pl.BlockSpec((None, None, rows, s), lambda bi, gi: (bi, gi, 0, 0))
    per_group_cols = pl.BlockSpec((None, None, s, V7X_LANES), lambda bi, gi: (bi, gi, 0, 0))
    est = 4 * rows * s * 4 + 4 * s * V7X_LANES * 2 + 2 * rowsel.shape[0] * s * 2 + 24 * V7X_LANES * s * 4
    return pl.pallas_call(
        _dt_prep_kernel,
        grid=(b, g),
        in_specs=[per_group_rows, vec, vec, pl.BlockSpec(rowsel.shape, lambda bi, gi: (0, 0))],
        out_specs=[
            per_group_cols,
            per_group_cols,
            pl.BlockSpec((None, None, rowsel.shape[0], s), lambda bi, gi: (bi, gi, 0, 0)),
            per_group_rows,
            per_group_rows,
        ],
        out_shape=[
            jax.ShapeDtypeStruct((b, g, s, V7X_LANES), BF16),
            jax.ShapeDtypeStruct((b, g, s, V7X_LANES), BF16),
            jax.ShapeDtypeStruct((b, g, rowsel.shape[0], s), BF16),
            jax.ShapeDtypeStruct((b, g, rows, s), F32),
            jax.ShapeDtypeStruct((b, g, rows, s), F32),
        ],
        compiler_params=_cparams(("parallel", "parallel"), est),
        name="ssd_dt_prep",
    )(dt_raw, a_log, dt_bias, rowsel)


R = SSD_HEADS_PER_GROUP
N_PAIRS = R // 2
PIECES = 3
ACS_LANES = PIECES * 2 * R
EXPAND_EF, EXPAND_WST = 0, 1
SSD_BWD_CHUNKS_PER_STEP = 8
SSD_FWD_UNROLL = 4


def _ssd_constants():
    sel = np.zeros((2 * N_PAIRS, ACS_LANES, 2 * V7X_LANES), np.float32)
    expand = np.zeros((4, V7X_LANES, SSD_GROUP_WIDTH), np.float32)
    for piece in range(PIECES):
        for d in range(2):
            for pair in range(N_PAIRS):
                for half in range(2):
                    lane = 2 * R * piece + R * d + 2 * pair + half
                    sel[2 * pair + d, lane, half * V7X_LANES:(half + 1) * V7X_LANES] = 1.0
            for t in range(2):
                for r in range(R):
                    lane = ACS_LANES * t + 2 * R * piece + R * d + r
                    expand[2 * t + d, lane, r * SSD_HEAD_DIM:(r + 1) * SSD_HEAD_DIM] = 1.0
    return jnp.asarray(sel, BF16), jnp.asarray(expand, BF16)


def _ssd_kernel(xs_ref, b_ref, c_ref, z_ref, nw_ref, acs_ref, exp_ref, seg_ref, dt_ref, cd_ref, dskip_ref, sel_ref,
                expand_ref, gated_ref, ss_ref, hb_ref, state_ref):
    q = SSD_CHUNK
    n_chunks = xs_ref.shape[0] // q

    def transposed_b(r0):
        return b_ref[pl.ds(r0, q), :].astype(F32).T.astype(BF16)

    def expand_heads(table, which, direction):
        return jnp.dot(table, expand_ref[2 * which + direction], preferred_element_type=F32)

    state_ref[...] = jnp.zeros_like(state_ref)

    n_batch = math.gcd(n_chunks, SSD_BWD_CHUNKS_PER_STEP)

    def bwd_body(i, carry):
        chunks = [n_chunks - 1 - (n_batch * i + k) for k in range(n_batch)]
        starts = [pl.multiple_of(c * q, q) for c in chunks]
        weights = [expand_heads(exp_ref[pl.ds(r0, q), :], EXPAND_WST, 1) for r0 in starts]
        xdws = [(xs_ref[pl.ds(r0, q), :].astype(F32) * w).astype(BF16) for r0, w in zip(starts, weights)]
        contribs = [jnp.dot(transposed_b(r0), xdw, preferred_element_type=F32)
                    for r0, xdw in zip(starts, xdws)]
        state = state_ref[...]
        for c, contrib in zip(chunks, contribs):
            hb_ref[c] = state.astype(hb_ref.dtype)
            state = state * cd_ref[pl.ds(n_chunks + c, 1), :] + contrib
        state_ref[...] = state
        return carry

    lax.fori_loop(0, n_chunks // n_batch, bwd_body, 0)

    state_ref[...] = jnp.zeros_like(state_ref)
    li = lax.broadcasted_iota(jnp.int32, (q, 2 * q), 0)
    si = lax.broadcasted_iota(jnp.int32, (q, 2 * q), 1) % q
    lower, upper = li >= si, li <= si
    lane_low = lax.broadcasted_iota(jnp.int32, (q, V7X_LANES), 1) < SSD_HEAD_DIM
    unused_rows = jnp.zeros((V7X_LANES - ACS_LANES - V7X_BF16_SUBLANES, 2 * q), BF16)

    def fwd_body(c, carry):
        r0 = pl.multiple_of(c * q, q)
        x_bf = xs_ref[pl.ds(r0, q), :]
        x = x_bf.astype(F32)
        bm = b_ref[pl.ds(r0, q), :]
        cm = c_ref[pl.ds(r0, q), :]
        acs_tab = acs_ref[pl.ds(r0, q), :]
        exp_tab = exp_ref[pl.ds(r0, q), :]
        dts = dt_ref[:, pl.ds(r0, q)]
        cb = lax.dot_general(cm, bm, (((1,), (1,)), ((), ())), preferred_element_type=F32)
        cb2 = jnp.concatenate([cb, cb], axis=1)

        def pair_exponents(pair, d):
            k = 2 * pair + d
            tile = jnp.concatenate([seg_ref[k, 0, :, pl.ds(r0, q)], seg_ref[k, 1, :, pl.ds(r0, q)]], axis=1)
            rhs = jnp.concatenate([sel_ref[k], tile, unused_rows], axis=0)
            return jnp.dot(acs_tab, rhs, preferred_element_type=F32)

        def pair_dt(pair, d):
            row = R * d + 2 * pair
            return jnp.concatenate([dts[row:row + 1, :], dts[row + 1:row + 2, :]], axis=1)

        segs = [(pair_exponents(pair, 0), pair_exponents(pair, 1)) for pair in range(N_PAIRS)]
        h_prev = state_ref[...]
        off_f = jnp.dot(cm, h_prev.astype(BF16), preferred_element_type=F32)
        off_b = jnp.dot(cm, hb_ref[c], preferred_element_type=F32)
        ef_f = expand_heads(exp_tab, EXPAND_EF, 0)
        ef_b = expand_heads(exp_tab, EXPAND_EF, 1)
        wst_f = expand_heads(exp_tab, EXPAND_WST, 0)

        diag = []
        for pair in range(N_PAIRS):
            xp = x_bf[:, pair * V7X_LANES:(pair + 1) * V7X_LANES]
            zero = jnp.zeros_like(xp)
            rhs = jnp.concatenate([jnp.where(lane_low, xp, zero), jnp.where(lane_low, zero, xp)], axis=0)
            lf = jnp.where(lower, jnp.exp(segs[pair][0]), 0.0) * pair_dt(pair, 0)
            lb = jnp.where(upper, jnp.exp(segs[pair][1]), 0.0) * pair_dt(pair, 1)
            lhs = (cb2 * (lf + lb)).astype(BF16)
            diag.append(jnp.dot(lhs, rhs, preferred_element_type=F32))
        y = jnp.concatenate(diag, axis=1)
        y = y + off_f * ef_f + off_b * ef_b + x * dskip_ref[...]
        yz = y * _silu(z_ref[pl.ds(r0, q), :].astype(F32))
        ss_ref[pl.ds(r0, q), :] = jnp.sum(yz * yz, axis=1, keepdims=True)
        gated_ref[pl.ds(r0, q), :] = (yz * nw_ref[...]).astype(gated_ref.dtype)

        xdw = (x * wst_f).astype(BF16)
        contrib = jnp.dot(transposed_b(r0), xdw, preferred_element_type=F32)
        state_ref[...] = h_prev * cd_ref[pl.ds(c, 1), :] + contrib
        return carry

    lax.fori_loop(0, n_chunks, fwd_body, 0, unroll=math.gcd(n_chunks, SSD_FWD_UNROLL))


def _ssd_scan(xbc_c, u3, norm_w, acs_tab, exp_tab, seg_rows, dt_rows, cd, dskip):
    b, s, _ = xbc_c.shape
    n_chunks = s // SSD_CHUNK
    gw = SSD_GROUP_WIDTH
    b_first = SSD_D_INNER // SSD_STATE
    c_first = b_first + SSD_GROUPS
    sel, expand = _ssd_constants()
    seg_block = seg_rows.shape[2:]
    est = (3 * 2 * s * gw * 2 + 2 * s * V7X_LANES * 4 + 4 * s * SSD_STATE * 2 + 4 * s * V7X_LANES * 2
           + 2 * 2 * R * s * 4
           + 2 * int(np.prod(seg_block)) * 2 + 2 * (sel.size + expand.size) * 2
           + n_chunks * SSD_STATE * gw * 2 + 4 * n_chunks * gw * 4 + 64 * SSD_CHUNK * gw * 4)
    return pl.pallas_call(
        _ssd_kernel,
        grid=(b, SSD_GROUPS),
        in_specs=[
            pl.BlockSpec((None, s, gw), lambda bi, g: (bi, 0, g)),
            pl.BlockSpec((None, s, SSD_STATE), lambda bi, g: (bi, 0, b_first + g)),
            pl.BlockSpec((None, s, SSD_STATE), lambda bi, g: (bi, 0, c_first + g)),
            pl.BlockSpec((None, s, gw), lambda bi, g: (bi, 0, U_Z // gw + g)),
            pl.BlockSpec((1, gw), lambda bi, g: (0, g)),
            pl.BlockSpec((None, None, s, V7X_LANES), lambda bi, g: (bi, g, 0, 0)),
            pl.BlockSpec((None, None, s, V7X_LANES), lambda bi, g: (bi, g, 0, 0)),
            pl.BlockSpec((None, None) + seg_block, lambda bi, g: (bi, g, 0, 0, 0, 0)),
            pl.BlockSpec((None, None, 2 * R, s), lambda bi, g: (bi, g, 0, 0)),
            pl.BlockSpec((None, None, 2 * n_chunks, gw), lambda bi, g: (bi, g, 0, 0)),
            pl.BlockSpec((None, 1, gw), lambda bi, g: (g, 0, 0)),
            pl.BlockSpec(sel.shape, lambda bi, g: (0, 0, 0)),
            pl.BlockSpec(expand.shape, lambda bi, g: (0, 0, 0)),
        ],
        out_specs=[
            pl.BlockSpec((None, s, gw), lambda bi, g: (bi, 0, g)),
            pl.BlockSpec((None, None, s, 1), lambda bi, g: (bi, g, 0, 0)),
        ],
        out_shape=[
            jax.ShapeDtypeStruct((b, s, SSD_D_INNER), BF16),
            jax.ShapeDtypeStruct((b, SSD_GROUPS, s, 1), F32),
        ],
        scratch_shapes=[
            pltpu.VMEM((n_chunks, SSD_STATE, gw), BF16),
            pltpu.VMEM((SSD_STATE, gw), F32),
        ],
        compiler_params=_cparams(("parallel", "parallel"), est),
        name="ssd_scan",
    )(xbc_c, xbc_c, xbc_c, u3, norm_w.reshape(1, -1).astype(F32), acs_tab, exp_tab, seg_rows, dt_rows, cd, dskip,
      sel, expand)


def _mix_kernel(o_ref, gated_ref, ss_ref, wa_ref, ws_ref, ga_ref, gs_ref, m_ref):
    attn = jnp.dot(o_ref[...], wa_ref[...], preferred_element_type=F32)
    ssd = jnp.dot(gated_ref[...], ws_ref[...], preferred_element_type=F32)
    mean_sq = jnp.sum(ss_ref[...], axis=1, keepdims=True) * (1.0 / gated_ref.shape[1])
    ssd = ssd * lax.rsqrt(mean_sq + EPS)
    mixed = _sigmoid(ga_ref[...].astype(F32)) * attn + _sigmoid(gs_ref[...].astype(F32)) * ssd
    m_ref[...] = mixed.astype(m_ref.dtype)


def _mix(o2, gated2, ss2, u2, wa, ws, tm, tn):
    t = o2.shape[0]
    tm = min(tm, t)
    d_attn, d_ssd = o2.shape[1], gated2.shape[1]
    est = (2 * tm * (d_attn + d_ssd) * 2 + 2 * tm * V7X_LANES * 4 + 2 * (d_attn + d_ssd) * tn * 2
           + 6 * tm * tn * 2 + 6 * tm * tn * 4)
    return pl.pallas_call(
        _mix_kernel,
        grid=(t // tm, D_MODEL // tn),
        in_specs=[
            pl.BlockSpec((tm, d_attn), lambda i, j: (i, 0)),
            pl.BlockSpec((tm, d_ssd), lambda i, j: (i, 0)),
            pl.BlockSpec((tm, ss2.shape[1]), lambda i, j: (i, 0)),
            pl.BlockSpec((d_attn, tn), lambda i, j: (0, j)),
            pl.BlockSpec((d_ssd, tn), lambda i, j: (0, j)),
            pl.BlockSpec((tm, tn), lambda i, j: (i, U_GA // tn + j)),
            pl.BlockSpec((tm, tn), lambda i, j: (i, U_GS // tn + j)),
        ],
        out_specs=pl.BlockSpec((tm, tn), lambda i, j: (i, j)),
        out_shape=jax.ShapeDtypeStruct((t, D_MODEL), BF16),
        compiler_params=_cparams(("parallel", "parallel"), est),
        name="branch_mix",
    )(o2, gated2, ss2, wa, ws, u2, u2)


def _residual_matmul_kernel(a_ref, w_ref, x_ref, o_ref):
    o_ref[...] = x_ref[...] + jnp.dot(a_ref[...], w_ref[...], preferred_element_type=F32)


def _residual_matmul(a, w, x, tm, tn):
    t, k = a.shape
    n = w.shape[1]
    tm, tn = min(tm, t), min(tn, n)
    est = 2 * tm * k * 2 + 2 * k * tn * 2 + 4 * tm * tn * 4
    return pl.pallas_call(
        _residual_matmul_kernel,
        grid=(t // tm, n // tn),
        in_specs=[
            pl.BlockSpec((tm, k), lambda i, j: (i, 0)),
            pl.BlockSpec((k, tn), lambda i, j: (0, j)),
            pl.BlockSpec((tm, tn), lambda i, j: (i, j)),
        ],
        out_specs=pl.BlockSpec((tm, tn), lambda i, j: (i, j)),
        out_shape=jax.ShapeDtypeStruct((t, n), F32),
        compiler_params=_cparams(("parallel", "parallel"), est),
        name="residual_matmul",
    )(a, w, x)


def _ffn_kernel(prev_ref, main_ref, next_ref, g_ref, wg_ref, wv_ref, cwg_ref, cwv_ref, cbg_ref, cbv_ref,
                wd_ref, gf_ref, o_ref, hn_ref):
    j, n_j = pl.program_id(2), pl.num_programs(2)
    tm = main_ref.shape[0]

    @pl.when(j == 0)
    def _():
        ext = _with_halo(prev_ref[...], main_ref[...], next_ref[...])
        hn_ref[...] = (ext * _rms_scale(ext) * g_ref[...]).astype(hn_ref.dtype)
        o_ref[...] = main_ref[...]

    hn = hn_ref[...]
    gate = _conv_rows(jnp.dot(hn, wg_ref[...], preferred_element_type=F32), cwg_ref[...], cbg_ref[...], tm)
    val = _conv_rows(jnp.dot(hn, wv_ref[...], preferred_element_type=F32), cwv_ref[...], cbv_ref[...], tm)
    act = (_silu(gate) * val).astype(BF16)
    o_ref[...] += jnp.dot(act, wd_ref[...], preferred_element_type=F32)

    @pl.when(j == n_j - 1)
    def _():
        h = o_ref[...]
        o_ref[...] = h * _rms_scale(h) * gf_ref[...]


def _ffn(h3, gain, w_up, conv_w, conv_b, w_down, final_gain, tm, tf):
    b, s, d = h3.shape
    tm = min(tm, s)
    n_f = FFN_DIM // tf
    row = lambda bi, i, j: (0, 0)
    est = (2 * (tm + 2 * HALO) * d * 4 + (tm + 2 * HALO) * d * 2 + 2 * 2 * d * tf * 2 + 2 * tf * d * 2
           + 2 * tm * d * 4 + 8 * (tm + 2 * HALO) * tf * 4)
    return pl.pallas_call(
        _ffn_kernel,
        grid=(b, s // tm, n_f),
        in_specs=_halo_specs(s, tm, d, lambda j: 0) + [
            pl.BlockSpec((1, d), row),
            pl.BlockSpec((d, tf), lambda bi, i, j: (0, j)),
            pl.BlockSpec((d, tf), lambda bi, i, j: (0, n_f + j)),
            pl.BlockSpec((FFN_CONV, tf), lambda bi, i, j: (0, j)),
            pl.BlockSpec((FFN_CONV, tf), lambda bi, i, j: (0, n_f + j)),
            pl.BlockSpec((1, tf), lambda bi, i, j: (0, j)),
            pl.BlockSpec((1, tf), lambda bi, i, j: (0, n_f + j)),
            pl.BlockSpec((tf, d), lambda bi, i, j: (j, 0)),
            pl.BlockSpec((1, d), row),
        ],
        out_specs=pl.BlockSpec((None, tm, d), lambda bi, i, j: (bi, i, 0)),
        out_shape=jax.ShapeDtypeStruct((b, s, d), F32),
        scratch_shapes=[pltpu.VMEM((tm + 2 * HALO, d), BF16)],
        compiler_params=_cparams(("parallel", "parallel", "arbitrary"), est),
        name="conv_ffn",
    )(h3, h3, h3, gain.reshape(1, -1).astype(F32), w_up, w_up, conv_w.astype(F32), conv_w.astype(F32),
      conv_b.reshape(1, -1).astype(F32), conv_b.reshape(1, -1).astype(F32), w_down,
      final_gain.reshape(1, -1).astype(F32))


def _swap_halves(w):
    half = w.shape[-1] // 2
    return jnp.concatenate([w[..., half:], w[..., :half]], axis=-1)


def _layer(h, cos_t, sin_t, norm_mix_w, w_in, q_norm_w, w_uq, kv_norm_w, w_ukv, w_o_attn, ssd_conv_w,
           ssd_conv_b, a_log_fwd, a_log_bwd, dt_bias_fwd, dt_bias_bwd, ssd_d, ssd_norm_w, w_o_ssd, w_out):
    b, s, d = h.shape
    t = b * s
    n_chunks = s // SSD_CHUNK
    x2 = h.reshape(t, d)

    w_in_stack, layer = w_in
    w_in_t_stack = jnp.swapaxes(w_in_stack, 1, 2)
    assert w_in_t_stack.shape[1] == IN_WIDTH
    w_t = w_in_t_stack[layer]
    half = QK_ROPE_DIM // 2
    w_small_t = jnp.concatenate([w_t[IN_QL:IN_Z], w_t[IN_KR + half:IN_Z], w_t[IN_KR:IN_KR + half],
                                 w_t[IN_DTF:IN_GA]], axis=0)
    w_small_t = lax.optimization_barrier(w_small_t).astype(BF16)
    u2 = _norm_matmul(x2, norm_mix_w, _main_in_weights(w_in_t_stack, layer), BF16, tm=1024, tn=1024)
    small2 = _norm_matmul(x2, norm_mix_w, w_small_t, F32, tm=1024, tn=SM_WIDTH)
    u3 = u2.reshape(b, s, U_WIDTH)
    small3 = small2.reshape(b, s, SM_WIDTH)

    wq = w_uq.reshape(Q_LORA_RANK, MLA_HEADS, QK_NOPE_DIM + QK_ROPE_DIM)
    wq_rope = wq[..., QK_NOPE_DIM:]
    wq = jnp.concatenate([wq[..., :QK_NOPE_DIM], wq_rope, _swap_halves(wq_rope)], axis=-1)
    wq = wq.transpose(1, 0, 2).astype(BF16)
    wkv = w_ukv.reshape(KV_LORA_RANK, MLA_HEADS, QK_NOPE_DIM + V_HEAD_DIM).transpose(1, 0, 2).astype(BF16)
    q, k, vt = _mla_projections(small3, cos_t, sin_t, q_norm_w, wq, kv_norm_w, wkv, tm=512)
    o = _attention(q, k, vt, tq=1024, tk=1024)

    xbc_c = _ssd_conv(u3, ssd_conv_w, ssd_conv_b, tm=512, tc=512)
    g_ = SSD_GROUPS

    def per_group(v):
        lead = v.shape[:-1]
        v = jnp.moveaxis(v.reshape(lead + (2, g_, R)), -2, 0)
        return v.reshape((g_,) + lead + (2 * R,))

    dt_raw = per_group(small3[:, :, SM_DT:]).transpose(1, 0, 3, 2)
    a_log = per_group(jnp.concatenate([a_log_fwd, a_log_bwd]).astype(F32))[..., None]
    dt_bias = per_group(jnp.concatenate([dt_bias_fwd, dt_bias_bwd]).astype(F32))[..., None]
    acs_tab, exp_tab, seg_rows, dt_heads, cd = _dt_prep(dt_raw, a_log, dt_bias)
    seg_rows = seg_rows.reshape(b, g_, 2 * N_PAIRS, 2, V7X_BF16_SUBLANES, s)
    cd = cd[..., ::SSD_CHUNK].reshape(b, g_, 2, R, n_chunks).transpose(0, 1, 2, 4, 3)
    cd = jnp.repeat(cd, SSD_HEAD_DIM, axis=-1).reshape(b, g_, 2 * n_chunks, SSD_GROUP_WIDTH)
    dskip = jnp.repeat(ssd_d.astype(F32), SSD_HEAD_DIM).reshape(g_, 1, SSD_GROUP_WIDTH)
    gated, ss = _ssd_scan(xbc_c, u3, ssd_norm_w, acs_tab, exp_tab, seg_rows, dt_heads, cd, dskip)
    ss2 = ss.reshape(b, g_, s).transpose(0, 2, 1).reshape(t, g_)

    mixed = _mix(o.reshape(t, -1), gated.reshape(t, -1), ss2, u2, w_o_attn.astype(BF16),
                 w_o_ssd.astype(BF16), tm=1024, tn=512)
    h1 = _residual_matmul(mixed, w_out.astype(BF16), x2, tm=1024, tn=1024)
    return h1.reshape(b, s, d)


def kernel(x, positions, norm_mix_w, w_in, q_norm_w, w_uq, kv_norm_w, w_ukv, w_o_attn, ssd_conv_w, ssd_conv_b,
           a_log_fwd, a_log_bwd, dt_bias_fwd, dt_bias_bwd, ssd_d, ssd_norm_w, w_o_ssd, w_out, norm_ffn_w,
           ffn_w_up, ffn_conv_w, ffn_conv_b, ffn_w_down, norm_final_w):
    depth = w_in.shape[0]
    assert depth == 1, "the conv-gated MLP kernel fuses the final RMSNorm, which is only valid for one layer"
    half = QK_ROPE_DIM // 2
    inv_freq = ROPE_THETA ** (-jnp.arange(half, dtype=F32) / half)
    ang = positions.astype(F32)[..., None] * inv_freq
    cos, sin = jnp.cos(ang), jnp.sin(ang)
    pad = jnp.zeros(cos.shape[:-1] + (V7X_LANES - QK_ROPE_DIM,), F32)
    cos_t = jnp.concatenate([cos, cos, pad], axis=-1)
    sin_t = jnp.concatenate([-sin, sin, pad], axis=-1)

    l = 0
    h1 = _layer(x, cos_t, sin_t, norm_mix_w[l], (w_in, l), q_norm_w[l], w_uq[l], kv_norm_w[l], w_ukv[l],
                w_o_attn[l], ssd_conv_w[l], ssd_conv_b[l], a_log_fwd[l], a_log_bwd[l], dt_bias_fwd[l],
                dt_bias_bwd[l], ssd_d[l], ssd_norm_w[l], w_o_ssd[l], w_out[l])
    return _ffn(h1, norm_ffn_w[l], ffn_w_up[l].astype(BF16), ffn_conv_w[l], ffn_conv_b[l],
                ffn_w_down[l].astype(BF16), norm_final_w, tm=512, tf=512)
```

```python
import functools
import math

import jax
import jax.numpy as jnp
import numpy as np
from jax import lax
from jax.experimental import pallas as pl
from jax.experimental.pallas import tpu as pltpu

D_MODEL = 2048
MLA_HEADS = 16
Q_LORA_RANK = 512
KV_LORA_RANK = 512
QK_NOPE_DIM = 128
QK_ROPE_DIM = 64
V_HEAD_DIM = 128
ROPE_THETA = 10000.0
SSD_D_INNER = 2 * D_MODEL
SSD_HEAD_DIM = 64
SSD_HEADS = SSD_D_INNER // SSD_HEAD_DIM
SSD_GROUPS = 8
SSD_HEADS_PER_GROUP = SSD_HEADS // SSD_GROUPS
SSD_STATE = 128
SSD_CONV = 5
SSD_CHUNK = 128
SSD_CONV_DIM = SSD_D_INNER + 2 * SSD_GROUPS * SSD_STATE
SSD_GROUP_WIDTH = SSD_HEADS_PER_GROUP * SSD_HEAD_DIM
FFN_DIM = 5632
FFN_CONV = 3
EPS = 1e-6

V7X_LANES = 128
V7X_BF16_SUBLANES = 16
V7X_VMEM_BYTES = 64 * 1024 * 1024
VMEM_LIMIT_CAP = V7X_VMEM_BYTES - 8 * 1024 * 1024

F32 = jnp.float32
BF16 = jnp.bfloat16

IN_QL = 0
IN_KVL = IN_QL + Q_LORA_RANK
IN_KR = IN_KVL + KV_LORA_RANK
IN_Z = IN_KR + QK_ROPE_DIM
IN_XBC = IN_Z + SSD_D_INNER
IN_DTF = IN_XBC + SSD_CONV_DIM
IN_DTB = IN_DTF + SSD_HEADS
IN_GA = IN_DTB + SSD_HEADS
IN_GS = IN_GA + D_MODEL
IN_WIDTH = IN_GS + D_MODEL

U_Z = 0
U_XBC = U_Z + SSD_D_INNER
U_GA = U_XBC + SSD_CONV_DIM
U_GS = U_GA + D_MODEL
U_WIDTH = U_GS + D_MODEL
SM_QL = 0
SM_KVL = SM_QL + Q_LORA_RANK
SM_ROPE = SM_KVL + KV_LORA_RANK
SM_DT = SM_ROPE + 2 * QK_ROPE_DIM
SM_WIDTH = SM_DT + 2 * SSD_HEADS


def _cparams(semantics, vmem_estimate_bytes):
    limit = int(min(max(vmem_estimate_bytes * 5 // 4, 16 * 1024 * 1024), VMEM_LIMIT_CAP))
    return pltpu.CompilerParams(dimension_semantics=semantics, vmem_limit_bytes=limit)


def _rms_scale(x):
    return lax.rsqrt(jnp.mean(x * x, axis=-1, keepdims=True) + EPS)


def _silu(x):
    return x * (1.0 / (1.0 + jnp.exp(-x)))


def _sigmoid(x):
    return 1.0 / (1.0 + jnp.exp(-x))


W_PREP_BLOCK = 512


def _w_prep_kernel(w_ref, o_ref):
    o_ref[...] = w_ref[0].astype(o_ref.dtype)


def _main_in_weights(w_in_t_stack, layer):
    _, _, k = w_in_t_stack.shape
    n_first = (IN_DTF - IN_Z) // W_PREP_BLOCK

    unit = math.gcd(IN_Z, W_PREP_BLOCK, IN_GA - IN_DTF)

    def src_row(j):
        units = IN_Z // unit + (W_PREP_BLOCK // unit) * j + jnp.where(j >= n_first, (IN_GA - IN_DTF) // unit, 0)
        return units * unit

    est = 2 * W_PREP_BLOCK * k * (4 + 2) + 2 * W_PREP_BLOCK * k * 4
    return pl.pallas_call(
        _w_prep_kernel,
        grid=(U_WIDTH // W_PREP_BLOCK,),
        in_specs=[pl.BlockSpec((pl.Element(1), pl.Element(W_PREP_BLOCK), pl.Element(k)),
                               lambda j: (layer, src_row(j), 0))],
        out_specs=pl.BlockSpec((W_PREP_BLOCK, k), lambda j: (j, 0)),
        out_shape=jax.ShapeDtypeStruct((U_WIDTH, k), BF16),
        compiler_params=_cparams(("parallel",), est),
        name="w_in_prep",
    )(w_in_t_stack)


def _norm_matmul_kernel(x_ref, g_ref, w_ref, o_ref, xn_ref):
    @pl.when(pl.program_id(1) == 0)
    def _():
        x = x_ref[...].astype(F32)
        xn_ref[...] = (x * _rms_scale(x) * g_ref[...]).astype(xn_ref.dtype)

    o_ref[...] = lax.dot_general(xn_ref[...], w_ref[...], (((1,), (1,)), ((), ())),
                                 preferred_element_type=F32).astype(o_ref.dtype)


def _norm_matmul(x, gain, w_t, out_dtype, tm, tn):
    t, k = x.shape
    n = w_t.shape[0]
    tm, tn = min(tm, t), min(tn, n)
    est = 2 * tm * k * x.dtype.itemsize + tm * k * 2 + 2 * k * tn * 2 + 2 * tm * tn * 4
    return pl.pallas_call(
        _norm_matmul_kernel,
        grid=(t // tm, n // tn),
        in_specs=[
            pl.BlockSpec((tm, k), lambda i, j: (i, 0)),
            pl.BlockSpec((1, k), lambda i, j: (0, 0)),
            pl.BlockSpec((tn, k), lambda i, j: (j, 0)),
        ],
        out_specs=pl.BlockSpec((tm, tn), lambda i, j: (i, j)),
        out_shape=jax.ShapeDtypeStruct((t, n), out_dtype),
        scratch_shapes=[pltpu.VMEM((tm, k), BF16)],
        compiler_params=_cparams(("parallel", "arbitrary"), est),
        name="norm_matmul",
    )(x, gain.reshape(1, k).astype(F32), w_t)


def _rope_half(y2, cos_t, sin_t):
    return y2 * cos_t + pltpu.roll(y2, QK_ROPE_DIM, 1) * sin_t


def _q_proj_kernel(ql_ref, g_ref, w_ref, cos_ref, sin_ref, o_ref, *, scale):
    x = ql_ref[...].astype(F32)
    xn = (x * _rms_scale(x) * g_ref[...]).astype(BF16)
    cos_t = cos_ref[...] * scale
    sin_t = sin_ref[...] * scale
    for h in range(MLA_HEADS):
        y = jnp.dot(xn, w_ref[h], preferred_element_type=F32)
        o_ref[h, :, :QK_NOPE_DIM] = (y[:, :QK_NOPE_DIM] * scale).astype(o_ref.dtype)
        o_ref[h, :, QK_NOPE_DIM:] = _rope_half(y[:, QK_NOPE_DIM:], cos_t, sin_t).astype(o_ref.dtype)


def _kv_proj_kernel(kvl_ref, g_ref, w_ref, kr_ref, cos_ref, sin_ref, k_ref, vt_ref):
    x = kvl_ref[...].astype(F32)
    xn = (x * _rms_scale(x) * g_ref[...]).astype(BF16)
    roped = _rope_half(kr_ref[...], cos_ref[...], sin_ref[...]).astype(k_ref.dtype)
    for h in range(MLA_HEADS):
        y = jnp.dot(xn, w_ref[h], preferred_element_type=F32)
        k_ref[h, :, :QK_NOPE_DIM] = y[:, :QK_NOPE_DIM].astype(k_ref.dtype)
        k_ref[h, :, QK_NOPE_DIM:] = roped
        vt_ref[h] = y[:, QK_NOPE_DIM:].T.astype(vt_ref.dtype)


def _mla_projections(small3, cos_t, sin_t, q_gain, wq, kv_gain, wkv, tm):
    b, s, _ = small3.shape
    tm = min(tm, s)
    grid = (b, s // tm)
    head_w = 2 * V7X_LANES
    scale = (QK_NOPE_DIM + QK_ROPE_DIM) ** -0.5 * math.log2(math.e)
    table = pl.BlockSpec((None, tm, V7X_LANES), lambda bi, i: (bi, i, 0))
    gain = pl.BlockSpec((1, Q_LORA_RANK), lambda bi, i: (0, 0))
    wspec = pl.BlockSpec((MLA_HEADS, Q_LORA_RANK, head_w), lambda bi, i: (0, 0, 0))
    est = (2 * tm * Q_LORA_RANK * 4 + 2 * MLA_HEADS * Q_LORA_RANK * head_w * 2
           + 2 * MLA_HEADS * tm * (head_w + V_HEAD_DIM) * 2 + 8 * tm * V7X_LANES * 4)
    params = _cparams(("parallel", "parallel"), est)
    q = pl.pallas_call(
        functools.partial(_q_proj_kernel, scale=scale),
        grid=grid,
        in_specs=[
            pl.BlockSpec((None, tm, Q_LORA_RANK), lambda bi, i: (bi, i, SM_QL // Q_LORA_RANK)),
            gain, wspec, table, table,
        ],
        out_specs=pl.BlockSpec((None, MLA_HEADS, tm, head_w), lambda bi, i: (bi, 0, i, 0)),
        out_shape=jax.ShapeDtypeStruct((b, MLA_HEADS, s, head_w), BF16),
        compiler_params=params,
        name="mla_q_proj",
    )(small3, q_gain.reshape(1, -1).astype(F32), wq, cos_t, sin_t)
    k, vt = pl.pallas_call(
        _kv_proj_kernel,
        grid=grid,
        in_specs=[
            pl.BlockSpec((None, tm, KV_LORA_RANK), lambda bi, i: (bi, i, SM_KVL // KV_LORA_RANK)),
            gain, wspec,
            pl.BlockSpec((None, tm, V7X_LANES), lambda bi, i: (bi, i, SM_ROPE // V7X_LANES)),
            table, table,
        ],
        out_specs=[
            pl.BlockSpec((None, MLA_HEADS, tm, head_w), lambda bi, i: (bi, 0, i, 0)),
            pl.BlockSpec((None, MLA_HEADS, V_HEAD_DIM, tm), lambda bi, i: (bi, 0, 0, i)),
        ],
        out_shape=[
            jax.ShapeDtypeStruct((b, MLA_HEADS, s, head_w), BF16),
            jax.ShapeDtypeStruct((b, MLA_HEADS, V_HEAD_DIM, s), BF16),
        ],
        compiler_params=params,
        name="mla_kv_proj",
    )(small3, kv_gain.reshape(1, -1).astype(F32), wkv, small3, cos_t, sin_t)
    return q, k, vt


ATTN_QUERY_GROUPS = 4


def _attn_kernel(q_ref, k_ref, vt_ref, o_ref, acc_ref, st0_ref, st1_ref, p0_ref, p1_ref, *, tk):
    tq = q_ref.shape[0]
    n_k = k_ref.shape[0] // tk
    width = tq // math.gcd(tq // V7X_LANES, ATTN_QUERY_GROUPS)
    groups = [slice(c, c + width) for c in range(0, tq, width)]
    st_refs, p_refs = (st0_ref, st1_ref), (p0_ref, p1_ref)

    def scores(j):
        for g in groups:
            st_refs[j % 2][:, g] = lax.dot_general(k_ref[j * tk:(j + 1) * tk, :], q_ref[g, :],
                                                   (((1,), (1,)), ((), ())), preferred_element_type=F32)

    ms = [jnp.full((1, width), -jnp.inf, F32) for _ in groups]
    ls = [jnp.zeros((1, width), F32) for _ in groups]
    scores(0)
    for j in range(n_k):
        if j + 1 < n_k:
            scores(j + 1)
        for i, g in enumerate(groups):
            st = st_refs[j % 2][:, g]
            m_new = jnp.maximum(ms[i], jnp.max(st, axis=0, keepdims=True))
            alpha = jnp.exp2(ms[i] - m_new)
            p = jnp.exp2(st - m_new)
            ls[i] = alpha * ls[i] + jnp.sum(p, axis=0, keepdims=True)
            ms[i] = m_new
            p_refs[j % 2][:, g] = p.astype(BF16)
            pv = jnp.dot(vt_ref[:, j * tk:(j + 1) * tk], p_refs[j % 2][:, g], preferred_element_type=F32)
            if j == 0:
                acc_ref[:, g] = pv
            else:
                acc_ref[:, g] = alpha * acc_ref[:, g] + pv
    o_ref[...] = (acc_ref[...] * (1.0 / jnp.concatenate(ls, axis=1))).T.astype(o_ref.dtype)


def _attention(q, k, vt, tq, tk):
    b, h, s, dk = q.shape
    tq, tk = min(tq, s), min(tk, s)
    est = (2 * tq * dk * 2 + 2 * s * dk * 2 + 2 * V_HEAD_DIM * s * 2 + 2 * tq * V_HEAD_DIM * 2
           + V_HEAD_DIM * tq * 4 + 2 * tk * tq * (4 + 2) + 4 * tk * tq * 4)
    scratch = [pltpu.VMEM((V_HEAD_DIM, tq), F32), pltpu.VMEM((tk, tq), F32), pltpu.VMEM((tk, tq), F32),
               pltpu.VMEM((tk, tq), BF16), pltpu.VMEM((tk, tq), BF16)]
    return pl.pallas_call(
        functools.partial(_attn_kernel, tk=tk),
        grid=(b, h, s // tq),
        in_specs=[
            pl.BlockSpec((None, None, tq, dk), lambda bi, hi, i: (bi, hi, i, 0)),
            pl.BlockSpec((None, None, s, dk), lambda bi, hi, i: (bi, hi, 0, 0)),
            pl.BlockSpec((None, None, V_HEAD_DIM, s), lambda bi, hi, i: (bi, hi, 0, 0)),
        ],
        out_specs=pl.BlockSpec((None, tq, V_HEAD_DIM), lambda bi, hi, i: (bi, i, hi)),
        out_shape=jax.ShapeDtypeStruct((b, s, h * V_HEAD_DIM), BF16),
        scratch_shapes=scratch,
        compiler_params=_cparams(("parallel", "parallel", "arbitrary"), est),
        name="mla_attention",
    )(q, k, vt)


HALO = V7X_BF16_SUBLANES


def _with_halo(prev, main, nxt):
    i, n = pl.program_id(1), pl.num_programs(1)
    prev = jnp.where(i > 0, prev, jnp.zeros_like(prev))
    nxt = jnp.where(i < n - 1, nxt, jnp.zeros_like(nxt))
    return jnp.concatenate([prev, main, nxt], axis=0)


def _conv_rows(ext, w, bias, rows):
    taps = w.shape[0]
    acc = bias
    for kk in range(taps):
        off = HALO + kk - (taps - 1) // 2
        acc = acc + ext[off:off + rows] * w[kk:kk + 1]
    return acc


def _halo_specs(s, tm, width, col_block):
    per = tm // HALO
    last = s // HALO - 1

    def prev_map(bi, i, *rest):
        return (bi, jnp.maximum(i * per - 1, 0), col_block(*rest))

    def main_map(bi, i, *rest):
        return (bi, i, col_block(*rest))

    def next_map(bi, i, *rest):
        return (bi, jnp.minimum((i + 1) * per, last), col_block(*rest))

    return [
        pl.BlockSpec((None, HALO, width), prev_map),
        pl.BlockSpec((None, tm, width), main_map),
        pl.BlockSpec((None, HALO, width), next_map),
    ]


def _conv_silu_kernel(prev_ref, main_ref, next_ref, w_ref, b_ref, o_ref):
    ext = _with_halo(prev_ref[...], main_ref[...], next_ref[...]).astype(F32)
    o_ref[...] = _silu(_conv_rows(ext, w_ref[...], b_ref[...], o_ref.shape[0])).astype(o_ref.dtype)


def _ssd_conv(u3, conv_w, conv_b, tm, tc):
    b, s, _ = u3.shape
    tm = min(tm, s)
    n_c = SSD_CONV_DIM // tc
    first = U_XBC // tc
    est = 2 * (tm + 2 * HALO) * tc * 2 + 2 * tm * tc * 2 + 8 * (tm + 2 * HALO) * tc * 4
    return pl.pallas_call(
        _conv_silu_kernel,
        grid=(b, s // tm, n_c),
        in_specs=_halo_specs(s, tm, tc, lambda c: first + c) + [
            pl.BlockSpec((SSD_CONV, tc), lambda bi, i, c: (0, c)),
            pl.BlockSpec((1, tc), lambda bi, i, c: (0, c)),
        ],
        out_specs=pl.BlockSpec((None, tm, tc), lambda bi, i, c: (bi, i, c)),
        out_shape=jax.ShapeDtypeStruct((b, s, SSD_CONV_DIM), BF16),
        compiler_params=_cparams(("parallel", "parallel", "parallel"), est),
        name="ssd_conv_silu",
    )(u3, u3, u3, conv_w.astype(F32), conv_b.reshape(1, -1).astype(F32))


def _softplus(x):
    return jnp.maximum(x, 0.0) + jnp.log1p(jnp.exp(-jnp.abs(x)))


def _chunk_cumsums(a, pos_in_chunk):
    n = a.shape[1]
    cf, cb = a, a
    step = 1
    while step < SSD_CHUNK:
        cf = cf + jnp.where(pos_in_chunk >= step, pltpu.roll(cf, step, 1), 0.0)
        cb = cb + jnp.where(pos_in_chunk < SSD_CHUNK - step, pltpu.roll(cb, n - step, 1), 0.0)
        step *= 2
    return cf, cb


def _split3(x):
    hi = x.astype(BF16)
    rest = x - hi.astype(F32)
    mid = rest.astype(BF16)
    lo = (rest - mid.astype(F32)).astype(BF16)
    return hi, mid, lo


def _dt_prep_kernel(dt_ref, alog_ref, bias_ref, rowsel_ref, acs_ref, exp_ref, seg_ref, dt_out_ref, cd_ref):
    rows, s = dt_ref.shape
    dt = _softplus(dt_ref[...] + bias_ref[...])
    a = -jnp.exp(alog_ref[...]) * dt
    pos = lax.broadcasted_iota(jnp.int32, a.shape, 1) % SSD_CHUNK
    row = lax.broadcasted_iota(jnp.int32, a.shape, 0)
    cf, cb = _chunk_cumsums(a, pos)
    total = cf + cb - a
    acs = jnp.where(row < R, cf, cb)
    dt_out_ref[...] = dt
    cd_ref[...] = jnp.exp(total)
    acs_pieces = [p.astype(F32) for p in _split3(acs)]
    ones_rows = jnp.where(row < PIECES, 1.0, 0.0)
    pad = jnp.zeros((V7X_LANES - (PIECES + 1) * rows, s), F32)
    acs_ref[...] = jnp.concatenate(acs_pieces + [ones_rows, pad], axis=0).T.astype(acs_ref.dtype)
    exp_pieces = [p.astype(F32) for table in (jnp.exp(acs), dt * jnp.exp(total - acs)) for p in _split3(table)]
    pad = jnp.zeros((V7X_LANES - 2 * PIECES * rows, s), F32)
    exp_ref[...] = jnp.concatenate(exp_pieces + [pad], axis=0).T.astype(exp_ref.dtype)
    neg = jnp.concatenate([-p for p in acs_pieces], axis=0).astype(BF16)
    seg_ref[...] = jnp.dot(rowsel_ref[...], neg, preferred_element_type=F32).astype(seg_ref.dtype)


def _dt_prep(dt_raw, a_log, dt_bias):
    b, g, rows, s = dt_raw.shape
    tile = V7X_BF16_SUBLANES
    rowsel = np.zeros((2 * N_PAIRS * 2 * tile, PIECES * rows), np.float32)
    for pair in range(N_PAIRS):
        for d in range(2):
            for half in range(2):
                for piece in range(PIECES):
                    rowsel[((2 * pair + d) * 2 + half) * tile + piece, piece * rows + R * d + 2 * pair + half] = 1.0
    rowsel = jnp.asarray(rowsel, BF16)
    vec = pl.BlockSpec((None, rows, 1), lambda bi, gi: (gi, 0, 0))
    per_group_rows = pl.BlockSpec((None, None, rows, s), lambda bi, gi: (bi, gi, 0, 0))
    per_group_cols = pl.BlockSpec((None, None, s, V7X_LANES), lambda bi, gi: (bi, gi, 0, 0))
    est = 4 * rows * s * 4 + 4 * s * V7X_LANES * 2 + 2 * rowsel.shape[0] * s * 2 + 24 * V7X_LANES * s * 4
    return pl.pallas_call(
        _dt_prep_kernel,
        grid=(b, g),
        in_specs=[per_group_rows, vec, vec, pl.BlockSpec(rowsel.shape, lambda bi, gi: (0, 0))],
        out_specs=[
            per_group_cols,
            per_group_cols,
            pl.BlockSpec((None, None, rowsel.shape[0], s), lambda bi, gi: (bi, gi, 0, 0)),
            per_group_rows,
            per_group_rows,
        ],
        out_shape=[
            jax.ShapeDtypeStruct((b, g, s, V7X_LANES), BF16),
            jax.ShapeDtypeStruct((b, g, s, V7X_LANES), BF16),
            jax.ShapeDtypeStruct((b, g, rowsel.shape[0], s), BF16),
            jax.ShapeDtypeStruct((b, g, rows, s), F32),
            jax.ShapeDtypeStruct((b, g, rows, s), F32),
        ],
        compiler_params=_cparams(("parallel", "parallel"), est),
        name="ssd_dt_prep",
    )(dt_raw, a_log, dt_bias, rowsel)


R = SSD_HEADS_PER_GROUP
N_PAIRS = R // 2
PIECES = 3
ACS_LANES = PIECES * 2 * R
EXPAND_EF, EXPAND_WST = 0, 1
SSD_BWD_CHUNKS_PER_STEP = 8
SSD_FWD_UNROLL = 4


def _ssd_constants():
    sel = np.zeros((2 * N_PAIRS, ACS_LANES, 2 * V7X_LANES), np.float32)
    expand = np.zeros((4, V7X_LANES, SSD_GROUP_WIDTH), np.float32)
    for piece in range(PIECES):
        for d in range(2):
            for pair in range(N_PAIRS):
                for half in range(2):
                    lane = 2 * R * piece + R * d + 2 * pair + half
                    sel[2 * pair + d, lane, half * V7X_LANES:(half + 1) * V7X_LANES] = 1.0
            for t in range(2):
                for r in range(R):
                    lane = ACS_LANES * t + 2 * R * piece + R * d + r
                    expand[2 * t + d, lane, r * SSD_HEAD_DIM:(r + 1) * SSD_HEAD_DIM] = 1.0
    return jnp.asarray(sel, BF16), jnp.asarray(expand, BF16)


def _ssd_kernel(xs_ref, b_ref, c_ref, z_ref, nw_ref, acs_ref, exp_ref, seg_ref, dt_ref, cd_ref, dskip_ref, sel_ref,
                expand_ref, gated_ref, ss_ref, hb_ref, state_ref):
    q = SSD_CHUNK
    n_chunks = xs_ref.shape[0] // q

    def transposed_b(r0):
        return b_ref[pl.ds(r0, q), :].astype(F32).T.astype(BF16)

    def expand_heads(table, which, direction):
        return jnp.dot(table, expand_ref[2 * which + direction], preferred_element_type=F32)

    state_ref[...] = jnp.zeros_like(state_ref)

    n_batch = math.gcd(n_chunks, SSD_BWD_CHUNKS_PER_STEP)

    def bwd_body(i, carry):
        chunks = [n_chunks - 1 - (n_batch * i + k) for k in range(n_batch)]
        starts = [pl.multiple_of(c * q, q) for c in chunks]
        weights = [expand_heads(exp_ref[pl.ds(r0, q), :], EXPAND_WST, 1) for r0 in starts]
        xdws = [(xs_ref[pl.ds(r0, q), :].astype(F32) * w).astype(BF16) for r0, w in zip(starts, weights)]
        contribs = [jnp.dot(transposed_b(r0), xdw, preferred_element_type=F32)
                    for r0, xdw in zip(starts, xdws)]
        state = state_ref[...]
        for c, contrib in zip(chunks, contribs):
            hb_ref[c] = state.astype(hb_ref.dtype)
            state = state * cd_ref[pl.ds(n_chunks + c, 1), :] + contrib
        state_ref[...] = state
        return carry

    lax.fori_loop(0, n_chunks // n_batch, bwd_body, 0)

    state_ref[...] = jnp.zeros_like(state_ref)
    li = lax.broadcasted_iota(jnp.int32, (q, 2 * q), 0)
    si = lax.broadcasted_iota(jnp.int32, (q, 2 * q), 1) % q
    lower, upper = li >= si, li <= si
    lane_low = lax.broadcasted_iota(jnp.int32, (q, V7X_LANES), 1) < SSD_HEAD_DIM
    unused_rows = jnp.zeros((V7X_LANES - ACS_LANES - V7X_BF16_SUBLANES, 2 * q), BF16)

    def fwd_body(c, carry):
        r0 = pl.multiple_of(c * q, q)
        x_bf = xs_ref[pl.ds(r0, q), :]
        x = x_bf.astype(F32)
        bm = b_ref[pl.ds(r0, q), :]
        cm = c_ref[pl.ds(r0, q), :]
        acs_tab = acs_ref[pl.ds(r0, q), :]
        exp_tab = exp_ref[pl.ds(r0, q), :]
        dts = dt_ref[:, pl.ds(r0, q)]
        cb = lax.dot_general(cm, bm, (((1,), (1,)), ((), ())), preferred_element_type=F32)
        cb2 = jnp.concatenate([cb, cb], axis=1)

        def pair_exponents(pair, d):
            k = 2 * pair + d
            tile = jnp.concatenate([seg_ref[k, 0, :, pl.ds(r0, q)], seg_ref[k, 1, :, pl.ds(r0, q)]], axis=1)
            rhs = jnp.concatenate([sel_ref[k], tile, unused_rows], axis=0)
            return jnp.dot(acs_tab, rhs, preferred_element_type=F32)

        def pair_dt(pair, d):
            row = R * d + 2 * pair
            return jnp.concatenate([dts[row:row + 1, :], dts[row + 1:row + 2, :]], axis=1)

        segs = [(pair_exponents(pair, 0), pair_exponents(pair, 1)) for pair in range(N_PAIRS)]
        h_prev = state_ref[...]
        off_f = jnp.dot(cm, h_prev.astype(BF16), preferred_element_type=F32)
        off_b = jnp.dot(cm, hb_ref[c], preferred_element_type=F32)
        ef_f = expand_heads(exp_tab, EXPAND_EF, 0)
        ef_b = expand_heads(exp_tab, EXPAND_EF, 1)
        wst_f = expand_heads(exp_tab, EXPAND_WST, 0)

        diag = []
        for pair in range(N_PAIRS):
            xp = x_bf[:, pair * V7X_LANES:(pair + 1) * V7X_LANES]
            zero = jnp.zeros_like(xp)
            rhs = jnp.concatenate([jnp.where(lane_low, xp, zero), jnp.where(lane_low, zero, xp)], axis=0)
            lf = jnp.where(lower, jnp.exp(segs[pair][0]), 0.0) * pair_dt(pair, 0)
            lb = jnp.where(upper, jnp.exp(segs[pair][1]), 0.0) * pair_dt(pair, 1)
            lhs = (cb2 * (lf + lb)).astype(BF16)
            diag.append(jnp.dot(lhs, rhs, preferred_element_type=F32))
        y = jnp.concatenate(diag, axis=1)
        y = y + off_f * ef_f + off_b * ef_b + x * dskip_ref[...]
        yz = y * _silu(z_ref[pl.ds(r0, q), :].astype(F32))
        ss_ref[pl.ds(r0, q), :] = jnp.sum(yz * yz, axis=1, keepdims=True)
        gated_ref[pl.ds(r0, q), :] = (yz * nw_ref[...]).astype(gated_ref.dtype)

        xdw = (x * wst_f).astype(BF16)
        contrib = jnp.dot(transposed_b(r0), xdw, preferred_element_type=F32)
        state_ref[...] = h_prev * cd_ref[pl.ds(c, 1), :] + contrib
        return carry

    lax.fori_loop(0, n_chunks, fwd_body, 0, unroll=math.gcd(n_chunks, SSD_FWD_UNROLL))


def _ssd_scan(xbc_c, u3, norm_w, acs_tab, exp_tab, seg_rows, dt_rows, cd, dskip):
    b, s, _ = xbc_c.shape
    n_chunks = s // SSD_CHUNK
    gw = SSD_GROUP_WIDTH
    b_first = SSD_D_INNER // SSD_STATE
    c_first = b_first + SSD_GROUPS
    sel, expand = _ssd_constants()
    seg_block = seg_rows.shape[2:]
    est = (3 * 2 * s * gw * 2 + 2 * s * V7X_LANES * 4 + 4 * s * SSD_STATE * 2 + 4 * s * V7X_LANES * 2
           + 2 * 2 * R * s * 4
           + 2 * int(np.prod(seg_block)) * 2 + 2 * (sel.size + expand.size) * 2
           + n_chunks * SSD_STATE * gw * 2 + 4 * n_chunks * gw * 4 + 64 * SSD_CHUNK * gw * 4)
    return pl.pallas_call(
        _ssd_kernel,
        grid=(b, SSD_GROUPS),
        in_specs=[
            pl.BlockSpec((None, s, gw), lambda bi, g: (bi, 0, g)),
            pl.BlockSpec((None, s, SSD_STATE), lambda bi, g: (bi, 0, b_first + g)),
            pl.BlockSpec((None, s, SSD_STATE), lambda bi, g: (bi, 0, c_first + g)),
            pl.BlockSpec((None, s, gw), lambda bi, g: (bi, 0, U_Z // gw + g)),
            pl.BlockSpec((1, gw), lambda bi, g: (0, g)),
            pl.BlockSpec((None, None, s, V7X_LANES), lambda bi, g: (bi, g, 0, 0)),
            pl.BlockSpec((None, None, s, V7X_LANES), lambda bi, g: (bi, g, 0, 0)),
            pl.BlockSpec((None, None) + seg_block, lambda bi, g: (bi, g, 0, 0, 0, 0)),
            pl.BlockSpec((None, None, 2 * R, s), lambda bi, g: (bi, g, 0, 0)),
            pl.BlockSpec((None, None, 2 * n_chunks, gw), lambda bi, g: (bi, g, 0, 0)),
            pl.BlockSpec((None, 1, gw), lambda bi, g: (g, 0, 0)),
            pl.BlockSpec(sel.shape, lambda bi, g: (0, 0, 0)),
            pl.BlockSpec(expand.shape, lambda bi, g: (0, 0, 0)),
        ],
        out_specs=[
            pl.BlockSpec((None, s, gw), lambda bi, g: (bi, 0, g)),
            pl.BlockSpec((None, None, s, 1), lambda bi, g: (bi, g, 0, 0)),
        ],
        out_shape=[
            jax.ShapeDtypeStruct((b, s, SSD_D_INNER), BF16),
            jax.ShapeDtypeStruct((b, SSD_GROUPS, s, 1), F32),
        ],
        scratch_shapes=[
            pltpu.VMEM((n_chunks, SSD_STATE, gw), BF16),
            pltpu.VMEM((SSD_STATE, gw), F32),
        ],
        compiler_params=_cparams(("parallel", "parallel"), est),
        name="ssd_scan",
    )(xbc_c, xbc_c, xbc_c, u3, norm_w.reshape(1, -1).astype(F32), acs_tab, exp_tab, seg_rows, dt_rows, cd, dskip,
      sel, expand)


def _mix_kernel(o_ref, gated_ref, ss_ref, wa_ref, ws_ref, ga_ref, gs_ref, m_ref):
    attn = jnp.dot(o_ref[...], wa_ref[...], preferred_element_type=F32)
    ssd = jnp.dot(gated_ref[...], ws_ref[...], preferred_element_type=F32)
    mean_sq = jnp.sum(ss_ref[...], axis=1, keepdims=True) * (1.0 / gated_ref.shape[1])
    ssd = ssd * lax.rsqrt(mean_sq + EPS)
    mixed = _sigmoid(ga_ref[...].astype(F32)) * attn + _sigmoid(gs_ref[...].astype(F32)) * ssd
    m_ref[...] = mixed.astype(m_ref.dtype)


def _mix(o2, gated2, ss2, u2, wa, ws, tm, tn):
    t = o2.shape[0]
    tm = min(tm, t)
    d_attn, d_ssd = o2.shape[1], gated2.shape[1]
    est = (2 * tm * (d_attn + d_ssd) * 2 + 2 * tm * V7X_LANES * 4 + 2 * (d_attn + d_ssd) * tn * 2
           + 6 * tm * tn * 2 + 6 * tm * tn * 4)
    return pl.pallas_call(
        _mix_kernel,
        grid=(t // tm, D_MODEL // tn),
        in_specs=[
            pl.BlockSpec((tm, d_attn), lambda i, j: (i, 0)),
            pl.BlockSpec((tm, d_ssd), lambda i, j: (i, 0)),
            pl.BlockSpec((tm, ss2.shape[1]), lambda i, j: (i, 0)),
            pl.BlockSpec((d_attn, tn), lambda i, j: (0, j)),
            pl.BlockSpec((d_ssd, tn), lambda i, j: (0, j)),
            pl.BlockSpec((tm, tn), lambda i, j: (i, U_GA // tn + j)),
            pl.BlockSpec((tm, tn), lambda i, j: (i, U_GS // tn + j)),
        ],
        out_specs=pl.BlockSpec((tm, tn), lambda i, j: (i, j)),
        out_shape=jax.ShapeDtypeStruct((t, D_MODEL), BF16),
        compiler_params=_cparams(("parallel", "parallel"), est),
        name="branch_mix",
    )(o2, gated2, ss2, wa, ws, u2, u2)


def _residual_matmul_kernel(a_ref, w_ref, x_ref, o_ref):
    o_ref[...] = x_ref[...] + jnp.dot(a_ref[...], w_ref[...], preferred_element_type=F32)


def _residual_matmul(a, w, x, tm, tn):
    t, k = a.shape
    n = w.shape[1]
    tm, tn = min(tm, t), min(tn, n)
    est = 2 * tm * k * 2 + 2 * k * tn * 2 + 4 * tm * tn * 4
    return pl.pallas_call(
        _residual_matmul_kernel,
        grid=(t // tm, n // tn),
        in_specs=[
            pl.BlockSpec((tm, k), lambda i, j: (i, 0)),
            pl.BlockSpec((k, tn), lambda i, j: (0, j)),
            pl.BlockSpec((tm, tn), lambda i, j: (i, j)),
        ],
        out_specs=pl.BlockSpec((tm, tn), lambda i, j: (i, j)),
        out_shape=jax.ShapeDtypeStruct((t, n), F32),
        compiler_params=_cparams(("parallel", "parallel"), est),
        name="residual_matmul",
    )(a, w, x)


FFN_ROW_PARTS = 2


def _ffn_kernel(prev_ref, main_ref, next_ref, g_ref, wg_ref, wv_ref, cwg_ref, cwv_ref, cbg_ref, cbv_ref,
                wd_ref, gf_ref, o_ref, hn_ref):
    j, n_j = pl.program_id(2), pl.num_programs(2)
    tm = main_ref.shape[0]

    @pl.when(j == 0)
    def _():
        ext = _with_halo(prev_ref[...], main_ref[...], next_ref[...])
        hn_ref[...] = (ext * _rms_scale(ext) * g_ref[...]).astype(hn_ref.dtype)
        o_ref[...] = main_ref[...]

    rows = tm // FFN_ROW_PARTS
    ups = []
    for part in range(FFN_ROW_PARTS):
        hn = hn_ref[part * rows:part * rows + rows + 2 * HALO, :]
        ups.append((jnp.dot(hn, wg_ref[...], preferred_element_type=F32),
                    jnp.dot(hn, wv_ref[...], preferred_element_type=F32)))
    for part, (up_gate, up_val) in enumerate(ups):
        gate = _conv_rows(up_gate, cwg_ref[...], cbg_ref[...], rows)
        val = _conv_rows(up_val, cwv_ref[...], cbv_ref[...], rows)
        act = (_silu(gate) * val).astype(BF16)
        o_ref[part * rows:(part + 1) * rows, :] += jnp.dot(act, wd_ref[...], preferred_element_type=F32)

    @pl.when(j == n_j - 1)
    def _():
        h = o_ref[...]
        o_ref[...] = h * _rms_scale(h) * gf_ref[...]


def _ffn(h3, gain, w_up, conv_w, conv_b, w_down, final_gain, tm, tf):
    b, s, d = h3.shape
    tm = min(tm, s)
    n_f = FFN_DIM // tf
    row = lambda bi, i, j: (0, 0)
    est = (2 * (tm + 2 * HALO) * d * 4 + (tm + 2 * HALO) * d * 2 + 2 * 2 * d * tf * 2 + 2 * tf * d * 2
           + 2 * tm * d * 4 + 8 * (tm + 2 * HALO) * tf * 4)
    return pl.pallas_call(
        _ffn_kernel,
        grid=(b, s // tm, n_f),
        in_specs=_halo_specs(s, tm, d, lambda j: 0) + [
            pl.BlockSpec((1, d), row),
            pl.BlockSpec((d, tf), lambda bi, i, j: (0, j)),
            pl.BlockSpec((d, tf), lambda bi, i, j: (0, n_f + j)),
            pl.BlockSpec((FFN_CONV, tf), lambda bi, i, j: (0, j)),
            pl.BlockSpec((FFN_CONV, tf), lambda bi, i, j: (0, n_f + j)),
            pl.BlockSpec((1, tf), lambda bi, i, j: (0, j)),
            pl.BlockSpec((1, tf), lambda bi, i, j: (0, n_f + j)),
            pl.BlockSpec((tf, d), lambda bi, i, j: (j, 0)),
            pl.BlockSpec((1, d), row),
        ],
        out_specs=pl.BlockSpec((None, tm, d), lambda bi, i, j: (bi, i, 0)),
        out_shape=jax.ShapeDtypeStruct((b, s, d), F32),
        scratch_shapes=[pltpu.VMEM((tm + 2 * HALO, d), BF16)],
        compiler_params=_cparams(("parallel", "parallel", "arbitrary"), est),
        name="conv_ffn",
    )(h3, h3, h3, gain.reshape(1, -1).astype(F32), w_up, w_up, conv_w.astype(F32), conv_w.astype(F32),
      conv_b.reshape(1, -1).astype(F32), conv_b.reshape(1, -1).astype(F32), w_down,
      final_gain.reshape(1, -1).astype(F32))


def _swap_halves(w):
    half = w.shape[-1] // 2
    return jnp.concatenate([w[..., half:], w[..., :half]], axis=-1)


def _layer(h, cos_t, sin_t, norm_mix_w, w_in, q_norm_w, w_uq, kv_norm_w, w_ukv, w_o_attn, ssd_conv_w,
           ssd_conv_b, a_log_fwd, a_log_bwd, dt_bias_fwd, dt_bias_bwd, ssd_d, ssd_norm_w, w_o_ssd, w_out):
    b, s, d = h.shape
    t = b * s
    n_chunks = s // SSD_CHUNK
    x2 = h.reshape(t, d)

    w_in_stack, layer = w_in
    w_in_t_stack = jnp.swapaxes(w_in_stack, 1, 2)
    assert w_in_t_stack.shape[1] == IN_WIDTH
    w_t = w_in_t_stack[layer]
    half = QK_ROPE_DIM // 2
    w_small_t = jnp.concatenate([w_t[IN_QL:IN_Z], w_t[IN_KR + half:IN_Z], w_t[IN_KR:IN_KR + half],
                                 w_t[IN_DTF:IN_GA]], axis=0)
    w_small_t = lax.optimization_barrier(w_small_t).astype(BF16)
    u2 = _norm_matmul(x2, norm_mix_w, _main_in_weights(w_in_t_stack, layer), BF16, tm=1024, tn=1024)
    small2 = _norm_matmul(x2, norm_mix_w, w_small_t, F32, tm=1024, tn=SM_WIDTH)
    u3 = u2.reshape(b, s, U_WIDTH)
    small3 = small2.reshape(b, s, SM_WIDTH)

    wq = w_uq.reshape(Q_LORA_RANK, MLA_HEADS, QK_NOPE_DIM + QK_ROPE_DIM)
    wq_rope = wq[..., QK_NOPE_DIM:]
    wq = jnp.concatenate([wq[..., :QK_NOPE_DIM], wq_rope, _swap_halves(wq_rope)], axis=-1)
    wq = wq.transpose(1, 0, 2).astype(BF16)
    wkv = w_ukv.reshape(KV_LORA_RANK, MLA_HEADS, QK_NOPE_DIM + V_HEAD_DIM).transpose(1, 0, 2).astype(BF16)
    q, k, vt = _mla_projections(small3, cos_t, sin_t, q_norm_w, wq, kv_norm_w, wkv, tm=512)
    o = _attention(q, k, vt, tq=1024, tk=1024)

    xbc_c = _ssd_conv(u3, ssd_conv_w, ssd_conv_b, tm=512, tc=512)
    g_ = SSD_GROUPS

    def per_group(v):
        lead = v.shape[:-1]
        v = jnp.moveaxis(v.reshape(lead + (2, g_, R)), -2, 0)
        return v.reshape((g_,) + lead + (2 * R,))

    dt_raw = per_group(small3[:, :, SM_DT:]).transpose(1, 0, 3, 2)
    a_log = per_group(jnp.concatenate([a_log_fwd, a_log_bwd]).astype(F32))[..., None]
    dt_bias = per_group(jnp.concatenate([dt_bias_fwd, dt_bias_bwd]).astype(F32))[..., None]
    acs_tab, exp_tab, seg_rows, dt_heads, cd = _dt_prep(dt_raw, a_log, dt_bias)
    seg_rows = seg_rows.reshape(b, g_, 2 * N_PAIRS, 2, V7X_BF16_SUBLANES, s)
    cd = cd[..., ::SSD_CHUNK].reshape(b, g_, 2, R, n_chunks).transpose(0, 1, 2, 4, 3)
    cd = jnp.repeat(cd, SSD_HEAD_DIM, axis=-1).reshape(b, g_, 2 * n_chunks, SSD_GROUP_WIDTH)
    dskip = jnp.repeat(ssd_d.astype(F32), SSD_HEAD_DIM).reshape(g_, 1, SSD_GROUP_WIDTH)
    gated, ss = _ssd_scan(xbc_c, u3, ssd_norm_w, acs_tab, exp_tab, seg_rows, dt_heads, cd, dskip)
    ss2 = ss.reshape(b, g_, s).transpose(0, 2, 1).reshape(t, g_)

    mixed = _mix(o.reshape(t, -1), gated.reshape(t, -1), ss2, u2, w_o_attn.astype(BF16),
                 w_o_ssd.astype(BF16), tm=1024, tn=512)
    h1 = _residual_matmul(mixed, w_out.astype(BF16), x2, tm=1024, tn=1024)
    return h1.reshape(b, s, d)


def kernel(x, positions, norm_mix_w, w_in, q_norm_w, w_uq, kv_norm_w, w_ukv, w_o_attn, ssd_conv_w, ssd_conv_b,
           a_log_fwd, a_log_bwd, dt_bias_fwd, dt_bias_bwd, ssd_d, ssd_norm_w, w_o_ssd, w_out, norm_ffn_w,
           ffn_w_up, ffn_conv_w, ffn_conv_b, ffn_w_down, norm_final_w):
    depth = w_in.shape[0]
    assert depth == 1, "the conv-gated MLP kernel fuses the final RMSNorm, which is only valid for one layer"
    half = QK_ROPE_DIM // 2
    inv_freq = ROPE_THETA ** (-jnp.arange(half, dtype=F32) / half)
    ang = positions.astype(F32)[..., None] * inv_freq
    cos, sin = jnp.cos(ang), jnp.sin(ang)
    pad = jnp.zeros(cos.shape[:-1] + (V7X_LANES - QK_ROPE_DIM,), F32)
    cos_t = jnp.concatenate([cos, cos, pad], axis=-1)
    sin_t = jnp.concatenate([-sin, sin, pad], axis=-1)

    l = 0
    h1 = _layer(x, cos_t, sin_t, norm_mix_w[l], (w_in, l), q_norm_w[l], w_uq[l], kv_norm_w[l], w_ukv[l],
                w_o_attn[l], ssd_conv_w[l], ssd_conv_b[l], a_log_fwd[l], a_log_bwd[l], dt_bias_fwd[l],
                dt_bias_bwd[l], ssd_d[l], ssd_norm_w[l], w_o_ssd[l], w_out[l])
    return _ffn(h1, norm_ffn_w[l], ffn_w_up[l].astype(BF16), ffn_conv_w[l], ffn_conv_b[l],
                ffn_w_down[l].astype(BF16), norm_final_w, tm=512, tf=512)
```

```python
import functools
import math

import jax
import jax.numpy as jnp
import numpy as np
from jax import lax
from jax.experimental import pallas as pl
from jax.experimental.pallas import tpu as pltpu

D_MODEL = 2048
MLA_HEADS = 16
Q_LORA_RANK = 512
KV_LORA_RANK = 512
QK_NOPE_DIM = 128
QK_ROPE_DIM = 64
V_HEAD_DIM = 128
ROPE_THETA = 10000.0
SSD_D_INNER = 2 * D_MODEL
SSD_HEAD_DIM = 64
SSD_HEADS = SSD_D_INNER // SSD_HEAD_DIM
SSD_GROUPS = 8
SSD_HEADS_PER_GROUP = SSD_HEADS // SSD_GROUPS
SSD_STATE = 128
SSD_CONV = 5
SSD_CHUNK = 128
SSD_CONV_DIM = SSD_D_INNER + 2 * SSD_GROUPS * SSD_STATE
SSD_GROUP_WIDTH = SSD_HEADS_PER_GROUP * SSD_HEAD_DIM
FFN_DIM = 5632
FFN_CONV = 3
EPS = 1e-6

V7X_LANES = 128
V7X_BF16_SUBLANES = 16
V7X_VMEM_BYTES = 64 * 1024 * 1024
VMEM_LIMIT_CAP = V7X_VMEM_BYTES - 8 * 1024 * 1024

F32 = jnp.float32
BF16 = jnp.bfloat16

IN_QL = 0
IN_KVL = IN_QL + Q_LORA_RANK
IN_KR = IN_KVL + KV_LORA_RANK
IN_Z = IN_KR + QK_ROPE_DIM
IN_XBC = IN_Z + SSD_D_INNER
IN_DTF = IN_XBC + SSD_CONV_DIM
IN_DTB = IN_DTF + SSD_HEADS
IN_GA = IN_DTB + SSD_HEADS
IN_GS = IN_GA + D_MODEL
IN_WIDTH = IN_GS + D_MODEL

U_Z = 0
U_XBC = U_Z + SSD_D_INNER
U_GA = U_XBC + SSD_CONV_DIM
U_GS = U_GA + D_MODEL
U_WIDTH = U_GS + D_MODEL
SM_QL = 0
SM_KVL = SM_QL + Q_LORA_RANK
SM_ROPE = SM_KVL + KV_LORA_RANK
SM_DT = SM_ROPE + 2 * QK_ROPE_DIM
SM_WIDTH = SM_DT + 2 * SSD_HEADS


def _cparams(semantics, vmem_estimate_bytes):
    limit = int(min(max(vmem_estimate_bytes * 5 // 4, 16 * 1024 * 1024), VMEM_LIMIT_CAP))
    return pltpu.CompilerParams(dimension_semantics=semantics, vmem_limit_bytes=limit)


def _rms_scale(x):
    return lax.rsqrt(jnp.mean(x * x, axis=-1, keepdims=True) + EPS)


def _silu(x):
    return x * (1.0 / (1.0 + jnp.exp(-x)))


def _sigmoid(x):
    return 1.0 / (1.0 + jnp.exp(-x))


W_PREP_BLOCK = 512


def _w_prep_kernel(w_ref, o_ref):
    o_ref[...] = w_ref[0].astype(o_ref.dtype)


def _main_in_weights(w_in_t_stack, layer):
    _, _, k = w_in_t_stack.shape
    n_first = (IN_DTF - IN_Z) // W_PREP_BLOCK

    unit = math.gcd(IN_Z, W_PREP_BLOCK, IN_GA - IN_DTF)

    def src_row(j):
        units = IN_Z // unit + (W_PREP_BLOCK // unit) * j + jnp.where(j >= n_first, (IN_GA - IN_DTF) // unit, 0)
        return units * unit

    est = 2 * W_PREP_BLOCK * k * (4 + 2) + 2 * W_PREP_BLOCK * k * 4
    return pl.pallas_call(
        _w_prep_kernel,
        grid=(U_WIDTH // W_PREP_BLOCK,),
        in_specs=[pl.BlockSpec((pl.Element(1), pl.Element(W_PREP_BLOCK), pl.Element(k)),
                               lambda j: (layer, src_row(j), 0))],
        out_specs=pl.BlockSpec((W_PREP_BLOCK, k), lambda j: (j, 0)),
        out_shape=jax.ShapeDtypeStruct((U_WIDTH, k), BF16),
        compiler_params=_cparams(("parallel",), est),
        name="w_in_prep",
    )(w_in_t_stack)


def _norm_matmul_kernel(x_ref, g_ref, w_ref, o_ref, xn_ref):
    @pl.when(pl.program_id(1) == 0)
    def _():
        x = x_ref[...].astype(F32)
        xn_ref[...] = (x * _rms_scale(x) * g_ref[...]).astype(xn_ref.dtype)

    o_ref[...] = lax.dot_general(xn_ref[...], w_ref[...], (((1,), (1,)), ((), ())),
                                 preferred_element_type=F32).astype(o_ref.dtype)


def _norm_matmul(x, gain, w_t, out_dtype, tm, tn):
    t, k = x.shape
    n = w_t.shape[0]
    tm, tn = min(tm, t), min(tn, n)
    est = 2 * tm * k * x.dtype.itemsize + tm * k * 2 + 2 * k * tn * 2 + 2 * tm * tn * 4
    return pl.pallas_call(
        _norm_matmul_kernel,
        grid=(t // tm, n // tn),
        in_specs=[
            pl.BlockSpec((tm, k), lambda i, j: (i, 0)),
            pl.BlockSpec((1, k), lambda i, j: (0, 0)),
            pl.BlockSpec((tn, k), lambda i, j: (j, 0)),
        ],
        out_specs=pl.BlockSpec((tm, tn), lambda i, j: (i, j)),
        out_shape=jax.ShapeDtypeStruct((t, n), out_dtype),
        scratch_shapes=[pltpu.VMEM((tm, k), BF16)],
        compiler_params=_cparams(("parallel", "arbitrary"), est),
        name="norm_matmul",
    )(x, gain.reshape(1, k).astype(F32), w_t)


def _rope_half(y2, cos_t, sin_t):
    return y2 * cos_t + pltpu.roll(y2, QK_ROPE_DIM, 1) * sin_t


def _q_proj_kernel(ql_ref, g_ref, w_ref, cos_ref, sin_ref, o_ref, *, scale):
    x = ql_ref[...].astype(F32)
    xn = (x * _rms_scale(x) * g_ref[...]).astype(BF16)
    cos_t = cos_ref[...] * scale
    sin_t = sin_ref[...] * scale
    for h in range(MLA_HEADS):
        y = jnp.dot(xn, w_ref[h], preferred_element_type=F32)
        o_ref[h, :, :QK_NOPE_DIM] = (y[:, :QK_NOPE_DIM] * scale).astype(o_ref.dtype)
        o_ref[h, :, QK_NOPE_DIM:] = _rope_half(y[:, QK_NOPE_DIM:], cos_t, sin_t).astype(o_ref.dtype)


def _kv_proj_kernel(kvl_ref, g_ref, w_ref, kr_ref, cos_ref, sin_ref, k_ref, vt_ref):
    x = kvl_ref[...].astype(F32)
    xn = (x * _rms_scale(x) * g_ref[...]).astype(BF16)
    roped = _rope_half(kr_ref[...], cos_ref[...], sin_ref[...]).astype(k_ref.dtype)
    for h in range(MLA_HEADS):
        y = jnp.dot(xn, w_ref[h], preferred_element_type=F32)
        k_ref[h, :, :QK_NOPE_DIM] = y[:, :QK_NOPE_DIM].astype(k_ref.dtype)
        k_ref[h, :, QK_NOPE_DIM:] = roped
        vt_ref[h] = y[:, QK_NOPE_DIM:].T.astype(vt_ref.dtype)


def _mla_projections(small3, cos_t, sin_t, q_gain, wq, kv_gain, wkv, tm):
    b, s, _ = small3.shape
    tm = min(tm, s)
    grid = (b, s // tm)
    head_w = 2 * V7X_LANES
    scale = (QK_NOPE_DIM + QK_ROPE_DIM) ** -0.5 * math.log2(math.e)
    table = pl.BlockSpec((None, tm, V7X_LANES), lambda bi, i: (bi, i, 0))
    gain = pl.BlockSpec((1, Q_LORA_RANK), lambda bi, i: (0, 0))
    wspec = pl.BlockSpec((MLA_HEADS, Q_LORA_RANK, head_w), lambda bi, i: (0, 0, 0))
    est = (2 * tm * Q_LORA_RANK * 4 + 2 * MLA_HEADS * Q_LORA_RANK * head_w * 2
           + 2 * MLA_HEADS * tm * (head_w + V_HEAD_DIM) * 2 + 8 * tm * V7X_LANES * 4)
    params = _cparams(("parallel", "parallel"), est)
    q = pl.pallas_call(
        functools.partial(_q_proj_kernel, scale=scale),
        grid=grid,
        in_specs=[
            pl.BlockSpec((None, tm, Q_LORA_RANK), lambda bi, i: (bi, i, SM_QL // Q_LORA_RANK)),
            gain, wspec, table, table,
        ],
        out_specs=pl.BlockSpec((None, MLA_HEADS, tm, head_w), lambda bi, i: (bi, 0, i, 0)),
        out_shape=jax.ShapeDtypeStruct((b, MLA_HEADS, s, head_w), BF16),
        compiler_params=params,
        name="mla_q_proj",
    )(small3, q_gain.reshape(1, -1).astype(F32), wq, cos_t, sin_t)
    k, vt = pl.pallas_call(
        _kv_proj_kernel,
        grid=grid,
        in_specs=[
            pl.BlockSpec((None, tm, KV_LORA_RANK), lambda bi, i: (bi, i, SM_KVL // KV_LORA_RANK)),
            gain, wspec,
            pl.BlockSpec((None, tm, V7X_LANES), lambda bi, i: (bi, i, SM_ROPE // V7X_LANES)),
            table, table,
        ],
        out_specs=[
            pl.BlockSpec((None, MLA_HEADS, tm, head_w), lambda bi, i: (bi, 0, i, 0)),
            pl.BlockSpec((None, MLA_HEADS, V_HEAD_DIM, tm), lambda bi, i: (bi, 0, 0, i)),
        ],
        out_shape=[
            jax.ShapeDtypeStruct((b, MLA_HEADS, s, head_w), BF16),
            jax.ShapeDtypeStruct((b, MLA_HEADS, V_HEAD_DIM, s), BF16),
        ],
        compiler_params=params,
        name="mla_kv_proj",
    )(small3, kv_gain.reshape(1, -1).astype(F32), wkv, small3, cos_t, sin_t)
    return q, k, vt


ATTN_QUERY_GROUPS = 4


def _attn_kernel(q_ref, k_ref, vt_ref, o_ref, acc_ref, st0_ref, st1_ref, p0_ref, p1_ref, *, tk):
    tq = q_ref.shape[0]
    n_k = k_ref.shape[0] // tk
    width = tq // math.gcd(tq // V7X_LANES, ATTN_QUERY_GROUPS)
    groups = [slice(c, c + width) for c in range(0, tq, width)]
    st_refs, p_refs = (st0_ref, st1_ref), (p0_ref, p1_ref)

    def scores(j):
        for g in groups:
            st_refs[j % 2][:, g] = lax.dot_general(k_ref[j * tk:(j + 1) * tk, :], q_ref[g, :],
                                                   (((1,), (1,)), ((), ())), preferred_element_type=F32)

    ms = [jnp.full((1, width), -jnp.inf, F32) for _ in groups]
    ls = [jnp.zeros((1, width), F32) for _ in groups]
    scores(0)
    for j in range(n_k):
        if j + 1 < n_k:
            scores(j + 1)
        for i, g in enumerate(groups):
            st = st_refs[j % 2][:, g]
            m_new = jnp.maximum(ms[i], jnp.max(st, axis=0, keepdims=True))
            alpha = jnp.exp2(ms[i] - m_new)
            p = jnp.exp2(st - m_new)
            ls[i] = alpha * ls[i] + jnp.sum(p, axis=0, keepdims=True)
            ms[i] = m_new
            p_refs[j % 2][:, g] = p.astype(BF16)
            pv = jnp.dot(vt_ref[:, j * tk:(j + 1) * tk], p_refs[j % 2][:, g], preferred_element_type=F32)
            if j == 0:
                acc_ref[:, g] = pv
            else:
                acc_ref[:, g] = alpha * acc_ref[:, g] + pv
    o_ref[...] = (acc_ref[...] * (1.0 / jnp.concatenate(ls, axis=1))).T.astype(o_ref.dtype)


def _attention(q, k, vt, tq, tk):
    b, h, s, dk = q.shape
    tq, tk = min(tq, s), min(tk, s)
    est = (2 * tq * dk * 2 + 2 * s * dk * 2 + 2 * V_HEAD_DIM * s * 2 + 2 * tq * V_HEAD_DIM * 2
           + V_HEAD_DIM * tq * 4 + 2 * tk * tq * (4 + 2) + 4 * tk * tq * 4)
    scratch = [pltpu.VMEM((V_HEAD_DIM, tq), F32), pltpu.VMEM((tk, tq), F32), pltpu.VMEM((tk, tq), F32),
               pltpu.VMEM((tk, tq), BF16), pltpu.VMEM((tk, tq), BF16)]
    return pl.pallas_call(
        functools.partial(_attn_kernel, tk=tk),
        grid=(b, h, s // tq),
        in_specs=[
            pl.BlockSpec((None, None, tq, dk), lambda bi, hi, i: (bi, hi, i, 0)),
            pl.BlockSpec((None, None, s, dk), lambda bi, hi, i: (bi, hi, 0, 0)),
            pl.BlockSpec((None, None, V_HEAD_DIM, s), lambda bi, hi, i: (bi, hi, 0, 0)),
        ],
        out_specs=pl.BlockSpec((None, tq, V_HEAD_DIM), lambda bi, hi, i: (bi, i, hi)),
        out_shape=jax.ShapeDtypeStruct((b, s, h * V_HEAD_DIM), BF16),
        scratch_shapes=scratch,
        compiler_params=_cparams(("parallel", "parallel", "arbitrary"), est),
        name="mla_attention",
    )(q, k, vt)


HALO = V7X_BF16_SUBLANES


def _with_halo(prev, main, nxt):
    i, n = pl.program_id(1), pl.num_programs(1)
    prev = jnp.where(i > 0, prev, jnp.zeros_like(prev))
    nxt = jnp.where(i < n - 1, nxt, jnp.zeros_like(nxt))
    return jnp.concatenate([prev, main, nxt], axis=0)


def _conv_rows(ext, w, bias, rows):
    taps = w.shape[0]
    acc = bias
    for kk in range(taps):
        off = HALO + kk - (taps - 1) // 2
        acc = acc + ext[off:off + rows] * w[kk:kk + 1]
    return acc


def _halo_specs(s, tm, width, col_block):
    per = tm // HALO
    last = s // HALO - 1

    def prev_map(bi, i, *rest):
        return (bi, jnp.maximum(i * per - 1, 0), col_block(*rest))

    def main_map(bi, i, *rest):
        return (bi, i, col_block(*rest))

    def next_map(bi, i, *rest):
        return (bi, jnp.minimum((i + 1) * per, last), col_block(*rest))

    return [
        pl.BlockSpec((None, HALO, width), prev_map),
        pl.BlockSpec((None, tm, width), main_map),
        pl.BlockSpec((None, HALO, width), next_map),
    ]


def _conv_silu_kernel(prev_ref, main_ref, next_ref, w_ref, b_ref, o_ref):
    ext = _with_halo(prev_ref[...], main_ref[...], next_ref[...]).astype(F32)
    o_ref[...] = _silu(_conv_rows(ext, w_ref[...], b_ref[...], o_ref.shape[0])).astype(o_ref.dtype)


def _ssd_conv(u3, conv_w, conv_b, tm, tc):
    b, s, _ = u3.shape
    tm = min(tm, s)
    n_c = SSD_CONV_DIM // tc
    first = U_XBC // tc
    est = 2 * (tm + 2 * HALO) * tc * 2 + 2 * tm * tc * 2 + 8 * (tm + 2 * HALO) * tc * 4
    return pl.pallas_call(
        _conv_silu_kernel,
        grid=(b, s // tm, n_c),
        in_specs=_halo_specs(s, tm, tc, lambda c: first + c) + [
            pl.BlockSpec((SSD_CONV, tc), lambda bi, i, c: (0, c)),
            pl.BlockSpec((1, tc), lambda bi, i, c: (0, c)),
        ],
        out_specs=pl.BlockSpec((None, tm, tc), lambda bi, i, c: (bi, i, c)),
        out_shape=jax.ShapeDtypeStruct((b, s, SSD_CONV_DIM), BF16),
        compiler_params=_cparams(("parallel", "parallel", "parallel"), est),
        name="ssd_conv_silu",
    )(u3, u3, u3, conv_w.astype(F32), conv_b.reshape(1, -1).astype(F32))


LOG_DT_FLOOR = -1e30


def _softplus(x):
    return jnp.maximum(x, 0.0) + jnp.log1p(jnp.exp(-jnp.abs(x)))


def _chunk_cumsums(a, pos_in_chunk):
    n = a.shape[1]
    cf, cb = a, a
    step = 1
    while step < SSD_CHUNK:
        cf = cf + jnp.where(pos_in_chunk >= step, pltpu.roll(cf, step, 1), 0.0)
        cb = cb + jnp.where(pos_in_chunk < SSD_CHUNK - step, pltpu.roll(cb, n - step, 1), 0.0)
        step *= 2
    return cf, cb


def _split3(x):
    hi = x.astype(BF16)
    rest = x - hi.astype(F32)
    mid = rest.astype(BF16)
    lo = (rest - mid.astype(F32)).astype(BF16)
    return hi, mid, lo


def _dt_prep_kernel(dt_ref, alog_ref, bias_ref, rowsel_ref, acs_ref, exp_ref, seg_ref, cd_ref):
    rows, s = dt_ref.shape
    dt = _softplus(dt_ref[...] + bias_ref[...])
    a = -jnp.exp(alog_ref[...]) * dt
    pos = lax.broadcasted_iota(jnp.int32, a.shape, 1) % SSD_CHUNK
    row = lax.broadcasted_iota(jnp.int32, a.shape, 0)
    cf, cb = _chunk_cumsums(a, pos)
    total = cf + cb - a
    acs = jnp.where(row < R, cf, cb)
    cd_ref[...] = jnp.exp(total)
    acs_pieces = [p.astype(F32) for p in _split3(acs)]
    ones_rows = jnp.where(row < PIECES, 1.0, 0.0)
    pad = jnp.zeros((V7X_LANES - (PIECES + 1) * rows, s), F32)
    acs_ref[...] = jnp.concatenate(acs_pieces + [ones_rows, pad], axis=0).T.astype(acs_ref.dtype)
    exp_pieces = [p.astype(F32) for table in (jnp.exp(acs), dt * jnp.exp(total - acs)) for p in _split3(table)]
    pad = jnp.zeros((V7X_LANES - 2 * PIECES * rows, s), F32)
    exp_ref[...] = jnp.concatenate(exp_pieces + [pad], axis=0).T.astype(exp_ref.dtype)
    src = acs - jnp.maximum(jnp.log(dt), LOG_DT_FLOOR)
    neg = jnp.concatenate([-p.astype(F32) for p in _split3(src)], axis=0).astype(BF16)
    seg_ref[...] = jnp.dot(rowsel_ref[...], neg, preferred_element_type=F32).astype(seg_ref.dtype)


def _dt_prep(dt_raw, a_log, dt_bias):
    b, g, rows, s = dt_raw.shape
    tile = V7X_BF16_SUBLANES
    rowsel = np.zeros((2 * N_PAIRS * 2 * tile, PIECES * rows), np.float32)
    for pair in range(N_PAIRS):
        for d in range(2):
            for half in range(2):
                for piece in range(PIECES):
                    rowsel[((2 * pair + d) * 2 + half) * tile + piece, piece * rows + R * d + 2 * pair + half] = 1.0
    rowsel = jnp.asarray(rowsel, BF16)
    vec = pl.BlockSpec((None, rows, 1), lambda bi, gi: (gi, 0, 0))
    per_group_rows = pl.BlockSpec((None, None, rows, s), lambda bi, gi: (bi, gi, 0, 0))
    per_group_cols = pl.BlockSpec((None, None, s, V7X_LANES), lambda bi, gi: (bi, gi, 0, 0))
    est = 4 * rows * s * 4 + 4 * s * V7X_LANES * 2 + 2 * rowsel.shape[0] * s * 2 + 24 * V7X_LANES * s * 4
    return pl.pallas_call(
        _dt_prep_kernel,
        grid=(b, g),
        in_specs=[per_group_rows, vec, vec, pl.BlockSpec(rowsel.shape, lambda bi, gi: (0, 0))],
        out_specs=[
            per_group_cols,
            per_group_cols,
            pl.BlockSpec((None, None, rowsel.shape[0], s), lambda bi, gi: (bi, gi, 0, 0)),
            per_group_rows,
        ],
        out_shape=[
            jax.ShapeDtypeStruct((b, g, s, V7X_LANES), BF16),
            jax.ShapeDtypeStruct((b, g, s, V7X_LANES), BF16),
            jax.ShapeDtypeStruct((b, g, rowsel.shape[0], s), BF16),
            jax.ShapeDtypeStruct((b, g, rows, s), F32),
        ],
        compiler_params=_cparams(("parallel", "parallel"), est),
        name="ssd_dt_prep",
    )(dt_raw, a_log, dt_bias, rowsel)


R = SSD_HEADS_PER_GROUP
N_PAIRS = R // 2
PIECES = 3
ACS_LANES = PIECES * 2 * R
EXPAND_EF, EXPAND_WST = 0, 1
SSD_BWD_CHUNKS_PER_STEP = 8
SSD_FWD_UNROLL = 4


def _ssd_constants():
    sel = np.zeros((2 * N_PAIRS, ACS_LANES, 2 * V7X_LANES), np.float32)
    expand = np.zeros((4, V7X_LANES, SSD_GROUP_WIDTH), np.float32)
    for piece in range(PIECES):
        for d in range(2):
            for pair in range(N_PAIRS):
                for half in range(2):
                    lane = 2 * R * piece + R * d + 2 * pair + half
                    sel[2 * pair + d, lane, half * V7X_LANES:(half + 1) * V7X_LANES] = 1.0
            for t in range(2):
                for r in range(R):
                    lane = ACS_LANES * t + 2 * R * piece + R * d + r
                    expand[2 * t + d, lane, r * SSD_HEAD_DIM:(r + 1) * SSD_HEAD_DIM] = 1.0
    return jnp.asarray(sel, BF16), jnp.asarray(expand, BF16)


def _ssd_kernel(xs_ref, b_ref, c_ref, z_ref, nw_ref, acs_ref, exp_ref, seg_ref, cd_ref, dskip_ref, sel_ref,
                expand_ref, gated_ref, ss_ref, hb_ref, state_ref):
    q = SSD_CHUNK
    n_chunks = xs_ref.shape[0] // q

    def transposed_b(r0):
        return b_ref[pl.ds(r0, q), :].astype(F32).T.astype(BF16)

    def expand_heads(table, which, direction):
        return jnp.dot(table, expand_ref[2 * which + direction], preferred_element_type=F32)

    state_ref[...] = jnp.zeros_like(state_ref)

    n_batch = math.gcd(n_chunks, SSD_BWD_CHUNKS_PER_STEP)

    def bwd_body(i, carry):
        chunks = [n_chunks - 1 - (n_batch * i + k) for k in range(n_batch)]
        starts = [pl.multiple_of(c * q, q) for c in chunks]
        weights = [expand_heads(exp_ref[pl.ds(r0, q), :], EXPAND_WST, 1) for r0 in starts]
        xdws = [(xs_ref[pl.ds(r0, q), :].astype(F32) * w).astype(BF16) for r0, w in zip(starts, weights)]
        contribs = [jnp.dot(transposed_b(r0), xdw, preferred_element_type=F32)
                    for r0, xdw in zip(starts, xdws)]
        state = state_ref[...]
        for c, contrib in zip(chunks, contribs):
            hb_ref[c] = state.astype(hb_ref.dtype)
            state = state * cd_ref[pl.ds(n_chunks + c, 1), :] + contrib
        state_ref[...] = state
        return carry

    lax.fori_loop(0, n_chunks // n_batch, bwd_body, 0)

    state_ref[...] = jnp.zeros_like(state_ref)
    li = lax.broadcasted_iota(jnp.int32, (q, 2 * q), 0)
    si = lax.broadcasted_iota(jnp.int32, (q, 2 * q), 1) % q
    lower, upper = li >= si, li <= si
    lane_low = lax.broadcasted_iota(jnp.int32, (q, V7X_LANES), 1) < SSD_HEAD_DIM
    unused_rows = jnp.zeros((V7X_LANES - ACS_LANES - V7X_BF16_SUBLANES, 2 * q), BF16)

    def fwd_body(c, carry):
        r0 = pl.multiple_of(c * q, q)
        x_bf = xs_ref[pl.ds(r0, q), :]
        x = x_bf.astype(F32)
        bm = b_ref[pl.ds(r0, q), :]
        cm = c_ref[pl.ds(r0, q), :]
        acs_tab = acs_ref[pl.ds(r0, q), :]
        exp_tab = exp_ref[pl.ds(r0, q), :]
        cb = lax.dot_general(cm, bm, (((1,), (1,)), ((), ())), preferred_element_type=F32)
        cb2 = jnp.concatenate([cb, cb], axis=1)

        def pair_exponents(pair, d):
            k = 2 * pair + d
            tile = jnp.concatenate([seg_ref[k, 0, :, pl.ds(r0, q)], seg_ref[k, 1, :, pl.ds(r0, q)]], axis=1)
            rhs = jnp.concatenate([sel_ref[k], tile, unused_rows], axis=0)
            return jnp.dot(acs_tab, rhs, preferred_element_type=F32)

        segs = [(pair_exponents(pair, 0), pair_exponents(pair, 1)) for pair in range(N_PAIRS)]
        h_prev = state_ref[...]
        off_f = jnp.dot(cm, h_prev.astype(BF16), preferred_element_type=F32)
        off_b = jnp.dot(cm, hb_ref[c], preferred_element_type=F32)
        ef_f = expand_heads(exp_tab, EXPAND_EF, 0)
        ef_b = expand_heads(exp_tab, EXPAND_EF, 1)
        wst_f = expand_heads(exp_tab, EXPAND_WST, 0)

        diag = []
        for pair in range(N_PAIRS):
            xp = x_bf[:, pair * V7X_LANES:(pair + 1) * V7X_LANES]
            zero = jnp.zeros_like(xp)
            rhs = jnp.concatenate([jnp.where(lane_low, xp, zero), jnp.where(lane_low, zero, xp)], axis=0)
            lf = jnp.where(lower, jnp.exp(segs[pair][0]), 0.0)
            lb = jnp.where(upper, jnp.exp(segs[pair][1]), 0.0)
            lhs = (cb2 * (lf + lb)).astype(BF16)
            diag.append(jnp.dot(lhs, rhs, preferred_element_type=F32))
        y = jnp.concatenate(diag, axis=1)
        y = y + off_f * ef_f + off_b * ef_b + x * dskip_ref[...]
        yz = y * _silu(z_ref[pl.ds(r0, q), :].astype(F32))
        ss_ref[pl.ds(r0, q), :] = jnp.sum(yz * yz, axis=1, keepdims=True)
        gated_ref[pl.ds(r0, q), :] = (yz * nw_ref[...]).astype(gated_ref.dtype)

        xdw = (x * wst_f).astype(BF16)
        contrib = jnp.dot(transposed_b(r0), xdw, preferred_element_type=F32)
        state_ref[...] = h_prev * cd_ref[pl.ds(c, 1), :] + contrib
        return carry

    lax.fori_loop(0, n_chunks, fwd_body, 0, unroll=math.gcd(n_chunks, SSD_FWD_UNROLL))


def _ssd_scan(xbc_c, u3, norm_w, acs_tab, exp_tab, seg_rows, cd, dskip):
    b, s, _ = xbc_c.shape
    n_chunks = s // SSD_CHUNK
    gw = SSD_GROUP_WIDTH
    b_first = SSD_D_INNER // SSD_STATE
    c_first = b_first + SSD_GROUPS
    sel, expand = _ssd_constants()
    seg_block = seg_rows.shape[2:]
    est = (3 * 2 * s * gw * 2 + 2 * s * V7X_LANES * 4 + 4 * s * SSD_STATE * 2 + 4 * s * V7X_LANES * 2
           + 2 * 2 * R * s * 4
           + 2 * int(np.prod(seg_block)) * 2 + 2 * (sel.size + expand.size) * 2
           + n_chunks * SSD_STATE * gw * 2 + 4 * n_chunks * gw * 4 + 64 * SSD_CHUNK * gw * 4)
    return pl.pallas_call(
        _ssd_kernel,
        grid=(b, SSD_GROUPS),
        in_specs=[
            pl.BlockSpec((None, s, gw), lambda bi, g: (bi, 0, g)),
            pl.BlockSpec((None, s, SSD_STATE), lambda bi, g: (bi, 0, b_first + g)),
            pl.BlockSpec((None, s, SSD_STATE), lambda bi, g: (bi, 0, c_first + g)),
            pl.BlockSpec((None, s, gw), lambda bi, g: (bi, 0, U_Z // gw + g)),
            pl.BlockSpec((1, gw), lambda bi, g: (0, g)),
            pl.BlockSpec((None, None, s, V7X_LANES), lambda bi, g: (bi, g, 0, 0)),
            pl.BlockSpec((None, None, s, V7X_LANES), lambda bi, g: (bi, g, 0, 0)),
            pl.BlockSpec((None, None) + seg_block, lambda bi, g: (bi, g, 0, 0, 0, 0)),
            pl.BlockSpec((None, None, 2 * n_chunks, gw), lambda bi, g: (bi, g, 0, 0)),
            pl.BlockSpec((None, 1, gw), lambda bi, g: (g, 0, 0)),
            pl.BlockSpec(sel.shape, lambda bi, g: (0, 0, 0)),
            pl.BlockSpec(expand.shape, lambda bi, g: (0, 0, 0)),
        ],
        out_specs=[
            pl.BlockSpec((None, s, gw), lambda bi, g: (bi, 0, g)),
            pl.BlockSpec((None, None, s, 1), lambda bi, g: (bi, g, 0, 0)),
        ],
        out_shape=[
            jax.ShapeDtypeStruct((b, s, SSD_D_INNER), BF16),
            jax.ShapeDtypeStruct((b, SSD_GROUPS, s, 1), F32),
        ],
        scratch_shapes=[
            pltpu.VMEM((n_chunks, SSD_STATE, gw), BF16),
            pltpu.VMEM((SSD_STATE, gw), F32),
        ],
        compiler_params=_cparams(("parallel", "parallel"), est),
        name="ssd_scan",
    )(xbc_c, xbc_c, xbc_c, u3, norm_w.reshape(1, -1).astype(F32), acs_tab, exp_tab, seg_rows, cd, dskip,
      sel, expand)


def _mix_kernel(o_ref, gated_ref, ss_ref, wa_ref, ws_ref, ga_ref, gs_ref, m_ref):
    attn = jnp.dot(o_ref[...], wa_ref[...], preferred_element_type=F32)
    ssd = jnp.dot(gated_ref[...], ws_ref[...], preferred_element_type=F32)
    mean_sq = jnp.sum(ss_ref[...], axis=1, keepdims=True) * (1.0 / gated_ref.shape[1])
    ssd = ssd * lax.rsqrt(mean_sq + EPS)
    mixed = _sigmoid(ga_ref[...].astype(F32)) * attn + _sigmoid(gs_ref[...].astype(F32)) * ssd
    m_ref[...] = mixed.astype(m_ref.dtype)


def _mix(o2, gated2, ss2, u2, wa, ws, tm, tn):
    t = o2.shape[0]
    tm = min(tm, t)
    d_attn, d_ssd = o2.shape[1], gated2.shape[1]
    est = (2 * tm * (d_attn + d_ssd) * 2 + 2 * tm * V7X_LANES * 4 + 2 * (d_attn + d_ssd) * tn * 2
           + 6 * tm * tn * 2 + 6 * tm * tn * 4)
    return pl.pallas_call(
        _mix_kernel,
        grid=(t // tm, D_MODEL // tn),
        in_specs=[
            pl.BlockSpec((tm, d_attn), lambda i, j: (i, 0)),
            pl.BlockSpec((tm, d_ssd), lambda i, j: (i, 0)),
            pl.BlockSpec((tm, ss2.shape[1]), lambda i, j: (i, 0)),
            pl.BlockSpec((d_attn, tn), lambda i, j: (0, j)),
            pl.BlockSpec((d_ssd, tn), lambda i, j: (0, j)),
            pl.BlockSpec((tm, tn), lambda i, j: (i, U_GA // tn + j)),
            pl.BlockSpec((tm, tn), lambda i, j: (i, U_GS // tn + j)),
        ],
        out_specs=pl.BlockSpec((tm, tn), lambda i, j: (i, j)),
        out_shape=jax.ShapeDtypeStruct((t, D_MODEL), BF16),
        compiler_params=_cparams(("parallel", "parallel"), est),
        name="branch_mix",
    )(o2, gated2, ss2, wa, ws, u2, u2)


def _residual_matmul_kernel(a_ref, w_ref, x_ref, o_ref):
    o_ref[...] = x_ref[...] + jnp.dot(a_ref[...], w_ref[...], preferred_element_type=F32)


def _residual_matmul(a, w, x, tm, tn):
    t, k = a.shape
    n = w.shape[1]
    tm, tn = min(tm, t), min(tn, n)
    est = 2 * tm * k * 2 + 2 * k * tn * 2 + 4 * tm * tn * 4
    return pl.pallas_call(
        _residual_matmul_kernel,
        grid=(t // tm, n // tn),
        in_specs=[
            pl.BlockSpec((tm, k), lambda i, j: (i, 0)),
            pl.BlockSpec((k, tn), lambda i, j: (0, j)),
            pl.BlockSpec((tm, tn), lambda i, j: (i, j)),
        ],
        out_specs=pl.BlockSpec((tm, tn), lambda i, j: (i, j)),
        out_shape=jax.ShapeDtypeStruct((t, n), F32),
        compiler_params=_cparams(("parallel", "parallel"), est),
        name="residual_matmul",
    )(a, w, x)


def _ffn_kernel(prev_ref, main_ref, next_ref, g_ref, wg_ref, wv_ref, cwg_ref, cwv_ref, cbg_ref, cbv_ref,
                wd_ref, gf_ref, o_ref, hn_ref):
    j, n_j = pl.program_id(2), pl.num_programs(2)
    tm = main_ref.shape[0]

    @pl.when(j == 0)
    def _():
        ext = _with_halo(prev_ref[...], main_ref[...], next_ref[...])
        hn_ref[...] = (ext * _rms_scale(ext) * g_ref[...]).astype(hn_ref.dtype)
        o_ref[...] = main_ref[...]

    hn = hn_ref[...]
    gate = _conv_rows(jnp.dot(hn, wg_ref[...], preferred_element_type=F32), cwg_ref[...], cbg_ref[...], tm)
    val = _conv_rows(jnp.dot(hn, wv_ref[...], preferred_element_type=F32), cwv_ref[...], cbv_ref[...], tm)
    act = (_silu(gate) * val).astype(BF16)
    o_ref[...] += jnp.dot(act, wd_ref[...], preferred_element_type=F32)

    @pl.when(j == n_j - 1)
    def _():
        h = o_ref[...]
        o_ref[...] = h * _rms_scale(h) * gf_ref[...]


def _ffn(h3, gain, w_up, conv_w, conv_b, w_down, final_gain, tm, tf):
    b, s, d = h3.shape
    tm = min(tm, s)
    n_f = FFN_DIM // tf
    row = lambda bi, i, j: (0, 0)
    est = (2 * (tm + 2 * HALO) * d * 4 + (tm + 2 * HALO) * d * 2 + 2 * 2 * d * tf * 2 + 2 * tf * d * 2
           + 2 * tm * d * 4 + 8 * (tm + 2 * HALO) * tf * 4)
    return pl.pallas_call(
        _ffn_kernel,
        grid=(b, s // tm, n_f),
        in_specs=_halo_specs(s, tm, d, lambda j: 0) + [
            pl.BlockSpec((1, d), row),
            pl.BlockSpec((d, tf), lambda bi, i, j: (0, j)),
            pl.BlockSpec((d, tf), lambda bi, i, j: (0, n_f + j)),
            pl.BlockSpec((FFN_CONV, tf), lambda bi, i, j: (0, j)),
            pl.BlockSpec((FFN_CONV, tf), lambda bi, i, j: (0, n_f + j)),
            pl.BlockSpec((1, tf), lambda bi, i, j: (0, j)),
            pl.BlockSpec((1, tf), lambda bi, i, j: (0, n_f + j)),
            pl.BlockSpec((tf, d), lambda bi, i, j: (j, 0)),
            pl.BlockSpec((1, d), row),
        ],
        out_specs=pl.BlockSpec((None, tm, d), lambda bi, i, j: (bi, i, 0)),
        out_shape=jax.ShapeDtypeStruct((b, s, d), F32),
        scratch_shapes=[pltpu.VMEM((tm + 2 * HALO, d), BF16)],
        compiler_params=_cparams(("parallel", "parallel", "arbitrary"), est),
        name="conv_ffn",
    )(h3, h3, h3, gain.reshape(1, -1).astype(F32), w_up, w_up, conv_w.astype(F32), conv_w.astype(F32),
      conv_b.reshape(1, -1).astype(F32), conv_b.reshape(1, -1).astype(F32), w_down,
      final_gain.reshape(1, -1).astype(F32))


def _swap_halves(w):
    half = w.shape[-1] // 2
    return jnp.concatenate([w[..., half:], w[..., :half]], axis=-1)


def _layer(h, cos_t, sin_t, norm_mix_w, w_in, q_norm_w, w_uq, kv_norm_w, w_ukv, w_o_attn, ssd_conv_w,
           ssd_conv_b, a_log_fwd, a_log_bwd, dt_bias_fwd, dt_bias_bwd, ssd_d, ssd_norm_w, w_o_ssd, w_out):
    b, s, d = h.shape
    t = b * s
    n_chunks = s // SSD_CHUNK
    x2 = h.reshape(t, d)

    w_in_stack, layer = w_in
    w_in_t_stack = jnp.swapaxes(w_in_stack, 1, 2)
    assert w_in_t_stack.shape[1] == IN_WIDTH
    w_t = w_in_t_stack[layer]
    half = QK_ROPE_DIM // 2
    w_small_t = jnp.concatenate([w_t[IN_QL:IN_Z], w_t[IN_KR + half:IN_Z], w_t[IN_KR:IN_KR + half],
                                 w_t[IN_DTF:IN_GA]], axis=0)
    w_small_t = lax.optimization_barrier(w_small_t).astype(BF16)
    u2 = _norm_matmul(x2, norm_mix_w, _main_in_weights(w_in_t_stack, layer), BF16, tm=1024, tn=1024)
    small2 = _norm_matmul(x2, norm_mix_w, w_small_t, F32, tm=1024, tn=SM_WIDTH)
    u3 = u2.reshape(b, s, U_WIDTH)
    small3 = small2.reshape(b, s, SM_WIDTH)

    wq = w_uq.reshape(Q_LORA_RANK, MLA_HEADS, QK_NOPE_DIM + QK_ROPE_DIM)
    wq_rope = wq[..., QK_NOPE_DIM:]
    wq = jnp.concatenate([wq[..., :QK_NOPE_DIM], wq_rope, _swap_halves(wq_rope)], axis=-1)
    wq = wq.transpose(1, 0, 2).astype(BF16)
    wkv = w_ukv.reshape(KV_LORA_RANK, MLA_HEADS, QK_NOPE_DIM + V_HEAD_DIM).transpose(1, 0, 2).astype(BF16)
    q, k, vt = _mla_projections(small3, cos_t, sin_t, q_norm_w, wq, kv_norm_w, wkv, tm=512)
    o = _attention(q, k, vt, tq=1024, tk=1024)

    xbc_c = _ssd_conv(u3, ssd_conv_w, ssd_conv_b, tm=512, tc=512)
    g_ = SSD_GROUPS

    def per_group(v):
        lead = v.shape[:-1]
        v = jnp.moveaxis(v.reshape(lead + (2, g_, R)), -2, 0)
        return v.reshape((g_,) + lead + (2 * R,))

    dt_raw = per_group(small3[:, :, SM_DT:]).transpose(1, 0, 3, 2)
    a_log = per_group(jnp.concatenate([a_log_fwd, a_log_bwd]).astype(F32))[..., None]
    dt_bias = per_group(jnp.concatenate([dt_bias_fwd, dt_bias_bwd]).astype(F32))[..., None]
    acs_tab, exp_tab, seg_rows, cd = _dt_prep(dt_raw, a_log, dt_bias)
    seg_rows = seg_rows.reshape(b, g_, 2 * N_PAIRS, 2, V7X_BF16_SUBLANES, s)
    cd = cd[..., ::SSD_CHUNK].reshape(b, g_, 2, R, n_chunks).transpose(0, 1, 2, 4, 3)
    cd = jnp.repeat(cd, SSD_HEAD_DIM, axis=-1).reshape(b, g_, 2 * n_chunks, SSD_GROUP_WIDTH)
    dskip = jnp.repeat(ssd_d.astype(F32), SSD_HEAD_DIM).reshape(g_, 1, SSD_GROUP_WIDTH)
    gated, ss = _ssd_scan(xbc_c, u3, ssd_norm_w, acs_tab, exp_tab, seg_rows, cd, dskip)
    ss2 = ss.reshape(b, g_, s).transpose(0, 2, 1).reshape(t, g_)

    mixed = _mix(o.reshape(t, -1), gated.reshape(t, -1), ss2, u2, w_o_attn.astype(BF16),
                 w_o_ssd.astype(BF16), tm=1024, tn=512)
    h1 = _residual_matmul(mixed, w_out.astype(BF16), x2, tm=1024, tn=1024)
    return h1.reshape(b, s, d)


def kernel(x, positions, norm_mix_w, w_in, q_norm_w, w_uq, kv_norm_w, w_ukv, w_o_attn, ssd_conv_w, ssd_conv_b,
           a_log_fwd, a_log_bwd, dt_bias_fwd, dt_bias_bwd, ssd_d, ssd_norm_w, w_o_ssd, w_out, norm_ffn_w,
           ffn_w_up, ffn_conv_w, ffn_conv_b, ffn_w_down, norm_final_w):
    depth = w_in.shape[0]
    assert depth == 1, "the conv-gated MLP kernel fuses the final RMSNorm, which is only valid for one layer"
    half = QK_ROPE_DIM // 2
    inv_freq = ROPE_THETA ** (-jnp.arange(half, dtype=F32) / half)
    ang = positions.astype(F32)[..., None] * inv_freq
    cos, sin = jnp.cos(ang), jnp.sin(ang)
    pad = jnp.zeros(cos.shape[:-1] + (V7X_LANES - QK_ROPE_DIM,), F32)
    cos_t = jnp.concatenate([cos, cos, pad], axis=-1)
    sin_t = jnp.concatenate([-sin, sin, pad], axis=-1)

    l = 0
    h1 = _layer(x, cos_t, sin_t, norm_mix_w[l], (w_in, l), q_norm_w[l], w_uq[l], kv_norm_w[l], w_ukv[l],
                w_o_attn[l], ssd_conv_w[l], ssd_conv_b[l], a_log_fwd[l], a_log_bwd[l], dt_bias_fwd[l],
                dt_bias_bwd[l], ssd_d[l], ssd_norm_w[l], w_o_ssd[l], w_out[l])
    return _ffn(h1, norm_ffn_w[l], ffn_w_up[l].astype(BF16), ffn_conv_w[l], ffn_conv_b[l],
                ffn_w_down[l].astype(BF16), norm_final_w, tm=512, tf=512)
```

```python
import functools
import math

import jax
import jax.numpy as jnp
import numpy as np
from jax import lax
from jax.experimental import pallas as pl
from jax.experimental.pallas import tpu as pltpu

D_MODEL = 2048
MLA_HEADS = 16
Q_LORA_RANK = 512
KV_LORA_RANK = 512
QK_NOPE_DIM = 128
QK_ROPE_DIM = 64
V_HEAD_DIM = 128
ROPE_THETA = 10000.0
SSD_D_INNER = 2 * D_MODEL
SSD_HEAD_DIM = 64
SSD_HEADS = SSD_D_INNER // SSD_HEAD_DIM
SSD_GROUPS = 8
SSD_HEADS_PER_GROUP = SSD_HEADS // SSD_GROUPS
SSD_STATE = 128
SSD_CONV = 5
SSD_CHUNK = 128
SSD_CONV_DIM = SSD_D_INNER + 2 * SSD_GROUPS * SSD_STATE
SSD_GROUP_WIDTH = SSD_HEADS_PER_GROUP * SSD_HEAD_DIM
FFN_DIM = 5632
FFN_CONV = 3
EPS = 1e-6

V7X_LANES = 128
V7X_BF16_SUBLANES = 16
V7X_VMEM_BYTES = 64 * 1024 * 1024
VMEM_LIMIT_CAP = V7X_VMEM_BYTES - 8 * 1024 * 1024

F32 = jnp.float32
BF16 = jnp.bfloat16

IN_QL = 0
IN_KVL = IN_QL + Q_LORA_RANK
IN_KR = IN_KVL + KV_LORA_RANK
IN_Z = IN_KR + QK_ROPE_DIM
IN_XBC = IN_Z + SSD_D_INNER
IN_DTF = IN_XBC + SSD_CONV_DIM
IN_DTB = IN_DTF + SSD_HEADS
IN_GA = IN_DTB + SSD_HEADS
IN_GS = IN_GA + D_MODEL
IN_WIDTH = IN_GS + D_MODEL

U_Z = 0
U_XBC = U_Z + SSD_D_INNER
U_GA = U_XBC + SSD_CONV_DIM
U_GS = U_GA + D_MODEL
U_WIDTH = U_GS + D_MODEL
SM_QL = 0
SM_KVL = SM_QL + Q_LORA_RANK
SM_ROPE = SM_KVL + KV_LORA_RANK
SM_DT = SM_ROPE + 2 * QK_ROPE_DIM
SM_WIDTH = SM_DT + 2 * SSD_HEADS


def _cparams(semantics, vmem_estimate_bytes):
    limit = int(min(max(vmem_estimate_bytes * 5 // 4, 16 * 1024 * 1024), VMEM_LIMIT_CAP))
    return pltpu.CompilerParams(dimension_semantics=semantics, vmem_limit_bytes=limit)


def _rms_scale(x):
    return lax.rsqrt(jnp.mean(x * x, axis=-1, keepdims=True) + EPS)


def _silu(x):
    return x * (1.0 / (1.0 + jnp.exp(-x)))


def _sigmoid(x):
    return 1.0 / (1.0 + jnp.exp(-x))


W_PREP_BLOCK = 512


def _w_prep_kernel(w_ref, o_ref):
    o_ref[...] = w_ref[0].astype(o_ref.dtype)


def _main_in_weights(w_in_t_stack, layer):
    _, _, k = w_in_t_stack.shape
    n_first = (IN_DTF - IN_Z) // W_PREP_BLOCK

    unit = math.gcd(IN_Z, W_PREP_BLOCK, IN_GA - IN_DTF)

    def src_row(j):
        units = IN_Z // unit + (W_PREP_BLOCK // unit) * j + jnp.where(j >= n_first, (IN_GA - IN_DTF) // unit, 0)
        return units * unit

    est = 2 * W_PREP_BLOCK * k * (4 + 2) + 2 * W_PREP_BLOCK * k * 4
    return pl.pallas_call(
        _w_prep_kernel,
        grid=(U_WIDTH // W_PREP_BLOCK,),
        in_specs=[pl.BlockSpec((pl.Element(1), pl.Element(W_PREP_BLOCK), pl.Element(k)),
                               lambda j: (layer, src_row(j), 0))],
        out_specs=pl.BlockSpec((W_PREP_BLOCK, k), lambda j: (j, 0)),
        out_shape=jax.ShapeDtypeStruct((U_WIDTH, k), BF16),
        compiler_params=_cparams(("parallel",), est),
        name="w_in_prep",
    )(w_in_t_stack)


def _norm_matmul_kernel(x_ref, g_ref, w_ref, o_ref, xn_ref):
    @pl.when(pl.program_id(1) == 0)
    def _():
        x = x_ref[...].astype(F32)
        xn_ref[...] = (x * _rms_scale(x) * g_ref[...]).astype(xn_ref.dtype)

    o_ref[...] = lax.dot_general(xn_ref[...], w_ref[...], (((1,), (1,)), ((), ())),
                                 preferred_element_type=F32).astype(o_ref.dtype)


def _norm_matmul(x, gain, w_t, out_dtype, tm, tn):
    t, k = x.shape
    n = w_t.shape[0]
    tm, tn = min(tm, t), min(tn, n)
    est = 2 * tm * k * x.dtype.itemsize + tm * k * 2 + 2 * k * tn * 2 + 2 * tm * tn * 4
    return pl.pallas_call(
        _norm_matmul_kernel,
        grid=(t // tm, n // tn),
        in_specs=[
            pl.BlockSpec((tm, k), lambda i, j: (i, 0)),
            pl.BlockSpec((1, k), lambda i, j: (0, 0)),
            pl.BlockSpec((tn, k), lambda i, j: (j, 0)),
        ],
        out_specs=pl.BlockSpec((tm, tn), lambda i, j: (i, j)),
        out_shape=jax.ShapeDtypeStruct((t, n), out_dtype),
        scratch_shapes=[pltpu.VMEM((tm, k), BF16)],
        compiler_params=_cparams(("parallel", "arbitrary"), est),
        name="norm_matmul",
    )(x, gain.reshape(1, k).astype(F32), w_t)


def _rope_half(y2, cos_t, sin_t):
    return y2 * cos_t + pltpu.roll(y2, QK_ROPE_DIM, 1) * sin_t


def _q_proj_kernel(ql_ref, g_ref, w_ref, cos_ref, sin_ref, o_ref, *, scale):
    x = ql_ref[...].astype(F32)
    xn = (x * _rms_scale(x) * g_ref[...]).astype(BF16)
    cos_t = cos_ref[...] * scale
    sin_t = sin_ref[...] * scale
    for h in range(MLA_HEADS):
        y = jnp.dot(xn, w_ref[h], preferred_element_type=F32)
        o_ref[h, :, :QK_NOPE_DIM] = (y[:, :QK_NOPE_DIM] * scale).astype(o_ref.dtype)
        o_ref[h, :, QK_NOPE_DIM:] = _rope_half(y[:, QK_NOPE_DIM:], cos_t, sin_t).astype(o_ref.dtype)


def _kv_proj_kernel(kvl_ref, g_ref, w_ref, kr_ref, cos_ref, sin_ref, k_ref, vt_ref):
    x = kvl_ref[...].astype(F32)
    xn = (x * _rms_scale(x) * g_ref[...]).astype(BF16)
    roped = _rope_half(kr_ref[...], cos_ref[...], sin_ref[...]).astype(k_ref.dtype)
    for h in range(MLA_HEADS):
        y = jnp.dot(xn, w_ref[h], preferred_element_type=F32)
        k_ref[h, :, :QK_NOPE_DIM] = y[:, :QK_NOPE_DIM].astype(k_ref.dtype)
        k_ref[h, :, QK_NOPE_DIM:] = roped
        vt_ref[h] = y[:, QK_NOPE_DIM:].T.astype(vt_ref.dtype)


def _mla_projections(small3, cos_t, sin_t, q_gain, wq, kv_gain, wkv, tm):
    b, s, _ = small3.shape
    tm = min(tm, s)
    grid = (b, s // tm)
    head_w = 2 * V7X_LANES
    scale = (QK_NOPE_DIM + QK_ROPE_DIM) ** -0.5 * math.log2(math.e)
    table = pl.BlockSpec((None, tm, V7X_LANES), lambda bi, i: (bi, i, 0))
    gain = pl.BlockSpec((1, Q_LORA_RANK), lambda bi, i: (0, 0))
    wspec = pl.BlockSpec((MLA_HEADS, Q_LORA_RANK, head_w), lambda bi, i: (0, 0, 0))
    est = (2 * tm * Q_LORA_RANK * 4 + 2 * MLA_HEADS * Q_LORA_RANK * head_w * 2
           + 2 * MLA_HEADS * tm * (head_w + V_HEAD_DIM) * 2 + 8 * tm * V7X_LANES * 4)
    params = _cparams(("parallel", "parallel"), est)
    q = pl.pallas_call(
        functools.partial(_q_proj_kernel, scale=scale),
        grid=grid,
        in_specs=[
            pl.BlockSpec((None, tm, Q_LORA_RANK), lambda bi, i: (bi, i, SM_QL // Q_LORA_RANK)),
            gain, wspec, table, table,
        ],
        out_specs=pl.BlockSpec((None, MLA_HEADS, tm, head_w), lambda bi, i: (bi, 0, i, 0)),
        out_shape=jax.ShapeDtypeStruct((b, MLA_HEADS, s, head_w), BF16),
        compiler_params=params,
        name="mla_q_proj",
    )(small3, q_gain.reshape(1, -1).astype(F32), wq, cos_t, sin_t)
    k, vt = pl.pallas_call(
        _kv_proj_kernel,
        grid=grid,
        in_specs=[
            pl.BlockSpec((None, tm, KV_LORA_RANK), lambda bi, i: (bi, i, SM_KVL // KV_LORA_RANK)),
            gain, wspec,
            pl.BlockSpec((None, tm, V7X_LANES), lambda bi, i: (bi, i, SM_ROPE // V7X_LANES)),
            table, table,
        ],
        out_specs=[
            pl.BlockSpec((None, MLA_HEADS, tm, head_w), lambda bi, i: (bi, 0, i, 0)),
            pl.BlockSpec((None, MLA_HEADS, V_HEAD_DIM, tm), lambda bi, i: (bi, 0, 0, i)),
        ],
        out_shape=[
            jax.ShapeDtypeStruct((b, MLA_HEADS, s, head_w), BF16),
            jax.ShapeDtypeStruct((b, MLA_HEADS, V_HEAD_DIM, s), BF16),
        ],
        compiler_params=params,
        name="mla_kv_proj",
    )(small3, kv_gain.reshape(1, -1).astype(F32), wkv, small3, cos_t, sin_t)
    return q, k, vt


ATTN_QUERY_GROUPS = 4
ATTN_HEADS_PER_STEP = 2
ATTN_SCRATCH_PER_HEAD = 5


def _attn_kernel(q_ref, k_ref, vt_ref, o_ref, *scratch, tk):
    n_heads, tq, _ = q_ref.shape
    n_k = k_ref.shape[1] // tk
    width = tq // math.gcd(tq // V7X_LANES, ATTN_QUERY_GROUPS)
    groups = [slice(c, c + width) for c in range(0, tq, width)]
    heads = []
    for hd in range(n_heads):
        acc_ref, st0_ref, st1_ref, p0_ref, p1_ref = scratch[hd * ATTN_SCRATCH_PER_HEAD:(hd + 1) * ATTN_SCRATCH_PER_HEAD]
        heads.append(dict(acc=acc_ref, st=(st0_ref, st1_ref), p=(p0_ref, p1_ref),
                          m=[jnp.full((1, width), -jnp.inf, F32) for _ in groups],
                          l=[jnp.zeros((1, width), F32) for _ in groups]))

    def scores(hd, j):
        for g in groups:
            heads[hd]["st"][j % 2][:, g] = lax.dot_general(
                k_ref[hd, j * tk:(j + 1) * tk, :], q_ref[hd, g, :], (((1,), (1,)), ((), ())),
                preferred_element_type=F32)

    def softmax_and_pv(hd, j):
        head = heads[hd]
        for i, g in enumerate(groups):
            st = head["st"][j % 2][:, g]
            m_new = jnp.maximum(head["m"][i], jnp.max(st, axis=0, keepdims=True))
            alpha = jnp.exp2(head["m"][i] - m_new)
            p = jnp.exp2(st - m_new)
            head["l"][i] = alpha * head["l"][i] + jnp.sum(p, axis=0, keepdims=True)
            head["m"][i] = m_new
            head["p"][j % 2][:, g] = p.astype(BF16)
            pv = jnp.dot(vt_ref[hd, :, j * tk:(j + 1) * tk], head["p"][j % 2][:, g],
                         preferred_element_type=F32)
            if j == 0:
                head["acc"][:, g] = pv
            else:
                head["acc"][:, g] = alpha * head["acc"][:, g] + pv

    stages = [(hd, j) for hd in range(n_heads) for j in range(n_k)]
    scores(*stages[0])
    for idx, stage in enumerate(stages):
        if idx + 1 < len(stages):
            scores(*stages[idx + 1])
        softmax_and_pv(*stage)
    for hd, head in enumerate(heads):
        out = head["acc"][...] * (1.0 / jnp.concatenate(head["l"], axis=1))
        o_ref[:, hd * V_HEAD_DIM:(hd + 1) * V_HEAD_DIM] = out.T.astype(o_ref.dtype)


def _attention(q, k, vt, tq, tk):
    b, h, s, dk = q.shape
    tq, tk = min(tq, s), min(tk, s)
    hp = ATTN_HEADS_PER_STEP
    assert h % hp == 0
    est = hp * (2 * tq * dk * 2 + 2 * s * dk * 2 + 2 * V_HEAD_DIM * s * 2 + 2 * tq * V_HEAD_DIM * 2
                + V_HEAD_DIM * tq * 4 + 2 * tk * tq * (4 + 2)) + 4 * tk * tq * 4
    per_head = [pltpu.VMEM((V_HEAD_DIM, tq), F32), pltpu.VMEM((tk, tq), F32), pltpu.VMEM((tk, tq), F32),
                pltpu.VMEM((tk, tq), BF16), pltpu.VMEM((tk, tq), BF16)]
    assert len(per_head) == ATTN_SCRATCH_PER_HEAD
    return pl.pallas_call(
        functools.partial(_attn_kernel, tk=tk),
        grid=(b, h // hp, s // tq),
        in_specs=[
            pl.BlockSpec((None, hp, tq, dk), lambda bi, hi, i: (bi, hi, i, 0)),
            pl.BlockSpec((None, hp, s, dk), lambda bi, hi, i: (bi, hi, 0, 0)),
            pl.BlockSpec((None, hp, V_HEAD_DIM, s), lambda bi, hi, i: (bi, hi, 0, 0)),
        ],
        out_specs=pl.BlockSpec((None, tq, hp * V_HEAD_DIM), lambda bi, hi, i: (bi, i, hi)),
        out_shape=jax.ShapeDtypeStruct((b, s, h * V_HEAD_DIM), BF16),
        scratch_shapes=per_head * hp,
        compiler_params=_cparams(("parallel", "parallel", "arbitrary"), est),
        name="mla_attention",
    )(q, k, vt)


HALO = V7X_BF16_SUBLANES


def _with_halo(prev, main, nxt):
    i, n = pl.program_id(1), pl.num_programs(1)
    prev = jnp.where(i > 0, prev, jnp.zeros_like(prev))
    nxt = jnp.where(i < n - 1, nxt, jnp.zeros_like(nxt))
    return jnp.concatenate([prev, main, nxt], axis=0)


def _conv_rows(ext, w, bias, rows):
    taps = w.shape[0]
    acc = bias
    for kk in range(taps):
        off = HALO + kk - (taps - 1) // 2
        acc = acc + ext[off:off + rows] * w[kk:kk + 1]
    return acc


def _halo_specs(s, tm, width, col_block):
    per = tm // HALO
    last = s // HALO - 1

    def prev_map(bi, i, *rest):
        return (bi, jnp.maximum(i * per - 1, 0), col_block(*rest))

    def main_map(bi, i, *rest):
        return (bi, i, col_block(*rest))

    def next_map(bi, i, *rest):
        return (bi, jnp.minimum((i + 1) * per, last), col_block(*rest))

    return [
        pl.BlockSpec((None, HALO, width), prev_map),
        pl.BlockSpec((None, tm, width), main_map),
        pl.BlockSpec((None, HALO, width), next_map),
    ]


def _conv_silu_kernel(prev_ref, main_ref, next_ref, w_ref, b_ref, o_ref):
    ext = _with_halo(prev_ref[...], main_ref[...], next_ref[...]).astype(F32)
    o_ref[...] = _silu(_conv_rows(ext, w_ref[...], b_ref[...], o_ref.shape[0])).astype(o_ref.dtype)


def _ssd_conv(u3, conv_w, conv_b, tm, tc):
    b, s, _ = u3.shape
    tm = min(tm, s)
    n_c = SSD_CONV_DIM // tc
    first = U_XBC // tc
    est = 2 * (tm + 2 * HALO) * tc * 2 + 2 * tm * tc * 2 + 8 * (tm + 2 * HALO) * tc * 4
    return pl.pallas_call(
        _conv_silu_kernel,
        grid=(b, s // tm, n_c),
        in_specs=_halo_specs(s, tm, tc, lambda c: first + c) + [
            pl.BlockSpec((SSD_CONV, tc), lambda bi, i, c: (0, c)),
            pl.BlockSpec((1, tc), lambda bi, i, c: (0, c)),
        ],
        out_specs=pl.BlockSpec((None, tm, tc), lambda bi, i, c: (bi, i, c)),
        out_shape=jax.ShapeDtypeStruct((b, s, SSD_CONV_DIM), BF16),
        compiler_params=_cparams(("parallel", "parallel", "parallel"), est),
        name="ssd_conv_silu",
    )(u3, u3, u3, conv_w.astype(F32), conv_b.reshape(1, -1).astype(F32))


LOG_DT_FLOOR = -1e30


def _softplus(x):
    return jnp.maximum(x, 0.0) + jnp.log1p(jnp.exp(-jnp.abs(x)))


def _chunk_cumsums(a, pos_in_chunk):
    n = a.shape[1]
    cf, cb = a, a
    step = 1
    while step < SSD_CHUNK:
        cf = cf + jnp.where(pos_in_chunk >= step, pltpu.roll(cf, step, 1), 0.0)
        cb = cb + jnp.where(pos_in_chunk < SSD_CHUNK - step, pltpu.roll(cb, n - step, 1), 0.0)
        step *= 2
    return cf, cb


def _split3(x):
    hi = x.astype(BF16)
    rest = x - hi.astype(F32)
    mid = rest.astype(BF16)
    lo = (rest - mid.astype(F32)).astype(BF16)
    return hi, mid, lo


def _dt_prep_kernel(dt_ref, alog_ref, bias_ref, rowsel_ref, acs_ref, exp_ref, seg_ref, cd_ref):
    rows, s = dt_ref.shape
    dt = _softplus(dt_ref[...] + bias_ref[...])
    a = -jnp.exp(alog_ref[...]) * dt
    pos = lax.broadcasted_iota(jnp.int32, a.shape, 1) % SSD_CHUNK
    row = lax.broadcasted_iota(jnp.int32, a.shape, 0)
    cf, cb = _chunk_cumsums(a, pos)
    total = cf + cb - a
    acs = jnp.where(row < R, cf, cb)
    cd_ref[...] = jnp.exp(total)
    acs_pieces = [p.astype(F32) for p in _split3(acs)]
    ones_rows = jnp.where(row < PIECES, 1.0, 0.0)
    pad = jnp.zeros((V7X_LANES - (PIECES + 1) * rows, s), F32)
    acs_ref[...] = jnp.concatenate(acs_pieces + [ones_rows, pad], axis=0).T.astype(acs_ref.dtype)
    exp_pieces = [p.astype(F32) for table in (jnp.exp(acs), dt * jnp.exp(total - acs)) for p in _split3(table)]
    pad = jnp.zeros((V7X_LANES - 2 * PIECES * rows, s), F32)
    exp_ref[...] = jnp.concatenate(exp_pieces + [pad], axis=0).T.astype(exp_ref.dtype)
    src = acs - jnp.maximum(jnp.log(dt), LOG_DT_FLOOR)
    neg = jnp.concatenate([-p.astype(F32) for p in _split3(src)], axis=0).astype(BF16)
    seg_ref[...] = jnp.dot(rowsel_ref[...], neg, preferred_element_type=F32).astype(seg_ref.dtype)


def _dt_prep(dt_raw, a_log, dt_bias):
    b, g, rows, s = dt_raw.shape
    tile = V7X_BF16_SUBLANES
    rowsel = np.zeros((2 * N_PAIRS * 2 * tile, PIECES * rows), np.float32)
    for pair in range(N_PAIRS):
        for d in range(2):
            for half in range(2):
                for piece in range(PIECES):
                    rowsel[((2 * pair + d) * 2 + half) * tile + piece, piece * rows + R * d + 2 * pair + half] = 1.0
    rowsel = jnp.asarray(rowsel, BF16)
    vec = pl.BlockSpec((None, rows, 1), lambda bi, gi: (gi, 0, 0))
    per_group_rows = pl.BlockSpec((None, None, rows, s), lambda bi, gi: (bi, gi, 0, 0))
    per_group_cols = pl.BlockSpec((None, None, s, V7X_LANES), lambda bi, gi: (bi, gi, 0, 0))
    est = 4 * rows * s * 4 + 4 * s * V7X_LANES * 2 + 2 * rowsel.shape[0] * s * 2 + 24 * V7X_LANES * s * 4
    return pl.pallas_call(
        _dt_prep_kernel,
        grid=(b, g),
        in_specs=[per_group_rows, vec, vec, pl.BlockSpec(rowsel.shape, lambda bi, gi: (0, 0))],
        out_specs=[
            per_group_cols,
            per_group_cols,
            pl.BlockSpec((None, None, rowsel.shape[0], s), lambda bi, gi: (bi, gi, 0, 0)),
            per_group_rows,
        ],
        out_shape=[
            jax.ShapeDtypeStruct((b, g, s, V7X_LANES), BF16),
            jax.ShapeDtypeStruct((b, g, s, V7X_LANES), BF16),
            jax.ShapeDtypeStruct((b, g, rowsel.shape[0], s), BF16),
            jax.ShapeDtypeStruct((b, g, rows, s), F32),
        ],
        compiler_params=_cparams(("parallel", "parallel"), est),
        name="ssd_dt_prep",
    )(dt_raw, a_log, dt_bias, rowsel)


R = SSD_HEADS_PER_GROUP
N_PAIRS = R // 2
PIECES = 3
ACS_LANES = PIECES * 2 * R
EXPAND_EF, EXPAND_WST = 0, 1
SSD_BWD_CHUNKS_PER_STEP = 8
SSD_FWD_UNROLL = 4


def _ssd_constants():
    sel = np.zeros((2 * N_PAIRS, ACS_LANES, 2 * V7X_LANES), np.float32)
    expand = np.zeros((4, V7X_LANES, SSD_GROUP_WIDTH), np.float32)
    for piece in range(PIECES):
        for d in range(2):
            for pair in range(N_PAIRS):
                for half in range(2):
                    lane = 2 * R * piece + R * d + 2 * pair + half
                    sel[2 * pair + d, lane, half * V7X_LANES:(half + 1) * V7X_LANES] = 1.0
            for t in range(2):
                for r in range(R):
                    lane = ACS_LANES * t + 2 * R * piece + R * d + r
                    expand[2 * t + d, lane, r * SSD_HEAD_DIM:(r + 1) * SSD_HEAD_DIM] = 1.0
    return jnp.asarray(sel, BF16), jnp.asarray(expand, BF16)


def _ssd_kernel(xs_ref, b_ref, c_ref, z_ref, nw_ref, acs_ref, exp_ref, seg_ref, cd_ref, dskip_ref, sel_ref,
                expand_ref, gated_ref, ss_ref, hb_ref, state_ref):
    q = SSD_CHUNK
    n_chunks = xs_ref.shape[0] // q

    def transposed_b(r0):
        return b_ref[pl.ds(r0, q), :].astype(F32).T.astype(BF16)

    def expand_heads(table, which, direction):
        return jnp.dot(table, expand_ref[2 * which + direction], preferred_element_type=F32)

    state_ref[...] = jnp.zeros_like(state_ref)

    n_batch = math.gcd(n_chunks, SSD_BWD_CHUNKS_PER_STEP)

    def bwd_body(i, carry):
        chunks = [n_chunks - 1 - (n_batch * i + k) for k in range(n_batch)]
        starts = [pl.multiple_of(c * q, q) for c in chunks]
        weights = [expand_heads(exp_ref[pl.ds(r0, q), :], EXPAND_WST, 1) for r0 in starts]
        xdws = [(xs_ref[pl.ds(r0, q), :].astype(F32) * w).astype(BF16) for r0, w in zip(starts, weights)]
        contribs = [jnp.dot(transposed_b(r0), xdw, preferred_element_type=F32)
                    for r0, xdw in zip(starts, xdws)]
        state = state_ref[...]
        for c, contrib in zip(chunks, contribs):
            hb_ref[c] = state.astype(hb_ref.dtype)
            state = state * cd_ref[pl.ds(n_chunks + c, 1), :] + contrib
        state_ref[...] = state
        return carry

    lax.fori_loop(0, n_chunks // n_batch, bwd_body, 0)

    state_ref[...] = jnp.zeros_like(state_ref)
    li = lax.broadcasted_iota(jnp.int32, (q, 2 * q), 0)
    si = lax.broadcasted_iota(jnp.int32, (q, 2 * q), 1) % q
    lower, upper = li >= si, li <= si
    lane_low = lax.broadcasted_iota(jnp.int32, (q, V7X_LANES), 1) < SSD_HEAD_DIM
    unused_rows = jnp.zeros((V7X_LANES - ACS_LANES - V7X_BF16_SUBLANES, 2 * q), BF16)

    def fwd_body(c, carry):
        r0 = pl.multiple_of(c * q, q)
        x_bf = xs_ref[pl.ds(r0, q), :]
        x = x_bf.astype(F32)
        bm = b_ref[pl.ds(r0, q), :]
        cm = c_ref[pl.ds(r0, q), :]
        acs_tab = acs_ref[pl.ds(r0, q), :]
        exp_tab = exp_ref[pl.ds(r0, q), :]
        cb = lax.dot_general(cm, bm, (((1,), (1,)), ((), ())), preferred_element_type=F32)
        cb2 = jnp.concatenate([cb, cb], axis=1)

        def pair_exponents(pair, d):
            k = 2 * pair + d
            tile = jnp.concatenate([seg_ref[k, 0, :, pl.ds(r0, q)], seg_ref[k, 1, :, pl.ds(r0, q)]], axis=1)
            rhs = jnp.concatenate([sel_ref[k], tile, unused_rows], axis=0)
            return jnp.dot(acs_tab, rhs, preferred_element_type=F32)

        segs = [(pair_exponents(pair, 0), pair_exponents(pair, 1)) for pair in range(N_PAIRS)]
        h_prev = state_ref[...]
        off_f = jnp.dot(cm, h_prev.astype(BF16), preferred_element_type=F32)
        off_b = jnp.dot(cm, hb_ref[c], preferred_element_type=F32)
        ef_f = expand_heads(exp_tab, EXPAND_EF, 0)
        ef_b = expand_heads(exp_tab, EXPAND_EF, 1)
        wst_f = expand_heads(exp_tab, EXPAND_WST, 0)

        diag = []
        for pair in range(N_PAIRS):
            xp = x_bf[:, pair * V7X_LANES:(pair + 1) * V7X_LANES]
            zero = jnp.zeros_like(xp)
            rhs = jnp.concatenate([jnp.where(lane_low, xp, zero), jnp.where(lane_low, zero, xp)], axis=0)
            lf = jnp.where(lower, jnp.exp(segs[pair][0]), 0.0)
            lb = jnp.where(upper, jnp.exp(segs[pair][1]), 0.0)
            lhs = (cb2 * (lf + lb)).astype(BF16)
            diag.append(jnp.dot(lhs, rhs, preferred_element_type=F32))
        y = jnp.concatenate(diag, axis=1)
        y = y + off_f * ef_f + off_b * ef_b + x * dskip_ref[...]
        yz = y * _silu(z_ref[pl.ds(r0, q), :].astype(F32))
        ss_ref[pl.ds(r0, q), :] = jnp.sum(yz * yz, axis=1, keepdims=True)
        gated_ref[pl.ds(r0, q), :] = (yz * nw_ref[...]).astype(gated_ref.dtype)

        xdw = (x * wst_f).astype(BF16)
        contrib = jnp.dot(transposed_b(r0), xdw, preferred_element_type=F32)
        state_ref[...] = h_prev * cd_ref[pl.ds(c, 1), :] + contrib
        return carry

    lax.fori_loop(0, n_chunks, fwd_body, 0, unroll=math.gcd(n_chunks, SSD_FWD_UNROLL))


def _ssd_scan(xbc_c, u3, norm_w, acs_tab, exp_tab, seg_rows, cd, dskip):
    b, s, _ = xbc_c.shape
    n_chunks = s // SSD_CHUNK
    gw = SSD_GROUP_WIDTH
    b_first = SSD_D_INNER // SSD_STATE
    c_first = b_first + SSD_GROUPS
    sel, expand = _ssd_constants()
    seg_block = seg_rows.shape[2:]
    est = (3 * 2 * s * gw * 2 + 2 * s * V7X_LANES * 4 + 4 * s * SSD_STATE * 2 + 4 * s * V7X_LANES * 2
           + 2 * 2 * R * s * 4
           + 2 * int(np.prod(seg_block)) * 2 + 2 * (sel.size + expand.size) * 2
           + n_chunks * SSD_STATE * gw * 2 + 4 * n_chunks * gw * 4 + 64 * SSD_CHUNK * gw * 4)
    return pl.pallas_call(
        _ssd_kernel,
        grid=(b, SSD_GROUPS),
        in_specs=[
            pl.BlockSpec((None, s, gw), lambda bi, g: (bi, 0, g)),
            pl.BlockSpec((None, s, SSD_STATE), lambda bi, g: (bi, 0, b_first + g)),
            pl.BlockSpec((None, s, SSD_STATE), lambda bi, g: (bi, 0, c_first + g)),
            pl.BlockSpec((None, s, gw), lambda bi, g: (bi, 0, U_Z // gw + g)),
            pl.BlockSpec((1, gw), lambda bi, g: (0, g)),
            pl.BlockSpec((None, None, s, V7X_LANES), lambda bi, g: (bi, g, 0, 0)),
            pl.BlockSpec((None, None, s, V7X_LANES), lambda bi, g: (bi, g, 0, 0)),
            pl.BlockSpec((None, None) + seg_block, lambda bi, g: (bi, g, 0, 0, 0, 0)),
            pl.BlockSpec((None, None, 2 * n_chunks, gw), lambda bi, g: (bi, g, 0, 0)),
            pl.BlockSpec((None, 1, gw), lambda bi, g: (g, 0, 0)),
            pl.BlockSpec(sel.shape, lambda bi, g: (0, 0, 0)),
            pl.BlockSpec(expand.shape, lambda bi, g: (0, 0, 0)),
        ],
        out_specs=[
            pl.BlockSpec((None, s, gw), lambda bi, g: (bi, 0, g)),
            pl.BlockSpec((None, None, s, 1), lambda bi, g: (bi, g, 0, 0)),
        ],
        out_shape=[
            jax.ShapeDtypeStruct((b, s, SSD_D_INNER), BF16),
            jax.ShapeDtypeStruct((b, SSD_GROUPS, s, 1), F32),
        ],
        scratch_shapes=[
            pltpu.VMEM((n_chunks, SSD_STATE, gw), BF16),
            pltpu.VMEM((SSD_STATE, gw), F32),
        ],
        compiler_params=_cparams(("parallel", "parallel"), est),
        name="ssd_scan",
    )(xbc_c, xbc_c, xbc_c, u3, norm_w.reshape(1, -1).astype(F32), acs_tab, exp_tab, seg_rows, cd, dskip,
      sel, expand)


def _mix_kernel(o_ref, gated_ref, ss_ref, wa_ref, ws_ref, ga_ref, gs_ref, m_ref):
    attn = jnp.dot(o_ref[...], wa_ref[...], preferred_element_type=F32)
    ssd = jnp.dot(gated_ref[...], ws_ref[...], preferred_element_type=F32)
    mean_sq = jnp.sum(ss_ref[...], axis=1, keepdims=True) * (1.0 / gated_ref.shape[1])
    ssd = ssd * lax.rsqrt(mean_sq + EPS)
    mixed = _sigmoid(ga_ref[...].astype(F32)) * attn + _sigmoid(gs_ref[...].astype(F32)) * ssd
    m_ref[...] = mixed.astype(m_ref.dtype)


def _mix(o2, gated2, ss2, u2, wa, ws, tm, tn):
    t = o2.shape[0]
    tm = min(tm, t)
    d_attn, d_ssd = o2.shape[1], gated2.shape[1]
    est = (2 * tm * (d_attn + d_ssd) * 2 + 2 * tm * V7X_LANES * 4 + 2 * (d_attn + d_ssd) * tn * 2
           + 6 * tm * tn * 2 + 6 * tm * tn * 4)
    return pl.pallas_call(
        _mix_kernel,
        grid=(t // tm, D_MODEL // tn),
        in_specs=[
            pl.BlockSpec((tm, d_attn), lambda i, j: (i, 0)),
            pl.BlockSpec((tm, d_ssd), lambda i, j: (i, 0)),
            pl.BlockSpec((tm, ss2.shape[1]), lambda i, j: (i, 0)),
            pl.BlockSpec((d_attn, tn), lambda i, j: (0, j)),
            pl.BlockSpec((d_ssd, tn), lambda i, j: (0, j)),
            pl.BlockSpec((tm, tn), lambda i, j: (i, U_GA // tn + j)),
            pl.BlockSpec((tm, tn), lambda i, j: (i, U_GS // tn + j)),
        ],
        out_specs=pl.BlockSpec((tm, tn), lambda i, j: (i, j)),
        out_shape=jax.ShapeDtypeStruct((t, D_MODEL), BF16),
        compiler_params=_cparams(("parallel", "parallel"), est),
        name="branch_mix",
    )(o2, gated2, ss2, wa, ws, u2, u2)


def _residual_matmul_kernel(a_ref, w_ref, x_ref, o_ref):
    o_ref[...] = x_ref[...] + jnp.dot(a_ref[...], w_ref[...], preferred_element_type=F32)


def _residual_matmul(a, w, x, tm, tn):
    t, k = a.shape
    n = w.shape[1]
    tm, tn = min(tm, t), min(tn, n)
    est = 2 * tm * k * 2 + 2 * k * tn * 2 + 4 * tm * tn * 4
    return pl.pallas_call(
        _residual_matmul_kernel,
        grid=(t // tm, n // tn),
        in_specs=[
            pl.BlockSpec((tm, k), lambda i, j: (i, 0)),
            pl.BlockSpec((k, tn), lambda i, j: (0, j)),
            pl.BlockSpec((tm, tn), lambda i, j: (i, j)),
        ],
        out_specs=pl.BlockSpec((tm, tn), lambda i, j: (i, j)),
        out_shape=jax.ShapeDtypeStruct((t, n), F32),
        compiler_params=_cparams(("parallel", "parallel"), est),
        name="residual_matmul",
    )(a, w, x)


def _ffn_kernel(prev_ref, main_ref, next_ref, g_ref, wg_ref, wv_ref, cwg_ref, cwv_ref, cbg_ref, cbv_ref,
                wd_ref, gf_ref, o_ref, hn_ref):
    j, n_j = pl.program_id(2), pl.num_programs(2)
    tm = main_ref.shape[0]

    @pl.when(j == 0)
    def _():
        ext = _with_halo(prev_ref[...], main_ref[...], next_ref[...])
        hn_ref[...] = (ext * _rms_scale(ext) * g_ref[...]).astype(hn_ref.dtype)
        o_ref[...] = main_ref[...]

    hn = hn_ref[...]
    gate = _conv_rows(jnp.dot(hn, wg_ref[...], preferred_element_type=F32), cwg_ref[...], cbg_ref[...], tm)
    val = _conv_rows(jnp.dot(hn, wv_ref[...], preferred_element_type=F32), cwv_ref[...], cbv_ref[...], tm)
    act = (_silu(gate) * val).astype(BF16)
    o_ref[...] += jnp.dot(act, wd_ref[...], preferred_element_type=F32)

    @pl.when(j == n_j - 1)
    def _():
        h = o_ref[...]
        o_ref[...] = h * _rms_scale(h) * gf_ref[...]


def _ffn(h3, gain, w_up, conv_w, conv_b, w_down, final_gain, tm, tf):
    b, s, d = h3.shape
    tm = min(tm, s)
    n_f = FFN_DIM // tf
    row = lambda bi, i, j: (0, 0)
    est = (2 * (tm + 2 * HALO) * d * 4 + (tm + 2 * HALO) * d * 2 + 2 * 2 * d * tf * 2 + 2 * tf * d * 2
           + 2 * tm * d * 4 + 8 * (tm + 2 * HALO) * tf * 4)
    return pl.pallas_call(
        _ffn_kernel,
        grid=(b, s // tm, n_f),
        in_specs=_halo_specs(s, tm, d, lambda j: 0) + [
            pl.BlockSpec((1, d), row),
            pl.BlockSpec((d, tf), lambda bi, i, j: (0, j)),
            pl.BlockSpec((d, tf), lambda bi, i, j: (0, n_f + j)),
            pl.BlockSpec((FFN_CONV, tf), lambda bi, i, j: (0, j)),
            pl.BlockSpec((FFN_CONV, tf), lambda bi, i, j: (0, n_f + j)),
            pl.BlockSpec((1, tf), lambda bi, i, j: (0, j)),
            pl.BlockSpec((1, tf), lambda bi, i, j: (0, n_f + j)),
            pl.BlockSpec((tf, d), lambda bi, i, j: (j, 0)),
            pl.BlockSpec((1, d), row),
        ],
        out_specs=pl.BlockSpec((None, tm, d), lambda bi, i, j: (bi, i, 0)),
        out_shape=jax.ShapeDtypeStruct((b, s, d), F32),
        scratch_shapes=[pltpu.VMEM((tm + 2 * HALO, d), BF16)],
        compiler_params=_cparams(("parallel", "parallel", "arbitrary"), est),
        name="conv_ffn",
    )(h3, h3, h3, gain.reshape(1, -1).astype(F32), w_up, w_up, conv_w.astype(F32), conv_w.astype(F32),
      conv_b.reshape(1, -1).astype(F32), conv_b.reshape(1, -1).astype(F32), w_down,
      final_gain.reshape(1, -1).astype(F32))


def _swap_halves(w):
    half = w.shape[-1] // 2
    return jnp.concatenate([w[..., half:], w[..., :half]], axis=-1)


def _layer(h, cos_t, sin_t, norm_mix_w, w_in, q_norm_w, w_uq, kv_norm_w, w_ukv, w_o_attn, ssd_conv_w,
           ssd_conv_b, a_log_fwd, a_log_bwd, dt_bias_fwd, dt_bias_bwd, ssd_d, ssd_norm_w, w_o_ssd, w_out):
    b, s, d = h.shape
    t = b * s
    n_chunks = s // SSD_CHUNK
    x2 = h.reshape(t, d)

    w_in_stack, layer = w_in
    w_in_t_stack = jnp.swapaxes(w_in_stack, 1, 2)
    assert w_in_t_stack.shape[1] == IN_WIDTH
    w_t = w_in_t_stack[layer]
    half = QK_ROPE_DIM // 2
    w_small_t = jnp.concatenate([w_t[IN_QL:IN_Z], w_t[IN_KR + half:IN_Z], w_t[IN_KR:IN_KR + half],
                                 w_t[IN_DTF:IN_GA]], axis=0)
    w_small_t = lax.optimization_barrier(w_small_t).astype(BF16)
    u2 = _norm_matmul(x2, norm_mix_w, _main_in_weights(w_in_t_stack, layer), BF16, tm=1024, tn=1024)
    small2 = _norm_matmul(x2, norm_mix_w, w_small_t, F32, tm=1024, tn=SM_WIDTH)
    u3 = u2.reshape(b, s, U_WIDTH)
    small3 = small2.reshape(b, s, SM_WIDTH)

    wq = w_uq.reshape(Q_LORA_RANK, MLA_HEADS, QK_NOPE_DIM + QK_ROPE_DIM)
    wq_rope = wq[..., QK_NOPE_DIM:]
    wq = jnp.concatenate([wq[..., :QK_NOPE_DIM], wq_rope, _swap_halves(wq_rope)], axis=-1)
    wq = wq.transpose(1, 0, 2).astype(BF16)
    wkv = w_ukv.reshape(KV_LORA_RANK, MLA_HEADS, QK_NOPE_DIM + V_HEAD_DIM).transpose(1, 0, 2).astype(BF16)
    q, k, vt = _mla_projections(small3, cos_t, sin_t, q_norm_w, wq, kv_norm_w, wkv, tm=512)
    o = _attention(q, k, vt, tq=1024, tk=1024)

    xbc_c = _ssd_conv(u3, ssd_conv_w, ssd_conv_b, tm=512, tc=512)
    g_ = SSD_GROUPS

    def per_group(v):
        lead = v.shape[:-1]
        v = jnp.moveaxis(v.reshape(lead + (2, g_, R)), -2, 0)
        return v.reshape((g_,) + lead + (2 * R,))

    dt_raw = per_group(small3[:, :, SM_DT:]).transpose(1, 0, 3, 2)
    a_log = per_group(jnp.concatenate([a_log_fwd, a_log_bwd]).astype(F32))[..., None]
    dt_bias = per_group(jnp.concatenate([dt_bias_fwd, dt_bias_bwd]).astype(F32))[..., None]
    acs_tab, exp_tab, seg_rows, cd = _dt_prep(dt_raw, a_log, dt_bias)
    seg_rows = seg_rows.reshape(b, g_, 2 * N_PAIRS, 2, V7X_BF16_SUBLANES, s)
    cd = cd[..., ::SSD_CHUNK].reshape(b, g_, 2, R, n_chunks).transpose(0, 1, 2, 4, 3)
    cd = jnp.repeat(cd, SSD_HEAD_DIM, axis=-1).reshape(b, g_, 2 * n_chunks, SSD_GROUP_WIDTH)
    dskip = jnp.repeat(ssd_d.astype(F32), SSD_HEAD_DIM).reshape(g_, 1, SSD_GROUP_WIDTH)
    gated, ss = _ssd_scan(xbc_c, u3, ssd_norm_w, acs_tab, exp_tab, seg_rows, cd, dskip)
    ss2 = ss.reshape(b, g_, s).transpose(0, 2, 1).reshape(t, g_)

    mixed = _mix(o.reshape(t, -1), gated.reshape(t, -1), ss2, u2, w_o_attn.astype(BF16),
                 w_o_ssd.astype(BF16), tm=1024, tn=512)
    h1 = _residual_matmul(mixed, w_out.astype(BF16), x2, tm=1024, tn=1024)
    return h1.reshape(b, s, d)


def kernel(x, positions, norm_mix_w, w_in, q_norm_w, w_uq, kv_norm_w, w_ukv, w_o_attn, ssd_conv_w, ssd_conv_b,
           a_log_fwd, a_log_bwd, dt_bias_fwd, dt_bias_bwd, ssd_d, ssd_norm_w, w_o_ssd, w_out, norm_ffn_w,
           ffn_w_up, ffn_conv_w, ffn_conv_b, ffn_w_down, norm_final_w):
    depth = w_in.shape[0]
    assert depth == 1, "the conv-gated MLP kernel fuses the final RMSNorm, which is only valid for one layer"
    half = QK_ROPE_DIM // 2
    inv_freq = ROPE_THETA ** (-jnp.arange(half, dtype=F32) / half)
    ang = positions.astype(F32)[..., None] * inv_freq
    cos, sin = jnp.cos(ang), jnp.sin(ang)
    pad = jnp.zeros(cos.shape[:-1] + (V7X_LANES - QK_ROPE_DIM,), F32)
    cos_t = jnp.concatenate([cos, cos, pad], axis=-1)
    sin_t = jnp.concatenate([-sin, sin, pad], axis=-1)

    l = 0
    h1 = _layer(x, cos_t, sin_t, norm_mix_w[l], (w_in, l), q_norm_w[l], w_uq[l], kv_norm_w[l], w_ukv[l],
                w_o_attn[l], ssd_conv_w[l], ssd_conv_b[l], a_log_fwd[l], a_log_bwd[l], dt_bias_fwd[l],
                dt_bias_bwd[l], ssd_d[l], ssd_norm_w[l], w_o_ssd[l], w_out[l])
    return _ffn(h1, norm_ffn_w[l], ffn_w_up[l].astype(BF16), ffn_conv_w[l], ffn_conv_b[l],
                ffn_w_down[l].astype(BF16), norm_final_w, tm=512, tf=512)
```
